```python
import math
import jax
import jax.numpy as jnp
from jax import lax
import numpy as np

D_MODEL = 1024
BATCH = 8
SEQ = 2048
DEPTH = 4
DEC_BATCH = 128
DEC_SEQ = 8
PAST_LEN = 16384
PAGE_SIZE = 128

EPS = 1e-6
D_A = D_MODEL
G_A = 8
DG_A = D_A // G_A
CHUNK_A = 128
EXPAND_B = 2
D_INNER_B = EXPAND_B * D_MODEL
P_B = 64
H_B = D_INNER_B // P_B
G_B = 8
R_B = H_B // G_B
N_B = 128
CONV_W = 4
CONV_DIM = D_INNER_B + 2 * G_B * N_B
CHUNK_B = 128
IN_DIM = 2 * D_A + D_INNER_B + CONV_DIM + H_B + 2 * D_MODEL
N_EXPERTS = 32
TOP_K = 4
D_FF = D_MODEL
SWIGLU_LIMIT = 7.0
SWIGLU_ALPHA = 1.702
MOE_BLOCK = 128

kernel_name = "hybrid_gmlp_ssd_moe_adaln_step"


def rms_norm(x, g):
    xf = x.astype(jnp.float32)
    xf = xf * lax.rsqrt(jnp.mean(xf * xf, axis=-1, keepdims=True) + EPS)
    return xf.astype(x.dtype) * g


def group_rms_norm(x, g, groups):
    shp = x.shape
    xf = x.astype(jnp.float32).reshape(shp[:-1] + (groups, shp[-1] // groups))
    xf = xf * lax.rsqrt(jnp.mean(xf * xf, axis=-1, keepdims=True) + EPS)
    return xf.reshape(shp).astype(x.dtype) * g


def layer_norm(x, g, b):
    xf = x.astype(jnp.float32)
    mu = jnp.mean(xf, axis=-1, keepdims=True)
    var = jnp.mean(jnp.square(xf - mu), axis=-1, keepdims=True)
    return ((xf - mu) * lax.rsqrt(var + EPS)).astype(x.dtype) * g + b


def chunk_gmlp(u, v, ln_g, ln_b, w_s, b_s):
    bsz, L, _ = v.shape
    u = jax.nn.gelu(u, approximate=False)
    v = layer_norm(jax.nn.gelu(v, approximate=False), ln_g, ln_b)
    n_chunks = -(-L // CHUNK_A)
    pad = n_chunks * CHUNK_A - L
    vc = jnp.pad(v, ((0, 0), (0, pad), (0, 0))).reshape(bsz, n_chunks, CHUNK_A, G_A, DG_A)
    causal = jnp.tril(jnp.ones((CHUNK_A, CHUNK_A), dtype=bool))
    ws = jnp.where(causal[None], w_s, 0)
    s = jnp.einsum("gts,bcsgd->bctgd", ws, vc) + b_s.T[None, None, :, :, None]
    s = s.reshape(bsz, n_chunks * CHUNK_A, D_A)[:, :L]
    return u * s, v


def ssd_scan(xdt, a, bm, cm, h0, chunk):
    bsz, L = xdt.shape[:2]
    nc = L // chunk
    xdt = xdt.reshape(bsz, nc, chunk, G_B, R_B, P_B)
    a = a.reshape(bsz, nc, chunk, G_B, R_B)
    bm = bm.reshape(bsz, nc, chunk, G_B, N_B)
    cm = cm.reshape(bsz, nc, chunk, G_B, N_B)
    a_cum = jnp.cumsum(a, axis=2)
    causal = jnp.tril(jnp.ones((chunk, chunk), dtype=bool))[None, None, :, :, None, None]
    seg = a_cum[:, :, :, None] - a_cum[:, :, None, :]
    decay = jnp.exp(jnp.where(causal, seg, -jnp.inf))
    cb = jnp.einsum("bclgn,bcsgn->bclsg", cm, bm)
    y_diag = jnp.einsum("bclsg,bclsgr,bcsgrp->bclgrp", cb, decay, xdt)
    to_end = jnp.exp(a_cum[:, :, -1:] - a_cum)
    chunk_states = jnp.einsum("bclgn,bclgr,bclgrp->bcgrpn", bm, to_end, xdt)
    chunk_decay = jnp.exp(a_cum[:, :, -1])

    def carry_step(h, inp):
        st, dec = inp
        return dec[..., None, None] * h + st, h

    h_final, h_prev = lax.scan(carry_step, h0,
                               (jnp.moveaxis(chunk_states, 1, 0), jnp.moveaxis(chunk_decay, 1, 0)))
    h_prev = jnp.moveaxis(h_prev, 0, 1)
    y_off = jnp.einsum("bclgn,bcgrpn,bclgr->bclgrp", cm, h_prev, jnp.exp(a_cum))
    return (y_diag + y_off).reshape(bsz, L, G_B, R_B, P_B), h_final


def ssd_mixer(z, xbc, dt, conv_state, ssm_state, conv_w, conv_b, dt_bias, a_log, d_skip, norm_g):
    bsz, L, _ = xbc.shape
    full = jnp.concatenate([conv_state.astype(xbc.dtype), xbc], axis=1)
    new_conv = full[:, full.shape[1] - (CONV_W - 1):]
    conv = conv_b
    for k in range(CONV_W):
        conv = conv + full[:, k:k + L] * conv_w[k]
    xbc = jax.nn.silu(conv)
    xs, bm, cm = jnp.split(xbc, [D_INNER_B, D_INNER_B + G_B * N_B], axis=-1)
    xs = xs.reshape(bsz, L, G_B, R_B, P_B).astype(jnp.float32)
    bm = bm.reshape(bsz, L, G_B, N_B).astype(jnp.float32)
    cm = cm.reshape(bsz, L, G_B, N_B).astype(jnp.float32)
    dtp = jax.nn.softplus((dt + dt_bias).astype(jnp.float32)).reshape(bsz, L, G_B, R_B)
    a = -jnp.exp(a_log.astype(jnp.float32)).reshape(G_B, R_B)
    h0 = ssm_state.reshape(bsz, G_B, R_B, P_B, N_B).astype(jnp.float32)
    y, h_final = ssd_scan(xs * dtp[..., None], dtp * a, bm, cm, h0, math.gcd(L, CHUNK_B))
    y = y + d_skip.astype(jnp.float32).reshape(G_B, R_B)[:, :, None] * xs
    y = y.reshape(bsz, L, D_INNER_B).astype(z.dtype) * jax.nn.silu(z)
    y = group_rms_norm(y, norm_g, G_B)
    return y, new_conv, h_final.reshape(bsz, H_B, P_B, N_B).astype(ssm_state.dtype)


def moe(h, router_w, router_b, w_gu, b_gu, w_dn, b_dn):
    bsz, L, D = h.shape
    t = h.reshape(-1, D)
    T = t.shape[0]
    logits = (t @ router_w + router_b).astype(jnp.float32)
    top_v, top_i = lax.top_k(logits, TOP_K)
    gates = jax.nn.softmax(top_v, axis=-1).astype(h.dtype)
    flat_e = top_i.reshape(-1)
    TK = T * TOP_K
    order = jnp.argsort(flat_e, stable=True)
    sorted_e = flat_e[order]
    counts = jnp.bincount(flat_e, length=N_EXPERTS)
    padded = (counts + MOE_BLOCK - 1) // MOE_BLOCK * MOE_BLOCK
    pad_end = jnp.cumsum(padded)
    pad_start = pad_end - padded
    grp_start = jnp.cumsum(counts) - counts
    rank = jnp.arange(TK, dtype=jnp.int32) - grp_start[sorted_e]
    dest = pad_start[sorted_e] + rank
    n_blocks = -(-TK // MOE_BLOCK) + N_EXPERTS
    rows = n_blocks * MOE_BLOCK
    row_tok = jnp.full((rows,), T, dtype=jnp.int32).at[dest].set((order // TOP_K).astype(jnp.int32))
    blk_start = jnp.arange(n_blocks, dtype=pad_end.dtype) * MOE_BLOCK
    blk_e = jnp.minimum(jnp.searchsorted(pad_end, blk_start, side="right"), N_EXPERTS - 1)
    t_pad = jnp.concatenate([t, jnp.zeros((1, D), t.dtype)], axis=0)
    xb = t_pad[row_tok].reshape(n_blocks, MOE_BLOCK, D)

    def expert_block(args):
        xblk, e = args
        gu = xblk @ w_gu[e] + b_gu[e]
        g, u = gu[:, :D_FF], gu[:, D_FF:]
        g = jnp.minimum(g, SWIGLU_LIMIT)
        u = jnp.clip(u, -SWIGLU_LIMIT, SWIGLU_LIMIT)
        act = g * jax.nn.sigmoid(SWIGLU_ALPHA * g) * (u + 1)
        return act @ w_dn[e] + b_dn[e]

    yb = lax.map(expert_block, (xb, blk_e)).reshape(rows, D)
    y_sorted = yb[dest]
    y_assign = jnp.zeros((TK, D), yb.dtype).at[order].set(y_sorted).reshape(T, TOP_K, D)
    out = jnp.einsum("tkd,tk->td", y_assign, gates)
    return out.reshape(bsz, L, D)


def decoder_layer(x, c, conv_state, ssm_state, p):
    (n1, n2, w_ada, b_ada, w_in, a_ln_g, a_ln_b, a_ws, a_bs, conv_w, conv_b, dt_bias, a_log,
     d_skip, b_norm_g, w_pa, w_pb, w_o, r_w, r_b, w_gu, b_gu, w_dn, b_dn) = p
    mod = jax.nn.silu(c) @ w_ada + b_ada
    sh1, sc1, gt1, sh2, sc2, gt2 = [m[:, None, :] for m in jnp.split(mod, 6, axis=-1)]
    h = rms_norm(x, n1) * (1 + sc1) + sh1
    proj = h @ w_in
    cuts = np.cumsum([D_A, D_A, D_INNER_B, CONV_DIM, H_B, D_MODEL]).tolist()
    u, v, z, xbc, dt, g_a, g_b = jnp.split(proj, cuts, axis=-1)
    o_a, v_rows = chunk_gmlp(u, v, a_ln_g, a_ln_b, a_ws, a_bs)
    o_b, new_conv, new_ssm = ssd_mixer(z, xbc, dt, conv_state, ssm_state, conv_w, conv_b,
                                       dt_bias, a_log, d_skip, b_norm_g)
    m = jax.nn.sigmoid(g_a) * (o_a @ w_pa) + jax.nn.sigmoid(g_b) * (o_b @ w_pb)
    x = x + gt1 * (m @ w_o)
    h2 = rms_norm(x, n2) * (1 + sc2) + sh2
    x = x + gt2 * moe(h2, r_w, r_b, w_gu, b_gu, w_dn, b_dn)
    return x, v_rows, new_conv, new_ssm


def setup_inputs(seed: int = 0) -> dict:
    key = jax.random.key(seed)
    ks = iter(jax.random.split(key, 40))
    f32 = jnp.float32

    def nrm(shape, scale):
        return jax.random.normal(next(ks), shape, f32) * scale

    def gain(shape):
        return 1.0 + nrm(shape, 0.02)

    dt0 = jnp.exp(jax.random.uniform(next(ks), (DEPTH, H_B), f32)
                  * (math.log(0.1) - math.log(0.001)) + math.log(0.001))
    return {
        "x_prompt": nrm((BATCH, SEQ, D_MODEL), 1.0),
        "x_sample": nrm((DEC_BATCH, DEC_SEQ, D_MODEL), 1.0),
        "c_prompt": nrm((BATCH, D_MODEL), 1.0),
        "c_sample": nrm((DEC_BATCH, D_MODEL), 1.0),
        "state_ssm": nrm((DEPTH, DEC_BATCH, H_B, P_B, N_B), 0.5),
        "state_conv": nrm((DEPTH, DEC_BATCH, CONV_W - 1, CONV_DIM), 1.0),
        "norm1_g": gain((DEPTH, D_MODEL)),
        "norm2_g": gain((DEPTH, D_MODEL)),
        "final_g": gain((D_MODEL,)),
        "w_ada": nrm((DEPTH, D_MODEL, 6 * D_MODEL), 0.2 * D_MODEL ** -0.5),
        "b_ada": nrm((DEPTH, 6 * D_MODEL), 0.02),
        "w_in": nrm((DEPTH, D_MODEL, IN_DIM), D_MODEL ** -0.5),
        "a_ln_g": gain((DEPTH, D_A)),
        "a_ln_b": nrm((DEPTH, D_A), 0.02),
        "a_ws": nrm((DEPTH, G_A, CHUNK_A, CHUNK_A), CHUNK_A ** -0.5),
        "a_bs": gain((DEPTH, G_A, CHUNK_A)),
        "b_conv_w": nrm((DEPTH, CONV_W, CONV_DIM), CONV_W ** -0.5),
        "b_conv_b": nrm((DEPTH, CONV_DIM), 0.02),
        "b_dt_bias": dt0 + jnp.log(-jnp.expm1(-dt0)),
        "b_a_log": jnp.log(jax.random.uniform(next(ks), (DEPTH, H_B), f32, 1.0, 16.0)),
        "b_d": gain((DEPTH, H_B)),
        "b_norm_g": gain((DEPTH, D_INNER_B)),
        "w_proj_a": nrm((DEPTH, D_A, D_MODEL), D_A ** -0.5),
        "w_proj_b": nrm((DEPTH, D_INNER_B, D_MODEL), D_INNER_B ** -0.5),
        "w_out": nrm((DEPTH, D_MODEL, D_MODEL), D_MODEL ** -0.5),
        "router_w": nrm((DEPTH, D_MODEL, N_EXPERTS), D_MODEL ** -0.5),
        "router_b": nrm((DEPTH, N_EXPERTS), 0.01),
        "w_gu": nrm((DEPTH, N_EXPERTS, D_MODEL, 2 * D_FF), D_MODEL ** -0.5),
        "b_gu": nrm((DEPTH, N_EXPERTS, 2 * D_FF), 0.02),
        "w_dn": nrm((DEPTH, N_EXPERTS, D_FF, D_MODEL), D_FF ** -0.5),
        "b_dn": nrm((DEPTH, N_EXPERTS, D_MODEL), 0.02),
    }


def reference(x_prompt, x_sample, c_prompt, c_sample, state_ssm, state_conv, norm1_g, norm2_g,
              final_g, w_ada, b_ada, w_in, a_ln_g, a_ln_b, a_ws, a_bs, b_conv_w, b_conv_b,
              b_dt_bias, b_a_log, b_d, b_norm_g, w_proj_a, w_proj_b, w_out, router_w, router_b,
              w_gu, b_gu, w_dn, b_dn):
    xp, xs = x_prompt, x_sample
    bp = x_prompt.shape[0]
    conv0 = jnp.zeros((bp, CONV_W - 1, CONV_DIM), x_prompt.dtype)
    ssm0 = jnp.zeros((bp, H_B, P_B, N_B), x_prompt.dtype)
    ssm_p, conv_p, ssm_s, conv_s, v_s = [], [], [], [], []
    for l in range(DEPTH):
        p = (norm1_g[l], norm2_g[l], w_ada[l], b_ada[l], w_in[l], a_ln_g[l], a_ln_b[l], a_ws[l],
             a_bs[l], b_conv_w[l], b_conv_b[l], b_dt_bias[l], b_a_log[l], b_d[l], b_norm_g[l],
             w_proj_a[l], w_proj_b[l], w_out[l], router_w[l], router_b[l], w_gu[l], b_gu[l],
             w_dn[l], b_dn[l])
        xp, _, cp, sp = decoder_layer(xp, c_prompt, conv0, ssm0, p)
        xs, vs, cs, ss = decoder_layer(xs, c_sample, state_conv[l], state_ssm[l], p)
        ssm_p.append(sp)
        conv_p.append(cp)
        ssm_s.append(ss)
        conv_s.append(cs)
        v_s.append(vs)
    y_prompt = rms_norm(xp, final_g)
    y_sample = rms_norm(xs, final_g)
    return (y_prompt, y_sample, jnp.stack(ssm_p), jnp.stack(conv_p), jnp.stack(ssm_s),
            jnp.stack(conv_s), jnp.stack(v_s))
```

```python
import functools

import jax
import jax.numpy as jnp
from jax import lax
from jax.experimental import pallas as pl
from jax.experimental.pallas import tpu as pltpu

F32 = jnp.float32
BF16 = jnp.bfloat16

EPS = 1e-6
SUBLANES = 8
LANES = 128
CHUNK = 128
G_A = 8
G_B = 8
R_B = 4
P_B = 64
N_B = 128
CONV_W = 4
TOP_K = 4
SWIGLU_LIMIT = 7.0
SWIGLU_ALPHA = 1.702
MIB = 2 ** 20


def _params(sem, vmem_mib):
    return pltpu.CompilerParams(dimension_semantics=sem, vmem_limit_bytes=vmem_mib * MIB)


def _sigmoid(x):
    return 1.0 / (1.0 + jnp.exp(-x))


def _silu(x):
    return x * _sigmoid(x)


def _gelu(x):
    return 0.5 * x * (1.0 + lax.erf(x * (0.5 ** 0.5)))


def _softplus(x):
    return jnp.maximum(x, 0.0) + jnp.log(1.0 + jnp.exp(-jnp.abs(x)))


def _dot(a, b):
    return jnp.dot(a, b, preferred_element_type=F32)


def _dot_nt(a, b):
    return lax.dot_general(a, b, (((1,), (1,)), ((), ())), preferred_element_type=F32)


def _ada_kernel(c_ref, w_ref, b_ref, o_ref):
    c = c_ref[...]
    o_ref[...] = _dot(_silu(c).astype(BF16), w_ref[...].astype(BF16)) + b_ref[...]


def _ada(c_all, w_ada, b_ada):
    depth, d, n = w_ada.shape
    nc = c_all.shape[0]
    tn = 1536
    return pl.pallas_call(
        _ada_kernel,
        grid=(depth, n // tn),
        in_specs=[pl.BlockSpec((nc, d), lambda l, j: (0, 0)),
                  pl.BlockSpec((None, d, tn), lambda l, j: (l, 0, j)),
                  pl.BlockSpec((None, 1, tn), lambda l, j: (l, 0, j))],
        out_specs=pl.BlockSpec((None, nc, tn), lambda l, j: (l, 0, j)),
        out_shape=jax.ShapeDtypeStruct((depth, nc, n), F32),
        compiler_params=_params(("arbitrary", "arbitrary"), 40),
        name="ada",
    )(c_all, w_ada, b_ada.reshape(depth, 1, n))


def _comb_kernel(*refs, combine, final):
    it = iter(refs)
    x_ref = next(it)
    if combine:
        y_ref, g_ref, gt_ref = next(it), next(it), next(it)
    n_ref = next(it)
    if not final:
        sc_ref, sh_ref = next(it), next(it)
    if combine and not final:
        xo_ref = next(it)
    o_ref = next(it)

    x = x_ref[...]
    if combine:
        g = g_ref[...]
        acc = g[:, :, 0:1] * y_ref[0]
        for k in range(1, TOP_K):
            acc = acc + g[:, :, k:k + 1] * y_ref[k]
        x = x + gt_ref[...] * acc
        if not final:
            xo_ref[...] = x
    ms = jnp.mean(x * x, axis=-1, keepdims=True)
    xn = x * lax.rsqrt(ms + EPS)
    if final:
        o_ref[...] = xn * n_ref[...]
    else:
        h = xn * n_ref[...] * (1.0 + sc_ref[...]) + sh_ref[...]
        o_ref[...] = h.reshape(o_ref.shape).astype(BF16)


def _comb(x3, norm_g, mod_n=None, k_sc=None, k_sh=None, y4=None, gates3=None, mod_gt=None, k_gt=None, *, tm=256):
    g8, _, d = x3.shape
    t = g8 * SUBLANES
    nb8 = tm // SUBLANES
    combine = y4 is not None
    final = k_sc is None
    x_spec = pl.BlockSpec((nb8, SUBLANES, d), lambda i: (i, 0, 0))

    def mod_spec(k):
        return pl.BlockSpec((None, nb8, 1, d), lambda i: (k, i, 0, 0))

    args, specs = [x3], [x_spec]
    if combine:
        args += [y4, gates3, mod_gt]
        specs += [pl.BlockSpec((TOP_K, nb8, SUBLANES, d), lambda i: (0, i, 0, 0)),
                  pl.BlockSpec((nb8, SUBLANES, LANES), lambda i: (i, 0, 0)),
                  mod_spec(k_gt)]
    args.append(norm_g.reshape(1, d))
    specs.append(pl.BlockSpec((1, d), lambda i: (0, 0)))
    if not final:
        args += [mod_n, mod_n]
        specs += [mod_spec(k_sc), mod_spec(k_sh)]
    out_shape, out_specs = [], []
    if combine and not final:
        out_shape.append(jax.ShapeDtypeStruct((g8, SUBLANES, d), F32))
        out_specs.append(x_spec)
    if final:
        out_shape.append(jax.ShapeDtypeStruct((g8, SUBLANES, d), F32))
        out_specs.append(x_spec)
    else:
        out_shape.append(jax.ShapeDtypeStruct((t, d), BF16))
        out_specs.append(pl.BlockSpec((tm, d), lambda i: (i, 0)))
    return pl.pallas_call(
        functools.partial(_comb_kernel, combine=combine, final=final),
        grid=(t // tm,),
        in_specs=specs, out_specs=out_specs, out_shape=out_shape,
        compiler_params=_params(("arbitrary",), 48),
        name="comb",
    )(*args)


def _gmlp_kernel(h_ref, w_ref, lng_ref, lnb_ref, wmix_ref, bias_ref, o_ref, v_ref, *, n_prompt_tiles):
    i = pl.program_id(0)
    tm, d = o_ref.shape
    uv = _dot(h_ref[...], w_ref[...])
    u = _gelu(uv[:, :d])
    v = _gelu(uv[:, d:])
    mu = jnp.mean(v, axis=-1, keepdims=True)
    vc = v - mu
    var = jnp.mean(vc * vc, axis=-1, keepdims=True)
    vn = vc * lax.rsqrt(var + EPS) * lng_ref[...] + lnb_ref[...]

    @pl.when(i >= n_prompt_tiles)
    def _():
        v_ref[...] = vn

    vb = vn.astype(BF16)
    dg = d // G_A
    for c in range(tm // CHUNK):
        rows = slice(c * CHUNK, (c + 1) * CHUNK)
        for g in range(G_A):
            cols = slice(g * dg, (g + 1) * dg)
            s = _dot(wmix_ref[g], vb[rows, cols]) + bias_ref[:, cols]
            o_ref[rows, cols] = (u[rows, cols] * s).astype(BF16)


def _gmlp(h, w_uv, ln_g, ln_b, wmix, bias, t_prompt, *, tm=512):
    t, d = h.shape
    n_p = t_prompt // tm
    t_s = t - t_prompt
    sel = lambda i: jnp.where(i >= n_p, 1, 0)
    return pl.pallas_call(
        functools.partial(_gmlp_kernel, n_prompt_tiles=n_p),
        grid=(t // tm,),
        in_specs=[pl.BlockSpec((tm, d), lambda i: (i, 0)),
                  pl.BlockSpec((d, 2 * d), lambda i: (0, 0)),
                  pl.BlockSpec((1, d), lambda i: (0, 0)),
                  pl.BlockSpec((1, d), lambda i: (0, 0)),
                  pl.BlockSpec((None, G_A, CHUNK, CHUNK), lambda i: (sel(i), 0, 0, 0)),
                  pl.BlockSpec((None, CHUNK, d), lambda i: (sel(i), 0, 0))],
        out_specs=[pl.BlockSpec((tm, d), lambda i: (i, 0)),
                   pl.BlockSpec((tm, d), lambda i: (jnp.maximum(i - n_p, 0), 0))],
        out_shape=[jax.ShapeDtypeStruct((t, d), BF16),
                   jax.ShapeDtypeStruct((t_s, d), F32)],
        compiler_params=_params(("arbitrary",), 48),
        name="gmlp",
    )(h, w_uv, ln_g.reshape(1, d), ln_b.reshape(1, d), wmix, bias)


def _proj_kernel(x_ref, w_ref, o_ref):
    o_ref[...] = _dot(x_ref[...], w_ref[...])


def _proj(x, w, *, tm=256):
    t, d = x.shape
    n = w.shape[1]
    return pl.pallas_call(
        _proj_kernel,
        grid=(t // tm,),
        in_specs=[pl.BlockSpec((tm, d), lambda i: (i, 0)),
                  pl.BlockSpec((d, n), lambda i: (0, 0))],
        out_specs=pl.BlockSpec((tm, n), lambda i: (i, 0)),
        out_shape=jax.ShapeDtypeStruct((t, n), F32),
        compiler_params=_params(("arbitrary",), 56),
        name="inproj",
    )(x, w)


D_INNER = G_B * R_B * P_B
GN = G_B * N_B
CONV_DIM = D_INNER + 2 * GN
GP = R_B * P_B
HIST = SUBLANES


def _ssd_stage_a(proj_ref, cw_ref, cb_ref, dtb_ref, alog_ref, cs_ref, xact_ref, acum_ref, acumT_ref,
                 dtpT_ref, *, nb, lc):
    r = nb * lc
    cs_ref[:, HIST:HIST + lc, :] = proj_ref[:, D_INNER:D_INNER + CONV_DIM].reshape(nb, lc, CONV_DIM)
    conv = cb_ref[...] + cs_ref[:, HIST - 3:HIST - 3 + lc, :] * cw_ref[0:1, :]
    for k in range(1, CONV_W):
        conv = conv + cs_ref[:, HIST - 3 + k:HIST - 3 + k + lc, :] * cw_ref[k:k + 1, :]
    xact_ref[...] = _silu(conv).reshape(r, CONV_DIM)
    tail = cs_ref[:, HIST + lc - 3:HIST + lc, :]

    dt = proj_ref[:, D_INNER + CONV_DIM:D_INNER + CONV_DIM + LANES]
    dtp = _softplus(dt + dtb_ref[...])
    acum = dtp * (-jnp.exp(alog_ref[...]))
    local = lax.broadcasted_iota(jnp.int32, (r, LANES), 0) % lc
    sh = 1
    while sh < lc:
        acum = acum + jnp.where(local >= sh, pltpu.roll(acum, sh, axis=0), 0.0)
        sh *= 2
    acum_ref[...] = acum
    acumT_ref[...] = acum.T
    dtpT_ref[...] = dtp.T
    return tail


def _ssd_heads(g, mask, cb, yoff, xact_ref, acum_ref, acumT_ref, dtpT_ref, dsk_ref):
    r = xact_ref.shape[0]
    ys = []
    for rr in range(R_B):
        hh = g * R_B + rr
        colb = jnp.broadcast_to(acum_ref[:, hh:hh + 1], (r, r))
        seg = colb - acumT_ref[hh:hh + 1, :]
        m = cb * jnp.exp(jnp.where(mask, seg, -jnp.inf)) * dtpT_ref[hh:hh + 1, :]
        c0 = g * GP + rr * P_B
        xs_r = xact_ref[:, c0:c0 + P_B]
        y = _dot(m.astype(BF16), xs_r.astype(BF16))
        y = y + jnp.exp(colb[:, :P_B]) * yoff[:, rr * P_B:(rr + 1) * P_B]
        y = y + dsk_ref[:, c0:c0 + P_B] * xs_r
        ys.append(y)
    return jnp.concatenate(ys, axis=1)


def _ssd_gate_norm(y, z, ng):
    yz = y * _silu(z)
    ms = jnp.mean(yz * yz, axis=-1, keepdims=True)
    return (yz * lax.rsqrt(ms + EPS) * ng).astype(BF16)


def _head_rows(vals):
    return jnp.concatenate([jnp.broadcast_to(v, (P_B, N_B)) for v in vals], axis=0)


def _ssd_prompt_kernel(proj_ref, cw_ref, cb_ref, dtb_ref, alog_ref, dsk_ref, ng_ref,
                       ob_ref, cout_ref, hout_ref,
                       cs_ref, hs_ref, xact_ref, acum_ref, acumT_ref, dtpT_ref):
    c = pl.program_id(1)
    lc = CHUNK

    @pl.when(c == 0)
    def _():
        cs_ref[:, 0:HIST, :] = jnp.zeros((1, HIST, CONV_DIM), F32)
        hs_ref[...] = jnp.zeros(hs_ref.shape, F32)

    tail = _ssd_stage_a(proj_ref, cw_ref, cb_ref, dtb_ref, alog_ref, cs_ref, xact_ref, acum_ref,
                        acumT_ref, dtpT_ref, nb=1, lc=lc)
    cs_ref[:, HIST - 3:HIST, :] = tail

    @pl.when(c == pl.num_programs(1) - 1)
    def _():
        cout_ref[...] = tail[0]

    li = lax.broadcasted_iota(jnp.int32, (lc, lc), 0)
    si = lax.broadcasted_iota(jnp.int32, (lc, lc), 1)
    mask = li >= si
    acum_t = acumT_ref[...]
    wt = dtpT_ref[...] * jnp.exp(acum_t[:, lc - 1:lc] - acum_t)
    for g in range(G_B):
        bg = xact_ref[:, D_INNER + g * N_B:D_INNER + (g + 1) * N_B].astype(BF16)
        cg = xact_ref[:, D_INNER + GN + g * N_B:D_INNER + GN + (g + 1) * N_B].astype(BF16)
        cb = _dot_nt(cg, bg)
        hg = hs_ref[g]
        yoff = _dot_nt(cg, hg.astype(BF16))
        y = _ssd_heads(g, mask, cb, yoff, xact_ref, acum_ref, acumT_ref, dtpT_ref, dsk_ref)
        cols = slice(g * GP, (g + 1) * GP)
        ob_ref[:, cols] = _ssd_gate_norm(y, proj_ref[:, cols], ng_ref[:, cols])
        xs_t = xact_ref[:, cols].T
        xw = jnp.concatenate(
            [xs_t[rr * P_B:(rr + 1) * P_B, :] * wt[g * R_B + rr:g * R_B + rr + 1, :] for rr in range(R_B)],
            axis=0)
        inc = _dot(xw.astype(BF16), bg)
        dec = _head_rows([jnp.exp(acum_t[g * R_B + rr:g * R_B + rr + 1, lc - 1:lc]) for rr in range(R_B)])
        hs_ref[g] = dec * hg + inc

    @pl.when(c == pl.num_programs(1) - 1)
    def _():
        hout_ref[...] = hs_ref[...]


def _ssd_prompt(proj, cw, cb, dtb, alog, dsk, ng, batch, seq):
    n = proj.shape[1]
    nc = seq // CHUNK
    vec = lambda w: pl.BlockSpec((1, w), lambda b, c: (0, 0))
    return pl.pallas_call(
        _ssd_prompt_kernel,
        grid=(batch, nc),
        in_specs=[pl.BlockSpec((CHUNK, n), lambda b, c: (b * nc + c, 0)),
                  pl.BlockSpec((CONV_W, CONV_DIM), lambda b, c: (0, 0)),
                  vec(CONV_DIM), vec(LANES), vec(LANES), vec(D_INNER), vec(D_INNER)],
        out_specs=[pl.BlockSpec((CHUNK, D_INNER), lambda b, c: (b * nc + c, 0)),
                   pl.BlockSpec((None, CONV_W - 1, CONV_DIM), lambda b, c: (b, 0, 0)),
                   pl.BlockSpec((None, G_B, GP, N_B), lambda b, c: (b, 0, 0, 0))],
        out_shape=[jax.ShapeDtypeStruct((batch * seq, D_INNER), BF16),
                   jax.ShapeDtypeStruct((batch, CONV_W - 1, CONV_DIM), F32),
                   jax.ShapeDtypeStruct((batch, G_B, GP, N_B), F32)],
        scratch_shapes=[pltpu.VMEM((1, HIST + CHUNK, CONV_DIM), F32),
                        pltpu.VMEM((G_B, GP, N_B), F32),
                        pltpu.VMEM((CHUNK, CONV_DIM), F32),
                        pltpu.VMEM((CHUNK, LANES), F32),
                        pltpu.VMEM((LANES, CHUNK), F32),
                        pltpu.VMEM((LANES, CHUNK), F32)],
        compiler_params=_params(("arbitrary", "arbitrary"), 48),
        name="ssd_prompt",
    )(proj, cw, cb, dtb, alog, dsk, ng)


def _ssd_sample_kernel(proj_ref, cw_ref, cb_ref, dtb_ref, alog_ref, dsk_ref, ng_ref, cin_ref, hin_ref,
                       ob_ref, cout_ref, hout_ref,
                       cs_ref, xact_ref, acum_ref, acumT_ref, dtpT_ref, yoff_ref, *, nb, lc):
    g = pl.program_id(1)
    r = nb * lc

    @pl.when(g == 0)
    def _():
        cs_ref[:, HIST - 3:HIST, :] = cin_ref[...]
        tail = _ssd_stage_a(proj_ref, cw_ref, cb_ref, dtb_ref, alog_ref, cs_ref, xact_ref, acum_ref,
                            acumT_ref, dtpT_ref, nb=nb, lc=lc)
        cout_ref[...] = tail

    li = lax.broadcasted_iota(jnp.int32, (r, r), 0)
    si = lax.broadcasted_iota(jnp.int32, (r, r), 1)
    mask = (li >= si) & ((li // lc) == (si // lc))
    lane_seq = lax.broadcasted_iota(jnp.int32, (GP, r), 1) // lc
    acum_t = acumT_ref[...]
    dtp_t = dtpT_ref[...]

    def branch(gg):
        bg = xact_ref[:, D_INNER + gg * N_B:D_INNER + (gg + 1) * N_B].astype(BF16)
        cg = xact_ref[:, D_INNER + GN + gg * N_B:D_INNER + GN + (gg + 1) * N_B].astype(BF16)
        cb = _dot_nt(cg, bg)
        cols = slice(gg * GP, (gg + 1) * GP)
        xs_t = xact_ref[:, cols].T
        win = 2 * SUBLANES
        for b in range(nb):
            h0 = hin_ref[b, 0]
            w0 = (b * lc // win) * win
            yo = _dot_nt(cg[w0:w0 + win], h0.astype(BF16))
            yoff_ref[b * lc:(b + 1) * lc, :] = yo[b * lc - w0:b * lc - w0 + lc]
            last = b * lc + lc - 1
            xw = jnp.concatenate(
                [xs_t[rr * P_B:(rr + 1) * P_B, :]
                 * (dtp_t[gg * R_B + rr:gg * R_B + rr + 1, :]
                    * jnp.exp(acum_t[gg * R_B + rr:gg * R_B + rr + 1, last:last + 1]
                              - acum_t[gg * R_B + rr:gg * R_B + rr + 1, :]))
                 for rr in range(R_B)], axis=0)
            xw = jnp.where(lane_seq == b, xw, 0.0)
            inc = _dot(xw.astype(BF16), bg)
            dec = _head_rows([jnp.exp(acum_t[gg * R_B + rr:gg * R_B + rr + 1, last:last + 1])
                              for rr in range(R_B)])
            hout_ref[b, 0] = dec * h0 + inc
        y = _ssd_heads(gg, mask, cb, yoff_ref[...], xact_ref, acum_ref, acumT_ref, dtpT_ref, dsk_ref)
        ob_ref[...] = _ssd_gate_norm(y, proj_ref[:, cols], ng_ref[:, cols])

    for gg in range(G_B):
        pl.when(g == gg)(functools.partial(branch, gg))


def _ssd_sample(proj, row0, lc, cw, cb, dtb, alog, dsk, ng, conv_state, ssm_state, *, nb=16):
    n = proj.shape[1]
    batch = conv_state.shape[0]
    r = nb * lc
    blk0 = row0 // r
    vec = lambda w: pl.BlockSpec((1, w), lambda i, g: (0, 0))
    return pl.pallas_call(
        functools.partial(_ssd_sample_kernel, nb=nb, lc=lc),
        grid=(batch // nb, G_B),
        in_specs=[pl.BlockSpec((r, n), lambda i, g: (blk0 + i, 0)),
                  pl.BlockSpec((CONV_W, CONV_DIM), lambda i, g: (0, 0)),
                  vec(CONV_DIM), vec(LANES), vec(LANES), vec(D_INNER), vec(D_INNER),
                  pl.BlockSpec((nb, CONV_W - 1, CONV_DIM), lambda i, g: (i, 0, 0)),
                  pl.BlockSpec((nb, 1, GP, N_B), lambda i, g: (i, g, 0, 0))],
        out_specs=[pl.BlockSpec((r, GP), lambda i, g: (i, g)),
                   pl.BlockSpec((nb, CONV_W - 1, CONV_DIM), lambda i, g: (i, 0, 0)),
                   pl.BlockSpec((nb, 1, GP, N_B), lambda i, g: (i, g, 0, 0))],
        out_shape=[jax.ShapeDtypeStruct((batch * lc, D_INNER), BF16),
                   jax.ShapeDtypeStruct((batch, CONV_W - 1, CONV_DIM), F32),
                   jax.ShapeDtypeStruct((batch, G_B, GP, N_B), F32)],
        scratch_shapes=[pltpu.VMEM((nb, HIST + lc, CONV_DIM), F32),
                        pltpu.VMEM((r, CONV_DIM), F32),
                        pltpu.VMEM((r, LANES), F32),
                        pltpu.VMEM((LANES, r), F32),
                        pltpu.VMEM((LANES, r), F32),
                        pltpu.VMEM((r, GP), F32)],
        compiler_params=_params(("arbitrary", "arbitrary"), 48),
        name="ssd_sample",
    )(proj, cw, cb, dtb, alog, dsk, ng, conv_state, ssm_state)


def _mid_kernel(h_ref, oa_ref, ob_ref, x_ref, gt_ref, sc_ref, sh_ref, n2_ref,
                wg_ref, wpa_ref, wpb_ref, wo_ref, rw_ref, rb_ref,
                x1_ref, h2_ref, gates_ref, idx_ref):
    tm, d = h2_ref.shape
    gab = _dot(h_ref[...], wg_ref[...])
    m = (_sigmoid(gab[:, :d]) * _dot(oa_ref[...], wpa_ref[...])
         + _sigmoid(gab[:, d:]) * _dot(ob_ref[...], wpb_ref[...]))
    mo = _dot(m.astype(BF16), wo_ref[...])
    x1 = x_ref[...] + gt_ref[...] * mo.reshape(x_ref.shape)
    x1_ref[...] = x1
    ms = jnp.mean(x1 * x1, axis=-1, keepdims=True)
    h2 = (x1 * lax.rsqrt(ms + EPS) * n2_ref[...] * (1.0 + sc_ref[...]) + sh_ref[...]).reshape(tm, d)
    h2_ref[...] = h2.astype(BF16)

    logits = jnp.dot(h2, rw_ref[...], preferred_element_type=F32,
                     precision=lax.Precision.HIGHEST) + rb_ref[...]
    lane = lax.broadcasted_iota(jnp.int32, logits.shape, 1).astype(F32)
    vals, idxs = [], []
    for _ in range(TOP_K):
        mx = jnp.max(logits, axis=-1, keepdims=True)
        ix = jnp.min(jnp.where(logits == mx, lane, float(LANES)), axis=-1, keepdims=True)
        vals.append(mx)
        idxs.append(ix)
        logits = jnp.where(lane == ix, -jnp.inf, logits)
    es = [jnp.exp(v - vals[0]) for v in vals]
    tot = es[0] + es[1] + es[2] + es[3]
    gates = jnp.zeros(logits.shape, F32)
    idx = jnp.zeros(logits.shape, F32)
    for k in range(TOP_K):
        gates = jnp.where(lane == float(k), es[k] / tot, gates)
        idx = jnp.where(lane == float(k), idxs[k], idx)
    gates_ref[...] = gates
    idx_ref[...] = idx.astype(jnp.int32)


def _mid(h, oa, ob, x3, mod, n2, wg, wpa, wpb, wo, rw, rb, *, tm=256):
    t, d = h.shape
    nb8 = tm // SUBLANES
    row = lambda w: pl.BlockSpec((tm, w), lambda i: (i, 0))
    full = lambda a: pl.BlockSpec(a.shape, lambda i: (0,) * a.ndim)
    x_spec = pl.BlockSpec((nb8, SUBLANES, d), lambda i: (i, 0, 0))
    mod_spec = lambda k: pl.BlockSpec((None, nb8, 1, d), lambda i: (k, i, 0, 0))
    n2 = n2.reshape(1, d)
    return pl.pallas_call(
        _mid_kernel,
        grid=(t // tm,),
        in_specs=[row(d), row(d), row(2 * d), x_spec, mod_spec(2), mod_spec(4), mod_spec(3), full(n2),
                  full(wg), full(wpa), full(wpb), full(wo), full(rw), full(rb)],
        out_specs=[x_spec, row(d), row(LANES), row(LANES)],
        out_shape=[jax.ShapeDtypeStruct(x3.shape, F32),
                   jax.ShapeDtypeStruct((t, d), BF16),
                   jax.ShapeDtypeStruct((t, LANES), F32),
                   jax.ShapeDtypeStruct((t, LANES), jnp.int32)],
        compiler_params=_params(("arbitrary",), 56),
        name="mid",
    )(h, oa, ob, x3, mod, mod, mod, n2, wg, wpa, wpb, wo, rw, rb)


def _moe_kernel(blk_e_ref, blk_first_ref, n_used_ref, x_ref, wgu_ref, bgu_ref, wdn_ref, bdn_ref, o_ref,
                wgu_bf, wdn_bf):
    i = pl.program_id(0)
    d_ff = wdn_bf.shape[0]

    @pl.when(blk_first_ref[i] == 1)
    def _():
        wgu_bf[...] = wgu_ref[...].astype(BF16)
        wdn_bf[...] = wdn_ref[...].astype(BF16)

    @pl.when(i < n_used_ref[0])
    def _():
        gu = _dot(x_ref[...], wgu_bf[...]) + bgu_ref[...]
        g = jnp.minimum(gu[:, :d_ff], SWIGLU_LIMIT)
        u = jnp.clip(gu[:, d_ff:], -SWIGLU_LIMIT, SWIGLU_LIMIT)
        act = g * _sigmoid(SWIGLU_ALPHA * g) * (u + 1.0)
        o_ref[...] = _dot(act.astype(BF16), wdn_bf[...]) + bdn_ref[...]

    @pl.when(i >= n_used_ref[0])
    def _():
        o_ref[...] = jnp.zeros(o_ref.shape, F32)


def _moe(xb, blk_e, blk_first, n_used, w_gu, b_gu, w_dn, b_dn, *, tm):
    rows, d = xb.shape
    n_e, _, d_gu = w_gu.shape
    d_ff = w_dn.shape[1]
    grid_spec = pltpu.PrefetchScalarGridSpec(
        num_scalar_prefetch=3,
        grid=(rows // tm,),
        in_specs=[pl.BlockSpec((tm, d), lambda i, be, bf, nu: (i, 0)),
                  pl.BlockSpec((None, d, d_gu), lambda i, be, bf, nu: (be[i], 0, 0)),
                  pl.BlockSpec((None, 1, d_gu), lambda i, be, bf, nu: (be[i], 0, 0)),
                  pl.BlockSpec((None, d_ff, d), lambda i, be, bf, nu: (be[i], 0, 0)),
                  pl.BlockSpec((None, 1, d), lambda i, be, bf, nu: (be[i], 0, 0))],
        out_specs=pl.BlockSpec((tm, d), lambda i, be, bf, nu: (i, 0)),
        scratch_shapes=[pltpu.VMEM((d, d_gu), BF16), pltpu.VMEM((d_ff, d), BF16)])
    return pl.pallas_call(
        _moe_kernel,
        grid_spec=grid_spec,
        out_shape=jax.ShapeDtypeStruct((rows, d), F32),
        compiler_params=_params(("arbitrary",), 56),
        name="moe",
    )(blk_e, blk_first, n_used, xb, w_gu, b_gu.reshape(n_e, 1, d_gu), w_dn, b_dn.reshape(n_e, 1, d))


def _route(idx, n_e, tm):
    t = idx.shape[0]
    tk = t * TOP_K
    flat_e = idx.reshape(-1)
    order = jnp.argsort(flat_e, stable=True).astype(jnp.int32)
    sorted_e = flat_e[order]
    counts = jnp.zeros((n_e,), jnp.int32).at[flat_e].add(1)
    padded = (counts + tm - 1) // tm * tm
    pad_end = jnp.cumsum(padded)
    pad_start = pad_end - padded
    grp_start = jnp.cumsum(counts) - counts
    rank = jnp.arange(tk, dtype=jnp.int32) - grp_start[sorted_e]
    dest = pad_start[sorted_e] + rank
    n_blocks = -(-tk // tm) + n_e
    rows = n_blocks * tm
    row_tok = jnp.zeros((rows,), jnp.int32).at[dest].set(order // TOP_K)
    blk_start = jnp.arange(n_blocks, dtype=jnp.int32) * tm
    blk_e = jnp.minimum(jnp.searchsorted(pad_end, blk_start, side="right"), n_e - 1).astype(jnp.int32)
    blk_first = jnp.concatenate([jnp.ones((1,), jnp.int32),
                                 (blk_e[1:] != blk_e[:-1]).astype(jnp.int32)])
    n_used = (pad_end[-1:] // tm).astype(jnp.int32)
    pair_row = jnp.zeros((tk,), jnp.int32).at[order].set(dest)
    return row_tok, blk_e, blk_first, n_used, pair_row.reshape(t, TOP_K)


def kernel(x_prompt, x_sample, c_prompt, c_sample, state_ssm, state_conv, norm1_g, norm2_g, final_g,
           w_ada, b_ada, w_in, a_ln_g, a_ln_b, a_ws, a_bs, b_conv_w, b_conv_b, b_dt_bias, b_a_log, b_d,
           b_norm_g, w_proj_a, w_proj_b, w_out, router_w, router_b, w_gu, b_gu, w_dn, b_dn):
    bp, seq, d = x_prompt.shape
    bs, lc_s, _ = x_sample.shape
    depth = w_ada.shape[0]
    n_e = router_w.shape[-1]
    heads = b_d.shape[-1]
    t_p, t_s = bp * seq, bs * lc_s
    t = t_p + t_s
    g8 = t // SUBLANES
    moe_tm = 256

    d_a = d
    c0 = 2 * d_a
    c1 = c0 + D_INNER + CONV_DIM + heads
    w_uv = w_in[:, :, :c0].astype(BF16)
    w_ssd = jnp.pad(w_in[:, :, c0:c1], ((0, 0), (0, 0), (0, LANES - heads))).astype(BF16)
    w_g = w_in[:, :, c1:].astype(BF16)
    w_pa, w_pb, w_o = w_proj_a.astype(BF16), w_proj_b.astype(BF16), w_out.astype(BF16)
    rw = jnp.pad(router_w, ((0, 0), (0, 0), (0, LANES - n_e)))
    rb = jnp.pad(router_b, ((0, 0), (0, LANES - n_e)), constant_values=-jnp.inf).reshape(depth, 1, LANES)
    causal = jnp.tril(jnp.ones((CHUNK, CHUNK), bool))
    ws_p = jnp.where(causal, a_ws, 0.0)
    blk = jnp.where(jnp.tril(jnp.ones((lc_s, lc_s), bool)), a_ws[:, :, :lc_s, :lc_s], 0.0)
    eye = jnp.eye(CHUNK // lc_s, dtype=F32)
    ws_s = jnp.einsum("ab,lgts->lgatbs", eye, blk).reshape(depth, G_A, CHUNK, CHUNK)
    wmix = jnp.stack([ws_p, ws_s], axis=1).astype(BF16)
    dg = d_a // G_A
    bias_p = jnp.repeat(jnp.swapaxes(a_bs, 1, 2), dg, axis=2)
    bias_s = jnp.tile(bias_p[:, :lc_s], (1, CHUNK // lc_s, 1))
    bias = jnp.stack([bias_p, bias_s], axis=1)
    dtb = jnp.pad(b_dt_bias, ((0, 0), (0, LANES - heads))).reshape(depth, 1, LANES)
    alog = jnp.pad(b_a_log, ((0, 0), (0, LANES - heads))).reshape(depth, 1, LANES)
    dsk = jnp.repeat(b_d, P_B, axis=1).reshape(depth, 1, D_INNER)
    cb = b_conv_b.reshape(depth, 1, CONV_DIM)
    ng = b_norm_g.reshape(depth, 1, D_INNER)

    mod_all = _ada(jnp.concatenate([c_prompt, c_sample], axis=0), w_ada, b_ada)
    mod_g = jnp.concatenate([jnp.repeat(mod_all[:, :bp], seq // SUBLANES, axis=1),
                             jnp.repeat(mod_all[:, bp:], lc_s // SUBLANES, axis=1)], axis=1)
    mod_g = mod_g.reshape(depth, g8, 6, d).transpose(0, 2, 1, 3).reshape(depth, 6, g8, 1, d)

    x3 = jnp.concatenate([x_prompt.reshape(t_p, d), x_sample.reshape(t_s, d)], axis=0).reshape(g8, SUBLANES, d)
    ssm_s_in = state_ssm.reshape(depth, bs, G_B, GP, N_B)

    h = _comb(x3, norm1_g[0], mod_g[0], k_sc=1, k_sh=0)[0]
    ssm_p, conv_p, ssm_s, conv_s, v_s = [], [], [], [], []
    y_final = None
    for l in range(depth):
        oa, v_rows = _gmlp(h, w_uv[l], a_ln_g[l], a_ln_b[l], wmix[l], bias[l], t_p)
        proj = _proj(h, w_ssd[l])
        ob_p, cp, sp = _ssd_prompt(proj, b_conv_w[l], cb[l], dtb[l], alog[l], dsk[l], ng[l], bp, seq)
        ob_s, cs, ss = _ssd_sample(proj, t_p, lc_s, b_conv_w[l], cb[l], dtb[l], alog[l], dsk[l], ng[l],
                                   state_conv[l], ssm_s_in[l])
        ob = jnp.concatenate([ob_p, ob_s], axis=0)
        x1, h2, gates, idx = _mid(h, oa, ob, x3, mod_g[l], norm2_g[l], w_g[l], w_pa[l], w_pb[l], w_o[l],
                                  rw[l], rb[l])
        row_tok, blk_e, blk_first, n_used, pair_row = _route(idx[:, :TOP_K], n_e, moe_tm)
        yb = _moe(h2[row_tok], blk_e, blk_first, n_used, w_gu[l], b_gu[l], w_dn[l], b_dn[l], tm=moe_tm)
        y4 = yb[pair_row.T].reshape(TOP_K, g8, SUBLANES, d)
        gates3 = gates.reshape(g8, SUBLANES, LANES)
        if l + 1 < depth:
            x3, h = _comb(x1, norm1_g[l + 1], mod_g[l + 1], k_sc=1, k_sh=0, y4=y4, gates3=gates3,
                          mod_gt=mod_g[l], k_gt=5)
        else:
            y_final = _comb(x1, final_g, y4=y4, gates3=gates3, mod_gt=mod_g[l], k_gt=5)[0]
        ssm_p.append(sp.reshape(bp, heads, P_B, N_B))
        conv_p.append(cp)
        ssm_s.append(ss.reshape(bs, heads, P_B, N_B))
        conv_s.append(cs)
        v_s.append(v_rows.reshape(bs, lc_s, d))
    y_final = y_final.reshape(t, d)
    return (y_final[:t_p].reshape(bp, seq, d), y_final[t_p:].reshape(bs, lc_s, d),
            jnp.stack(ssm_p), jnp.stack(conv_p), jnp.stack(ssm_s), jnp.stack(conv_s), jnp.stack(v_s))
```

```python
import functools

import jax
import jax.numpy as jnp
from jax import lax
from jax.experimental import pallas as pl
from jax.experimental.pallas import tpu as pltpu

F32 = jnp.float32
BF16 = jnp.bfloat16

EPS = 1e-6
SUBLANES = 8
LANES = 128
CHUNK = 128
G_A = 8
G_B = 8
R_B = 4
P_B = 64
N_B = 128
CONV_W = 4
TOP_K = 4
SWIGLU_LIMIT = 7.0
SWIGLU_ALPHA = 1.702
MIB = 2 ** 20


def _params(sem, vmem_mib):
    return pltpu.CompilerParams(dimension_semantics=sem, vmem_limit_bytes=vmem_mib * MIB)


def _sigmoid(x):
    return 1.0 / (1.0 + jnp.exp(-x))


def _silu(x):
    return x * _sigmoid(x)


def _gelu(x):
    return 0.5 * x * (1.0 + lax.erf(x * (0.5 ** 0.5)))


def _softplus(x):
    return jnp.maximum(x, 0.0) + jnp.log(1.0 + jnp.exp(-jnp.abs(x)))


def _dot(a, b):
    return jnp.dot(a, b, preferred_element_type=F32)


def _dot_nt(a, b):
    return lax.dot_general(a, b, (((1,), (1,)), ((), ())), preferred_element_type=F32)


def _ada_kernel(c_ref, w_ref, b_ref, o_ref):
    c = c_ref[...]
    o_ref[...] = _dot(_silu(c).astype(BF16), w_ref[...].astype(BF16)) + b_ref[...]


def _ada(c_all, w_ada, b_ada):
    depth, d, n = w_ada.shape
    nc = c_all.shape[0]
    tn = 1536
    return pl.pallas_call(
        _ada_kernel,
        grid=(depth, n // tn),
        in_specs=[pl.BlockSpec((nc, d), lambda l, j: (0, 0)),
                  pl.BlockSpec((None, d, tn), lambda l, j: (l, 0, j)),
                  pl.BlockSpec((None, 1, tn), lambda l, j: (l, 0, j))],
        out_specs=pl.BlockSpec((None, nc, tn), lambda l, j: (l, 0, j)),
        out_shape=jax.ShapeDtypeStruct((depth, nc, n), F32),
        compiler_params=_params(("arbitrary", "arbitrary"), 40),
        name="ada",
    )(c_all, w_ada, b_ada.reshape(depth, 1, n))


def _row_copy(src_ref, src_row, dst_ref, dst_row, sem):
    return pltpu.make_async_copy(src_ref.at[pl.ds(src_row, 1)], dst_ref.at[pl.ds(dst_row, 1)], sem)


def _comb_kernel(*refs, combine, final):
    it = iter(refs)
    if combine:
        pos_ref, posn_ref = next(it), next(it)
    x_ref = next(it)
    if combine:
        yb_ref, g_ref, gt_ref = next(it), next(it), next(it)
    n_ref = next(it)
    if not final:
        sc_ref, sh_ref = next(it), next(it)
    if combine and not final:
        xo_ref = next(it)
    o_ref = next(it)
    if combine:
        ybuf, sem = next(it), next(it)

    x = x_ref[...]
    if combine:
        i = pl.program_id(0)
        n = pl.num_programs(0)
        tm = ybuf.shape[2]
        slot = i % 2

        def gather(p_ref, s):
            def body(t, c):
                for k in range(TOP_K):
                    _row_copy(yb_ref, p_ref[k, t], ybuf.at[s, k], t, sem.at[s]).start()
                return c
            lax.fori_loop(0, tm, body, 0)

        @pl.when(i == 0)
        def _():
            gather(pos_ref, 0)

        @pl.when(i + 1 < n)
        def _():
            gather(posn_ref, 1 - slot)

        def drain(t, c):
            for k in range(TOP_K):
                _row_copy(yb_ref, 0, ybuf.at[slot, k], t, sem.at[slot]).wait()
            return c
        lax.fori_loop(0, tm, drain, 0)

        g = g_ref[...]
        acc = g[:, :, 0:1] * ybuf[slot, 0].reshape(x.shape)
        for k in range(1, TOP_K):
            acc = acc + g[:, :, k:k + 1] * ybuf[slot, k].reshape(x.shape)
        x = x + gt_ref[...] * acc
        if not final:
            xo_ref[...] = x
    ms = jnp.mean(x * x, axis=-1, keepdims=True)
    xn = x * lax.rsqrt(ms + EPS)
    if final:
        o_ref[...] = xn * n_ref[...]
    else:
        h = xn * n_ref[...] * (1.0 + sc_ref[...]) + sh_ref[...]
        o_ref[...] = h.reshape(o_ref.shape).astype(BF16)


def _comb(x3, norm_g, mod_n=None, k_sc=None, k_sh=None, yb=None, pos=None, gates3=None, mod_gt=None, k_gt=None,
          *, tm=256):
    g8, _, d = x3.shape
    t = g8 * SUBLANES
    nb8 = tm // SUBLANES
    n_tiles = t // tm
    combine = yb is not None
    final = k_sc is None
    x_spec = pl.BlockSpec((nb8, SUBLANES, d), lambda i: (i, 0, 0))

    def mod_spec(k):
        return pl.BlockSpec((None, nb8, 1, d), lambda i: (k, i, 0, 0))

    args, specs, scratch = [], [], []
    if combine:
        args += [pos, pos]
        specs += [pl.BlockSpec((None, TOP_K, tm), lambda i: (i, 0, 0), memory_space=pltpu.SMEM),
                  pl.BlockSpec((None, TOP_K, tm), lambda i: (jnp.minimum(i + 1, n_tiles - 1), 0, 0),
                               memory_space=pltpu.SMEM)]
        scratch = [pltpu.VMEM((2, TOP_K, tm, d), F32), pltpu.SemaphoreType.DMA((2,))]
    args.append(x3)
    specs.append(x_spec)
    if combine:
        args += [yb, gates3, mod_gt]
        specs += [pl.BlockSpec(memory_space=pl.ANY),
                  pl.BlockSpec((nb8, SUBLANES, LANES), lambda i: (i, 0, 0)),
                  mod_spec(k_gt)]
    args.append(norm_g.reshape(1, d))
    specs.append(pl.BlockSpec((1, d), lambda i: (0, 0)))
    if not final:
        args += [mod_n, mod_n]
        specs += [mod_spec(k_sc), mod_spec(k_sh)]
    out_shape, out_specs = [], []
    if combine and not final:
        out_shape.append(jax.ShapeDtypeStruct((g8, SUBLANES, d), F32))
        out_specs.append(x_spec)
    if final:
        out_shape.append(jax.ShapeDtypeStruct((g8, SUBLANES, d), F32))
        out_specs.append(x_spec)
    else:
        out_shape.append(jax.ShapeDtypeStruct((t, d), BF16))
        out_specs.append(pl.BlockSpec((tm, d), lambda i: (i, 0)))
    return pl.pallas_call(
        functools.partial(_comb_kernel, combine=combine, final=final),
        grid=(n_tiles,),
        in_specs=specs, out_specs=out_specs, out_shape=out_shape,
        scratch_shapes=scratch,
        compiler_params=_params(("arbitrary",), 48),
        name="comb",
    )(*args)


def _gmlp_kernel(h_ref, w_ref, lng_ref, lnb_ref, wmix_ref, bias_ref, o_ref, v_ref, *, n_prompt_tiles):
    i = pl.program_id(0)
    tm, d = o_ref.shape
    uv = _dot(h_ref[...], w_ref[...])
    u = _gelu(uv[:, :d])
    v = _gelu(uv[:, d:])
    mu = jnp.mean(v, axis=-1, keepdims=True)
    vc = v - mu
    var = jnp.mean(vc * vc, axis=-1, keepdims=True)
    vn = vc * lax.rsqrt(var + EPS) * lng_ref[...] + lnb_ref[...]

    @pl.when(i >= n_prompt_tiles)
    def _():
        v_ref[...] = vn

    vb = vn.astype(BF16)
    dg = d // G_A
    for c in range(tm // CHUNK):
        rows = slice(c * CHUNK, (c + 1) * CHUNK)
        for g in range(G_A):
            cols = slice(g * dg, (g + 1) * dg)
            s = _dot(wmix_ref[g], vb[rows, cols]) + bias_ref[:, cols]
            o_ref[rows, cols] = (u[rows, cols] * s).astype(BF16)


def _gmlp(h, w_uv, ln_g, ln_b, wmix, bias, t_prompt, *, tm=512):
    t, d = h.shape
    n_p = t_prompt // tm
    t_s = t - t_prompt
    sel = lambda i: jnp.where(i >= n_p, 1, 0)
    return pl.pallas_call(
        functools.partial(_gmlp_kernel, n_prompt_tiles=n_p),
        grid=(t // tm,),
        in_specs=[pl.BlockSpec((tm, d), lambda i: (i, 0)),
                  pl.BlockSpec((d, 2 * d), lambda i: (0, 0)),
                  pl.BlockSpec((1, d), lambda i: (0, 0)),
                  pl.BlockSpec((1, d), lambda i: (0, 0)),
                  pl.BlockSpec((None, G_A, CHUNK, CHUNK), lambda i: (sel(i), 0, 0, 0)),
                  pl.BlockSpec((None, CHUNK, d), lambda i: (sel(i), 0, 0))],
        out_specs=[pl.BlockSpec((tm, d), lambda i: (i, 0)),
                   pl.BlockSpec((tm, d), lambda i: (jnp.maximum(i - n_p, 0), 0))],
        out_shape=[jax.ShapeDtypeStruct((t, d), BF16),
                   jax.ShapeDtypeStruct((t_s, d), F32)],
        compiler_params=_params(("arbitrary",), 48),
        name="gmlp",
    )(h, w_uv, ln_g.reshape(1, d), ln_b.reshape(1, d), wmix, bias)


def _proj_kernel(x_ref, w_ref, o_ref):
    o_ref[...] = _dot(x_ref[...], w_ref[...])


def _proj(x, w, *, tm=256):
    t, d = x.shape
    n = w.shape[1]
    return pl.pallas_call(
        _proj_kernel,
        grid=(t // tm,),
        in_specs=[pl.BlockSpec((tm, d), lambda i: (i, 0)),
                  pl.BlockSpec((d, n), lambda i: (0, 0))],
        out_specs=pl.BlockSpec((tm, n), lambda i: (i, 0)),
        out_shape=jax.ShapeDtypeStruct((t, n), F32),
        compiler_params=_params(("arbitrary",), 56),
        name="inproj",
    )(x, w)


D_INNER = G_B * R_B * P_B
GN = G_B * N_B
CONV_DIM = D_INNER + 2 * GN
GP = R_B * P_B
HIST = SUBLANES


def _ssd_stage_a(proj_ref, cw_ref, cb_ref, dtb_ref, alog_ref, cs_ref, xact_ref, acum_ref, acumT_ref,
                 dtpT_ref, *, nb, lc):
    r = nb * lc
    cs_ref[:, HIST:HIST + lc, :] = proj_ref[:, D_INNER:D_INNER + CONV_DIM].reshape(nb, lc, CONV_DIM)
    conv = cb_ref[...] + cs_ref[:, HIST - 3:HIST - 3 + lc, :] * cw_ref[0:1, :]
    for k in range(1, CONV_W):
        conv = conv + cs_ref[:, HIST - 3 + k:HIST - 3 + k + lc, :] * cw_ref[k:k + 1, :]
    xact_ref[...] = _silu(conv).reshape(r, CONV_DIM)
    tail = cs_ref[:, HIST + lc - 3:HIST + lc, :]

    dt = proj_ref[:, D_INNER + CONV_DIM:D_INNER + CONV_DIM + LANES]
    dtp = _softplus(dt + dtb_ref[...])
    acum = dtp * (-jnp.exp(alog_ref[...]))
    local = lax.broadcasted_iota(jnp.int32, (r, LANES), 0) % lc
    sh = 1
    while sh < lc:
        acum = acum + jnp.where(local >= sh, pltpu.roll(acum, sh, axis=0), 0.0)
        sh *= 2
    acum_ref[...] = acum
    acumT_ref[...] = acum.T
    dtpT_ref[...] = dtp.T
    return tail


def _ssd_heads(g, mask, cb, yoff, xact_ref, acum_ref, acumT_ref, dtpT_ref, dsk_ref):
    r = xact_ref.shape[0]
    ys = []
    for rr in range(R_B):
        hh = g * R_B + rr
        colb = jnp.broadcast_to(acum_ref[:, hh:hh + 1], (r, r))
        seg = colb - acumT_ref[hh:hh + 1, :]
        m = cb * jnp.exp(jnp.where(mask, seg, -jnp.inf)) * dtpT_ref[hh:hh + 1, :]
        c0 = g * GP + rr * P_B
        xs_r = xact_ref[:, c0:c0 + P_B]
        y = _dot(m.astype(BF16), xs_r.astype(BF16))
        y = y + jnp.exp(colb[:, :P_B]) * yoff[:, rr * P_B:(rr + 1) * P_B]
        y = y + dsk_ref[:, c0:c0 + P_B] * xs_r
        ys.append(y)
    return jnp.concatenate(ys, axis=1)


def _ssd_gate_norm(y, z, ng):
    yz = y * _silu(z)
    ms = jnp.mean(yz * yz, axis=-1, keepdims=True)
    return (yz * lax.rsqrt(ms + EPS) * ng).astype(BF16)


def _head_rows(vals):
    return jnp.concatenate([jnp.broadcast_to(v, (P_B, N_B)) for v in vals], axis=0)


def _ssd_prompt_kernel(proj_ref, cw_ref, cb_ref, dtb_ref, alog_ref, dsk_ref, ng_ref,
                       ob_ref, cout_ref, hout_ref,
                       cs_ref, hs_ref, xact_ref, acum_ref, acumT_ref, dtpT_ref):
    c = pl.program_id(1)
    lc = CHUNK

    @pl.when(c == 0)
    def _():
        cs_ref[:, 0:HIST, :] = jnp.zeros((1, HIST, CONV_DIM), F32)
        hs_ref[...] = jnp.zeros(hs_ref.shape, F32)

    tail = _ssd_stage_a(proj_ref, cw_ref, cb_ref, dtb_ref, alog_ref, cs_ref, xact_ref, acum_ref,
                        acumT_ref, dtpT_ref, nb=1, lc=lc)
    cs_ref[:, HIST - 3:HIST, :] = tail

    @pl.when(c == pl.num_programs(1) - 1)
    def _():
        cout_ref[...] = tail[0]

    li = lax.broadcasted_iota(jnp.int32, (lc, lc), 0)
    si = lax.broadcasted_iota(jnp.int32, (lc, lc), 1)
    mask = li >= si
    acum_t = acumT_ref[...]
    wt = dtpT_ref[...] * jnp.exp(acum_t[:, lc - 1:lc] - acum_t)
    for g in range(G_B):
        bg = xact_ref[:, D_INNER + g * N_B:D_INNER + (g + 1) * N_B].astype(BF16)
        cg = xact_ref[:, D_INNER + GN + g * N_B:D_INNER + GN + (g + 1) * N_B].astype(BF16)
        cb = _dot_nt(cg, bg)
        hg = hs_ref[g]
        yoff = _dot_nt(cg, hg.astype(BF16))
        y = _ssd_heads(g, mask, cb, yoff, xact_ref, acum_ref, acumT_ref, dtpT_ref, dsk_ref)
        cols = slice(g * GP, (g + 1) * GP)
        ob_ref[:, cols] = _ssd_gate_norm(y, proj_ref[:, cols], ng_ref[:, cols])
        xs_t = xact_ref[:, cols].T
        xw = jnp.concatenate(
            [xs_t[rr * P_B:(rr + 1) * P_B, :] * wt[g * R_B + rr:g * R_B + rr + 1, :] for rr in range(R_B)],
            axis=0)
        inc = _dot(xw.astype(BF16), bg)
        dec = _head_rows([jnp.exp(acum_t[g * R_B + rr:g * R_B + rr + 1, lc - 1:lc]) for rr in range(R_B)])
        hs_ref[g] = dec * hg + inc

    @pl.when(c == pl.num_programs(1) - 1)
    def _():
        hout_ref[...] = hs_ref[...]


def _ssd_prompt(proj, cw, cb, dtb, alog, dsk, ng, batch, seq):
    n = proj.shape[1]
    nc = seq // CHUNK
    vec = lambda w: pl.BlockSpec((1, w), lambda b, c: (0, 0))
    return pl.pallas_call(
        _ssd_prompt_kernel,
        grid=(batch, nc),
        in_specs=[pl.BlockSpec((CHUNK, n), lambda b, c: (b * nc + c, 0)),
                  pl.BlockSpec((CONV_W, CONV_DIM), lambda b, c: (0, 0)),
                  vec(CONV_DIM), vec(LANES), vec(LANES), vec(D_INNER), vec(D_INNER)],
        out_specs=[pl.BlockSpec((CHUNK, D_INNER), lambda b, c: (b * nc + c, 0)),
                   pl.BlockSpec((None, CONV_W - 1, CONV_DIM), lambda b, c: (b, 0, 0)),
                   pl.BlockSpec((None, G_B, GP, N_B), lambda b, c: (b, 0, 0, 0))],
        out_shape=[jax.ShapeDtypeStruct((batch * seq, D_INNER), BF16),
                   jax.ShapeDtypeStruct((batch, CONV_W - 1, CONV_DIM), F32),
                   jax.ShapeDtypeStruct((batch, G_B, GP, N_B), F32)],
        scratch_shapes=[pltpu.VMEM((1, HIST + CHUNK, CONV_DIM), F32),
                        pltpu.VMEM((G_B, GP, N_B), F32),
                        pltpu.VMEM((CHUNK, CONV_DIM), F32),
                        pltpu.VMEM((CHUNK, LANES), F32),
                        pltpu.VMEM((LANES, CHUNK), F32),
                        pltpu.VMEM((LANES, CHUNK), F32)],
        compiler_params=_params(("arbitrary", "arbitrary"), 48),
        name="ssd_prompt",
    )(proj, cw, cb, dtb, alog, dsk, ng)


def _ssd_sample_kernel(proj_ref, cw_ref, cb_ref, dtb_ref, alog_ref, dsk_ref, ng_ref, cin_ref, hin_ref,
                       ob_ref, cout_ref, hout_ref,
                       cs_ref, xact_ref, acum_ref, acumT_ref, dtpT_ref, yoff_ref, *, nb, lc):
    g = pl.program_id(1)
    r = nb * lc

    @pl.when(g == 0)
    def _():
        cs_ref[:, HIST - 3:HIST, :] = cin_ref[...]
        tail = _ssd_stage_a(proj_ref, cw_ref, cb_ref, dtb_ref, alog_ref, cs_ref, xact_ref, acum_ref,
                            acumT_ref, dtpT_ref, nb=nb, lc=lc)
        cout_ref[...] = tail

    li = lax.broadcasted_iota(jnp.int32, (r, r), 0)
    si = lax.broadcasted_iota(jnp.int32, (r, r), 1)
    mask = (li >= si) & ((li // lc) == (si // lc))
    lane_seq = lax.broadcasted_iota(jnp.int32, (GP, r), 1) // lc
    acum_t = acumT_ref[...]
    dtp_t = dtpT_ref[...]

    def branch(gg):
        bg = xact_ref[:, D_INNER + gg * N_B:D_INNER + (gg + 1) * N_B].astype(BF16)
        cg = xact_ref[:, D_INNER + GN + gg * N_B:D_INNER + GN + (gg + 1) * N_B].astype(BF16)
        cb = _dot_nt(cg, bg)
        cols = slice(gg * GP, (gg + 1) * GP)
        xs_t = xact_ref[:, cols].T
        win = 2 * SUBLANES
        for b in range(nb):
            h0 = hin_ref[b, 0]
            w0 = (b * lc // win) * win
            yo = _dot_nt(cg[w0:w0 + win], h0.astype(BF16))
            yoff_ref[b * lc:(b + 1) * lc, :] = yo[b * lc - w0:b * lc - w0 + lc]
            last = b * lc + lc - 1
            xw = jnp.concatenate(
                [xs_t[rr * P_B:(rr + 1) * P_B, :]
                 * (dtp_t[gg * R_B + rr:gg * R_B + rr + 1, :]
                    * jnp.exp(acum_t[gg * R_B + rr:gg * R_B + rr + 1, last:last + 1]
                              - acum_t[gg * R_B + rr:gg * R_B + rr + 1, :]))
                 for rr in range(R_B)], axis=0)
            xw = jnp.where(lane_seq == b, xw, 0.0)
            inc = _dot(xw.astype(BF16), bg)
            dec = _head_rows([jnp.exp(acum_t[gg * R_B + rr:gg * R_B + rr + 1, last:last + 1])
                              for rr in range(R_B)])
            hout_ref[b, 0] = dec * h0 + inc
        y = _ssd_heads(gg, mask, cb, yoff_ref[...], xact_ref, acum_ref, acumT_ref, dtpT_ref, dsk_ref)
        ob_ref[...] = _ssd_gate_norm(y, proj_ref[:, cols], ng_ref[:, cols])

    for gg in range(G_B):
        pl.when(g == gg)(functools.partial(branch, gg))


def _ssd_sample(proj, row0, lc, cw, cb, dtb, alog, dsk, ng, conv_state, ssm_state, *, nb=16):
    n = proj.shape[1]
    batch = conv_state.shape[0]
    r = nb * lc
    blk0 = row0 // r
    vec = lambda w: pl.BlockSpec((1, w), lambda i, g: (0, 0))
    return pl.pallas_call(
        functools.partial(_ssd_sample_kernel, nb=nb, lc=lc),
        grid=(batch // nb, G_B),
        in_specs=[pl.BlockSpec((r, n), lambda i, g: (blk0 + i, 0)),
                  pl.BlockSpec((CONV_W, CONV_DIM), lambda i, g: (0, 0)),
                  vec(CONV_DIM), vec(LANES), vec(LANES), vec(D_INNER), vec(D_INNER),
                  pl.BlockSpec((nb, CONV_W - 1, CONV_DIM), lambda i, g: (i, 0, 0)),
                  pl.BlockSpec((nb, 1, GP, N_B), lambda i, g: (i, g, 0, 0))],
        out_specs=[pl.BlockSpec((r, GP), lambda i, g: (i, g)),
                   pl.BlockSpec((nb, CONV_W - 1, CONV_DIM), lambda i, g: (i, 0, 0)),
                   pl.BlockSpec((nb, 1, GP, N_B), lambda i, g: (i, g, 0, 0))],
        out_shape=[jax.ShapeDtypeStruct((batch * lc, D_INNER), BF16),
                   jax.ShapeDtypeStruct((batch, CONV_W - 1, CONV_DIM), F32),
                   jax.ShapeDtypeStruct((batch, G_B, GP, N_B), F32)],
        scratch_shapes=[pltpu.VMEM((nb, HIST + lc, CONV_DIM), F32),
                        pltpu.VMEM((r, CONV_DIM), F32),
                        pltpu.VMEM((r, LANES), F32),
                        pltpu.VMEM((LANES, r), F32),
                        pltpu.VMEM((LANES, r), F32),
                        pltpu.VMEM((r, GP), F32)],
        compiler_params=_params(("arbitrary", "arbitrary"), 48),
        name="ssd_sample",
    )(proj, cw, cb, dtb, alog, dsk, ng, conv_state, ssm_state)


def _mid_kernel(h_ref, oa_ref, ob_ref, x_ref, gt_ref, sc_ref, sh_ref, n2_ref,
                wg_ref, wpa_ref, wpb_ref, wo_ref, rw_ref, rb_ref,
                x1_ref, h2_ref, gates_ref, idx_ref, rank_ref, cnt_ref, carry_ref):
    tm, d = h2_ref.shape
    gab = _dot(h_ref[...], wg_ref[...])
    m = (_sigmoid(gab[:, :d]) * _dot(oa_ref[...], wpa_ref[...])
         + _sigmoid(gab[:, d:]) * _dot(ob_ref[...], wpb_ref[...]))
    mo = _dot(m.astype(BF16), wo_ref[...])
    x1 = x_ref[...] + gt_ref[...] * mo.reshape(x_ref.shape)
    x1_ref[...] = x1
    ms = jnp.mean(x1 * x1, axis=-1, keepdims=True)
    h2 = (x1 * lax.rsqrt(ms + EPS) * n2_ref[...] * (1.0 + sc_ref[...]) + sh_ref[...]).reshape(tm, d)
    h2_ref[...] = h2

    logits = jnp.dot(h2, rw_ref[...], preferred_element_type=F32,
                     precision=lax.Precision.HIGHEST) + rb_ref[...]
    lane = lax.broadcasted_iota(jnp.int32, logits.shape, 1).astype(F32)
    vals, idxs = [], []
    for _ in range(TOP_K):
        mx = jnp.max(logits, axis=-1, keepdims=True)
        ix = jnp.min(jnp.where(logits == mx, lane, float(LANES)), axis=-1, keepdims=True)
        vals.append(mx)
        idxs.append(ix)
        logits = jnp.where(lane == ix, -jnp.inf, logits)
    es = [jnp.exp(v - vals[0]) for v in vals]
    tot = es[0] + es[1] + es[2] + es[3]
    gates = jnp.zeros(logits.shape, F32)
    idx = jnp.zeros(logits.shape, F32)
    for k in range(TOP_K):
        gates = jnp.where(lane == float(k), es[k] / tot, gates)
        idx = jnp.where(lane == float(k), idxs[k], idx)
    gates_ref[...] = gates
    idx_ref[...] = idx.astype(jnp.int32)

    @pl.when(pl.program_id(0) == 0)
    def _():
        carry_ref[...] = jnp.zeros(carry_ref.shape, F32)

    earlier = (lax.broadcasted_iota(jnp.int32, (tm, tm), 1)
               < lax.broadcasted_iota(jnp.int32, (tm, tm), 0)).astype(BF16)
    onehots = [(lane == idxs[k]).astype(F32) for k in range(TOP_K)]
    prefix = _dot(earlier, jnp.concatenate(onehots, axis=1).astype(BF16))
    base = carry_ref[...]
    rank = jnp.zeros(logits.shape, F32)
    for k in range(TOP_K):
        before = prefix[:, k * LANES:(k + 1) * LANES] + base
        rank = jnp.where(lane == float(k), jnp.sum(onehots[k] * before, axis=-1, keepdims=True), rank)
        base = base + jnp.sum(onehots[k], axis=0, keepdims=True)
    carry_ref[...] = base
    rank_ref[...] = rank.astype(jnp.int32)
    cnt_ref[...] = base.astype(jnp.int32)


def _mid(h, oa, ob, x3, mod, n2, wg, wpa, wpb, wo, rw, rb, *, tm=256):
    t, d = h.shape
    nb8 = tm // SUBLANES
    row = lambda w: pl.BlockSpec((tm, w), lambda i: (i, 0))
    full = lambda a: pl.BlockSpec(a.shape, lambda i: (0,) * a.ndim)
    x_spec = pl.BlockSpec((nb8, SUBLANES, d), lambda i: (i, 0, 0))
    mod_spec = lambda k: pl.BlockSpec((None, nb8, 1, d), lambda i: (k, i, 0, 0))
    n2 = n2.reshape(1, d)
    return pl.pallas_call(
        _mid_kernel,
        grid=(t // tm,),
        in_specs=[row(d), row(d), row(2 * d), x_spec, mod_spec(2), mod_spec(4), mod_spec(3), full(n2),
                  full(wg), full(wpa), full(wpb), full(wo), full(rw), full(rb)],
        out_specs=[x_spec, row(d), row(LANES), row(LANES), row(LANES),
                   pl.BlockSpec((1, LANES), lambda i: (0, 0))],
        out_shape=[jax.ShapeDtypeStruct(x3.shape, F32),
                   jax.ShapeDtypeStruct((t, d), F32),
                   jax.ShapeDtypeStruct((t, LANES), F32),
                   jax.ShapeDtypeStruct((t, LANES), jnp.int32),
                   jax.ShapeDtypeStruct((t, LANES), jnp.int32),
                   jax.ShapeDtypeStruct((1, LANES), jnp.int32)],
        scratch_shapes=[pltpu.VMEM((1, LANES), F32)],
        compiler_params=_params(("arbitrary",), 56),
        name="mid",
    )(h, oa, ob, x3, mod, mod, mod, n2, wg, wpa, wpb, wo, rw, rb)


def _dispatch_kernel(fill0_ref, filln_ref, n_used_ref, pos_ref, h_ref, xb_ref, zrow, sem, zsem):
    i = pl.program_id(0)
    tm = h_ref.shape[0]
    n_e = filln_ref.shape[0]
    blk = zrow.shape[0]
    n_blocks = xb_ref.shape[0] // blk

    def scatter(t, c):
        for k in range(TOP_K):
            _row_copy(h_ref, t, xb_ref, pos_ref[k, t], sem).start()
        return c
    lax.fori_loop(0, tm, scatter, 0)

    @pl.when(i == pl.num_programs(0) - 1)
    def _():
        zrow[...] = jnp.zeros(zrow.shape, F32)

        def fill_expert(e, c):
            def fill(r, c2):
                _row_copy(zrow, 0, xb_ref, fill0_ref[e] + r, zsem).start()
                return c2
            return lax.fori_loop(0, filln_ref[e], fill, c)
        lax.fori_loop(0, n_e, fill_expert, 0)

        def drain_expert(e, c):
            def drain(r, c2):
                _row_copy(zrow, 0, xb_ref, 0, zsem).wait()
                return c2
            return lax.fori_loop(0, filln_ref[e], drain, c)
        lax.fori_loop(0, n_e, drain_expert, 0)

        def block_copy(b):
            return pltpu.make_async_copy(zrow, xb_ref.at[pl.ds(pl.multiple_of(b * blk, blk), blk)], zsem)

        def fill_block(b, c):
            block_copy(b).start()
            return c
        lax.fori_loop(n_used_ref[0], n_blocks, fill_block, 0)

        def drain_block(b, c):
            block_copy(b).wait()
            return c
        lax.fori_loop(n_used_ref[0], n_blocks, drain_block, 0)

    def drain(t, c):
        for k in range(TOP_K):
            _row_copy(h_ref, 0, xb_ref, 0, sem).wait()
        return c
    lax.fori_loop(0, tm, drain, 0)


def _dispatch(h2, pos, fill0, filln, n_used, rows, *, tm, blk):
    t, d = h2.shape
    grid_spec = pltpu.PrefetchScalarGridSpec(
        num_scalar_prefetch=3,
        grid=(t // tm,),
        in_specs=[pl.BlockSpec((None, TOP_K, tm), lambda i, f0, fn, nu: (i, 0, 0), memory_space=pltpu.SMEM),
                  pl.BlockSpec((tm, d), lambda i, f0, fn, nu: (i, 0))],
        out_specs=pl.BlockSpec(memory_space=pl.ANY),
        scratch_shapes=[pltpu.VMEM((blk, d), F32), pltpu.SemaphoreType.DMA(()),
                        pltpu.SemaphoreType.DMA(())])
    return pl.pallas_call(
        _dispatch_kernel,
        grid_spec=grid_spec,
        out_shape=jax.ShapeDtypeStruct((rows, d), F32),
        compiler_params=_params(("arbitrary",), 32),
        name="dispatch",
    )(fill0, filln, n_used, pos, h2)


def _moe_kernel(blk_e_ref, blk_first_ref, n_used_ref, x_ref, wgu_ref, bgu_ref, wdn_ref, bdn_ref, o_ref,
                wgu_bf, wdn_bf):
    i = pl.program_id(0)
    d_ff = wdn_bf.shape[0]

    @pl.when(blk_first_ref[i] == 1)
    def _():
        wgu_bf[...] = wgu_ref[...].astype(BF16)
        wdn_bf[...] = wdn_ref[...].astype(BF16)

    @pl.when(i < n_used_ref[0])
    def _():
        gu = _dot(x_ref[...].astype(BF16), wgu_bf[...]) + bgu_ref[...]
        g = jnp.minimum(gu[:, :d_ff], SWIGLU_LIMIT)
        u = jnp.clip(gu[:, d_ff:], -SWIGLU_LIMIT, SWIGLU_LIMIT)
        act = g * _sigmoid(SWIGLU_ALPHA * g) * (u + 1.0)
        o_ref[...] = _dot(act.astype(BF16), wdn_bf[...]) + bdn_ref[...]

    @pl.when(i >= n_used_ref[0])
    def _():
        o_ref[...] = jnp.zeros(o_ref.shape, F32)


def _moe(xb, blk_e, blk_first, n_used, w_gu, b_gu, w_dn, b_dn, *, tm):
    rows, d = xb.shape
    n_e, _, d_gu = w_gu.shape
    d_ff = w_dn.shape[1]
    grid_spec = pltpu.PrefetchScalarGridSpec(
        num_scalar_prefetch=3,
        grid=(rows // tm,),
        in_specs=[pl.BlockSpec((tm, d), lambda i, be, bf, nu: (jnp.minimum(i, nu[0] - 1), 0)),
                  pl.BlockSpec((None, d, d_gu), lambda i, be, bf, nu: (be[i], 0, 0)),
                  pl.BlockSpec((None, 1, d_gu), lambda i, be, bf, nu: (be[i], 0, 0)),
                  pl.BlockSpec((None, d_ff, d), lambda i, be, bf, nu: (be[i], 0, 0)),
                  pl.BlockSpec((None, 1, d), lambda i, be, bf, nu: (be[i], 0, 0))],
        out_specs=pl.BlockSpec((tm, d), lambda i, be, bf, nu: (i, 0)),
        scratch_shapes=[pltpu.VMEM((d, d_gu), BF16), pltpu.VMEM((d_ff, d), BF16)])
    return pl.pallas_call(
        _moe_kernel,
        grid_spec=grid_spec,
        out_shape=jax.ShapeDtypeStruct((rows, d), F32),
        compiler_params=_params(("arbitrary",), 56),
        name="moe",
    )(blk_e, blk_first, n_used, xb, w_gu, b_gu.reshape(n_e, 1, d_gu), w_dn, b_dn.reshape(n_e, 1, d))


def _route(idx, rank, counts, n_e, tm, tm_tok):
    t = idx.shape[0]
    padded = (counts + tm - 1) // tm * tm
    pad_end = jnp.cumsum(padded)
    pad_start = pad_end - padded
    experts = jnp.arange(n_e, dtype=jnp.int32)
    pos = jnp.sum(jnp.where(idx[..., None] == experts, pad_start, 0), axis=-1) + rank
    pos = pos.reshape(t // tm_tok, tm_tok, TOP_K).transpose(0, 2, 1).astype(jnp.int32)
    n_blocks = -(-t * TOP_K // tm) + n_e
    blk_start = jnp.arange(n_blocks, dtype=jnp.int32) * tm
    blk_e = jnp.minimum(jnp.sum((blk_start[:, None] >= pad_end[None, :]).astype(jnp.int32), axis=1),
                        n_e - 1).astype(jnp.int32)
    blk_first = jnp.concatenate([jnp.ones((1,), jnp.int32),
                                 (blk_e[1:] != blk_e[:-1]).astype(jnp.int32)])
    n_used = (pad_end[-1:] // tm).astype(jnp.int32)
    return (pos, blk_e, blk_first, n_used, (pad_start + counts).astype(jnp.int32),
            (padded - counts).astype(jnp.int32), n_blocks * tm)


def kernel(x_prompt, x_sample, c_prompt, c_sample, state_ssm, state_conv, norm1_g, norm2_g, final_g,
           w_ada, b_ada, w_in, a_ln_g, a_ln_b, a_ws, a_bs, b_conv_w, b_conv_b, b_dt_bias, b_a_log, b_d,
           b_norm_g, w_proj_a, w_proj_b, w_out, router_w, router_b, w_gu, b_gu, w_dn, b_dn):
    bp, seq, d = x_prompt.shape
    bs, lc_s, _ = x_sample.shape
    depth = w_ada.shape[0]
    n_e = router_w.shape[-1]
    heads = b_d.shape[-1]
    t_p, t_s = bp * seq, bs * lc_s
    t = t_p + t_s
    g8 = t // SUBLANES
    moe_tm = 256
    tok_tm = 256

    d_a = d
    c0 = 2 * d_a
    c1 = c0 + D_INNER + CONV_DIM + heads
    w_uv = w_in[:, :, :c0].astype(BF16)
    w_ssd = jnp.pad(w_in[:, :, c0:c1], ((0, 0), (0, 0), (0, LANES - heads))).astype(BF16)
    w_g = w_in[:, :, c1:].astype(BF16)
    w_pa, w_pb, w_o = w_proj_a.astype(BF16), w_proj_b.astype(BF16), w_out.astype(BF16)
    rw = jnp.pad(router_w, ((0, 0), (0, 0), (0, LANES - n_e)))
    rb = jnp.pad(router_b, ((0, 0), (0, LANES - n_e)), constant_values=-jnp.inf).reshape(depth, 1, LANES)
    causal = jnp.tril(jnp.ones((CHUNK, CHUNK), bool))
    ws_p = jnp.where(causal, a_ws, 0.0)
    blk = jnp.where(jnp.tril(jnp.ones((lc_s, lc_s), bool)), a_ws[:, :, :lc_s, :lc_s], 0.0)
    eye = jnp.eye(CHUNK // lc_s, dtype=F32)
    ws_s = jnp.einsum("ab,lgts->lgatbs", eye, blk).reshape(depth, G_A, CHUNK, CHUNK)
    wmix = jnp.stack([ws_p, ws_s], axis=1).astype(BF16)
    dg = d_a // G_A
    bias_p = jnp.repeat(jnp.swapaxes(a_bs, 1, 2), dg, axis=2)
    bias_s = jnp.tile(bias_p[:, :lc_s], (1, CHUNK // lc_s, 1))
    bias = jnp.stack([bias_p, bias_s], axis=1)
    dtb = jnp.pad(b_dt_bias, ((0, 0), (0, LANES - heads))).reshape(depth, 1, LANES)
    alog = jnp.pad(b_a_log, ((0, 0), (0, LANES - heads))).reshape(depth, 1, LANES)
    dsk = jnp.repeat(b_d, P_B, axis=1).reshape(depth, 1, D_INNER)
    cb = b_conv_b.reshape(depth, 1, CONV_DIM)
    ng = b_norm_g.reshape(depth, 1, D_INNER)

    mod_all = _ada(jnp.concatenate([c_prompt, c_sample], axis=0), w_ada, b_ada)
    mod_g = jnp.concatenate([jnp.repeat(mod_all[:, :bp], seq // SUBLANES, axis=1),
                             jnp.repeat(mod_all[:, bp:], lc_s // SUBLANES, axis=1)], axis=1)
    mod_g = mod_g.reshape(depth, g8, 6, d).transpose(0, 2, 1, 3).reshape(depth, 6, g8, 1, d)

    x3 = jnp.concatenate([x_prompt.reshape(t_p, d), x_sample.reshape(t_s, d)], axis=0).reshape(g8, SUBLANES, d)
    ssm_s_in = state_ssm.reshape(depth, bs, G_B, GP, N_B)

    h = _comb(x3, norm1_g[0], mod_g[0], k_sc=1, k_sh=0)[0]
    ssm_p, conv_p, ssm_s, conv_s, v_s = [], [], [], [], []
    y_final = None
    for l in range(depth):
        oa, v_rows = _gmlp(h, w_uv[l], a_ln_g[l], a_ln_b[l], wmix[l], bias[l], t_p)
        proj = _proj(h, w_ssd[l])
        ob_p, cp, sp = _ssd_prompt(proj, b_conv_w[l], cb[l], dtb[l], alog[l], dsk[l], ng[l], bp, seq)
        ob_s, cs, ss = _ssd_sample(proj, t_p, lc_s, b_conv_w[l], cb[l], dtb[l], alog[l], dsk[l], ng[l],
                                   state_conv[l], ssm_s_in[l])
        ob = jnp.concatenate([ob_p, ob_s], axis=0)
        x1, h2, gates, idx, rank, cnt = _mid(h, oa, ob, x3, mod_g[l], norm2_g[l], w_g[l], w_pa[l], w_pb[l],
                                             w_o[l], rw[l], rb[l], tm=tok_tm)
        pos, blk_e, blk_first, n_used, fill0, filln, rows = _route(
            idx[:, :TOP_K], rank[:, :TOP_K], cnt[0, :n_e], n_e, moe_tm, tok_tm)
        xb = _dispatch(h2, pos, fill0, filln, n_used, rows, tm=tok_tm, blk=moe_tm)
        yb = _moe(xb, blk_e, blk_first, n_used, w_gu[l], b_gu[l], w_dn[l], b_dn[l], tm=moe_tm)
        gates3 = gates.reshape(g8, SUBLANES, LANES)
        if l + 1 < depth:
            x3, h = _comb(x1, norm1_g[l + 1], mod_g[l + 1], k_sc=1, k_sh=0, yb=yb, pos=pos, gates3=gates3,
                          mod_gt=mod_g[l], k_gt=5, tm=tok_tm)
        else:
            y_final = _comb(x1, final_g, yb=yb, pos=pos, gates3=gates3, mod_gt=mod_g[l], k_gt=5,
                            tm=tok_tm)[0]
        ssm_p.append(sp.reshape(bp, heads, P_B, N_B))
        conv_p.append(cp)
        ssm_s.append(ss.reshape(bs, heads, P_B, N_B))
        conv_s.append(cs)
        v_s.append(v_rows.reshape(bs, lc_s, d))
    y_final = y_final.reshape(t, d)
    return (y_final[:t_p].reshape(bp, seq, d), y_final[t_p:].reshape(bs, lc_s, d),
            jnp.stack(ssm_p), jnp.stack(conv_p), jnp.stack(ssm_s), jnp.stack(conv_s), jnp.stack(v_s))
```

```python
import functools

import jax
import jax.numpy as jnp
from jax import lax
from jax.experimental import pallas as pl
from jax.experimental.pallas import tpu as pltpu

F32 = jnp.float32
BF16 = jnp.bfloat16

EPS = 1e-6
SUBLANES = 8
LANES = 128
CHUNK = 128
G_A = 8
G_B = 8
R_B = 4
P_B = 64
N_B = 128
CONV_W = 4
TOP_K = 4
SWIGLU_LIMIT = 7.0
SWIGLU_ALPHA = 1.702
MIB = 2 ** 20
N_MOD = 6
K_SH1, K_SC1, K_GT1, K_SH2, K_SC2, K_GT2 = range(N_MOD)
DMA_UNROLL = 8


def _params(sem, vmem_mib):
    return pltpu.CompilerParams(dimension_semantics=sem, vmem_limit_bytes=vmem_mib * MIB)


def _sigmoid(x):
    return 1.0 / (1.0 + jnp.exp(-x))


def _silu(x):
    return x * _sigmoid(x)


def _gelu(x):
    return 0.5 * x * (1.0 + lax.erf(x * (0.5 ** 0.5)))


def _softplus(x):
    return jnp.maximum(x, 0.0) + jnp.log(1.0 + jnp.exp(-jnp.abs(x)))


def _dot(a, b):
    return jnp.dot(a, b, preferred_element_type=F32)


def _dot_nt(a, b):
    return lax.dot_general(a, b, (((1,), (1,)), ((), ())), preferred_element_type=F32)


class _Tokens:
    def __init__(self, bp, seq, bs, lc_s, tm):
        self.tm = tm
        self.t_p, self.t_s = bp * seq, bs * lc_s
        self.t = self.t_p + self.t_s
        self.n_tiles = self.t // tm
        self.n_p = self.t_p // tm
        self.per_seq = seq // tm
        self.bp = bp

    def mod_block(self, i):
        return jnp.where(i < self.n_p, i // self.per_seq, self.bp + i - self.n_p)

    def mod_spec(self, layer, k, d):
        return pl.BlockSpec((None, None, self.tm, d), lambda i: (layer, k, self.mod_block(i), 0))


def _ada_kernel(c_ref, w_ref, b_ref, o_ref):
    c = c_ref[...]
    o_ref[...] = _dot(_silu(c).astype(BF16), w_ref[...].astype(BF16)) + b_ref[...]


def _ada(c_all, w_ada, b_ada):
    depth, d, n = w_ada.shape
    nc = c_all.shape[0]
    tn = 1536
    return pl.pallas_call(
        _ada_kernel,
        grid=(depth, n // tn),
        in_specs=[pl.BlockSpec((nc, d), lambda l, j: (0, 0)),
                  pl.BlockSpec((None, d, tn), lambda l, j: (l, 0, j)),
                  pl.BlockSpec((None, 1, tn), lambda l, j: (l, 0, j))],
        out_specs=pl.BlockSpec((None, nc, tn), lambda l, j: (l, 0, j)),
        out_shape=jax.ShapeDtypeStruct((depth, nc, n), F32),
        compiler_params=_params(("arbitrary", "arbitrary"), 40),
        name="ada",
    )(c_all, w_ada, b_ada.reshape(depth, 1, n))


def _row_copy(src_ref, src_row, dst_ref, dst_row, sem):
    return pltpu.make_async_copy(src_ref.at[pl.ds(src_row, 1)], dst_ref.at[pl.ds(dst_row, 1)], sem)


def _comb_kernel(*refs, combine, final):
    it = iter(refs)
    if combine:
        pos_ref, posn_ref = next(it), next(it)
    x_ref = next(it)
    if combine:
        yb_ref, g_ref, gt_ref = next(it), next(it), next(it)
    n_ref = next(it)
    if not final:
        sc_ref, sh_ref = next(it), next(it)
    if combine and not final:
        xo_ref = next(it)
    o_ref = next(it)
    if combine:
        ybuf, sem = next(it), next(it)

    x = x_ref[...]
    if combine:
        i = pl.program_id(0)
        n = pl.num_programs(0)
        tm = ybuf.shape[2]
        slot = i % 2

        def gather(p_ref, s):
            def body(j, c):
                for u in range(DMA_UNROLL):
                    t = j * DMA_UNROLL + u
                    for k in range(TOP_K):
                        _row_copy(yb_ref, p_ref[k, t], ybuf.at[s, k], t, sem.at[s]).start()
                return c
            lax.fori_loop(0, tm // DMA_UNROLL, body, 0)

        @pl.when(i == 0)
        def _():
            gather(pos_ref, 0)

        @pl.when(i + 1 < n)
        def _():
            gather(posn_ref, 1 - slot)

        for k in range(TOP_K):
            pltpu.make_async_copy(yb_ref.at[pl.ds(0, tm)], ybuf.at[slot, k], sem.at[slot]).wait()

        g = g_ref[...]
        acc = g[:, 0:1] * ybuf[slot, 0]
        for k in range(1, TOP_K):
            acc = acc + g[:, k:k + 1] * ybuf[slot, k]
        x = x + gt_ref[...] * acc
        if not final:
            xo_ref[...] = x
    ms = jnp.mean(x * x, axis=-1, keepdims=True)
    xn = x * lax.rsqrt(ms + EPS)
    if final:
        o_ref[...] = xn * n_ref[...]
    else:
        o_ref[...] = (xn * n_ref[...] * (1.0 + sc_ref[...]) + sh_ref[...]).astype(BF16)


def _comb(tok, x, norm_g, mod=None, layer_n=None, yb=None, pos=None, gates=None, layer_gt=None):
    t, d = x.shape
    tm = tok.tm
    combine = yb is not None
    final = layer_n is None
    row = lambda w: pl.BlockSpec((tm, w), lambda i: (i, 0))

    args, specs, scratch = [], [], []
    if combine:
        args += [pos, pos]
        specs += [pl.BlockSpec((None, TOP_K, tm), lambda i: (i, 0, 0), memory_space=pltpu.SMEM),
                  pl.BlockSpec((None, TOP_K, tm), lambda i: (jnp.minimum(i + 1, tok.n_tiles - 1), 0, 0),
                               memory_space=pltpu.SMEM)]
        scratch = [pltpu.VMEM((2, TOP_K, tm, d), F32), pltpu.SemaphoreType.DMA((2,))]
    args.append(x)
    specs.append(row(d))
    if combine:
        args += [yb, gates, mod]
        specs += [pl.BlockSpec(memory_space=pl.ANY), row(LANES), tok.mod_spec(layer_gt, K_GT2, d)]
    args.append(norm_g.reshape(1, d))
    specs.append(pl.BlockSpec((1, d), lambda i: (0, 0)))
    if not final:
        args += [mod, mod]
        specs += [tok.mod_spec(layer_n, K_SC1, d), tok.mod_spec(layer_n, K_SH1, d)]
    out_shape, out_specs = [], []
    if combine and not final:
        out_shape.append(jax.ShapeDtypeStruct((t, d), F32))
        out_specs.append(row(d))
    out_shape.append(jax.ShapeDtypeStruct((t, d), F32 if final else BF16))
    out_specs.append(row(d))
    return pl.pallas_call(
        functools.partial(_comb_kernel, combine=combine, final=final),
        grid=(tok.n_tiles,),
        in_specs=specs, out_specs=out_specs, out_shape=out_shape,
        scratch_shapes=scratch,
        compiler_params=_params(("arbitrary",), 48),
        name="comb",
    )(*args)


def _gmlp_kernel(h_ref, w_ref, lng_ref, lnb_ref, wmix_ref, bias_ref, o_ref, v_ref, *, n_prompt_tiles):
    i = pl.program_id(0)
    tm, d = o_ref.shape
    uv = _dot(h_ref[...], w_ref[...])
    u = _gelu(uv[:, :d])
    v = _gelu(uv[:, d:])
    mu = jnp.mean(v, axis=-1, keepdims=True)
    vc = v - mu
    var = jnp.mean(vc * vc, axis=-1, keepdims=True)
    vn = vc * lax.rsqrt(var + EPS) * lng_ref[...] + lnb_ref[...]

    @pl.when(i >= n_prompt_tiles)
    def _():
        v_ref[...] = vn

    vb = vn.astype(BF16)
    dg = d // G_A
    for c in range(tm // CHUNK):
        rows = slice(c * CHUNK, (c + 1) * CHUNK)
        for g in range(G_A):
            cols = slice(g * dg, (g + 1) * dg)
            s = _dot(wmix_ref[g], vb[rows, cols]) + bias_ref[:, cols]
            o_ref[rows, cols] = (u[rows, cols] * s).astype(BF16)


def _gmlp(h, w_uv, ln_g, ln_b, wmix, bias, t_prompt, *, tm=512):
    t, d = h.shape
    n_p = t_prompt // tm
    t_s = t - t_prompt
    sel = lambda i: jnp.where(i >= n_p, 1, 0)
    return pl.pallas_call(
        functools.partial(_gmlp_kernel, n_prompt_tiles=n_p),
        grid=(t // tm,),
        in_specs=[pl.BlockSpec((tm, d), lambda i: (i, 0)),
                  pl.BlockSpec((d, 2 * d), lambda i: (0, 0)),
                  pl.BlockSpec((1, d), lambda i: (0, 0)),
                  pl.BlockSpec((1, d), lambda i: (0, 0)),
                  pl.BlockSpec((None, G_A, CHUNK, CHUNK), lambda i: (sel(i), 0, 0, 0)),
                  pl.BlockSpec((None, CHUNK, d), lambda i: (sel(i), 0, 0))],
        out_specs=[pl.BlockSpec((tm, d), lambda i: (i, 0)),
                   pl.BlockSpec((tm, d), lambda i: (jnp.maximum(i - n_p, 0), 0))],
        out_shape=[jax.ShapeDtypeStruct((t, d), BF16),
                   jax.ShapeDtypeStruct((t_s, d), F32)],
        compiler_params=_params(("arbitrary",), 48),
        name="gmlp",
    )(h, w_uv, ln_g.reshape(1, d), ln_b.reshape(1, d), wmix, bias)


def _proj_kernel(x_ref, w_ref, o_ref):
    o_ref[...] = _dot(x_ref[...], w_ref[...])


def _proj(x, w, *, tm=256):
    t, d = x.shape
    n = w.shape[1]
    return pl.pallas_call(
        _proj_kernel,
        grid=(t // tm,),
        in_specs=[pl.BlockSpec((tm, d), lambda i: (i, 0)),
                  pl.BlockSpec((d, n), lambda i: (0, 0))],
        out_specs=pl.BlockSpec((tm, n), lambda i: (i, 0)),
        out_shape=jax.ShapeDtypeStruct((t, n), F32),
        compiler_params=_params(("arbitrary",), 56),
        name="inproj",
    )(x, w)


D_INNER = G_B * R_B * P_B
GN = G_B * N_B
CONV_DIM = D_INNER + 2 * GN
GP = R_B * P_B
HIST = SUBLANES


def _ssd_stage_a(proj_ref, cw_ref, cb_ref, dtb_ref, alog_ref, cs_ref, xact_ref, acum_ref, acumT_ref,
                 dtpT_ref, *, nb, lc):
    r = nb * lc
    cs_ref[:, HIST:HIST + lc, :] = proj_ref[:, D_INNER:D_INNER + CONV_DIM].reshape(nb, lc, CONV_DIM)
    conv = cb_ref[...] + cs_ref[:, HIST - 3:HIST - 3 + lc, :] * cw_ref[0:1, :]
    for k in range(1, CONV_W):
        conv = conv + cs_ref[:, HIST - 3 + k:HIST - 3 + k + lc, :] * cw_ref[k:k + 1, :]
    xact_ref[...] = _silu(conv).reshape(r, CONV_DIM)
    tail = cs_ref[:, HIST + lc - 3:HIST + lc, :]

    dt = proj_ref[:, D_INNER + CONV_DIM:D_INNER + CONV_DIM + LANES]
    dtp = _softplus(dt + dtb_ref[...])
    acum = dtp * (-jnp.exp(alog_ref[...]))
    local = lax.broadcasted_iota(jnp.int32, (r, LANES), 0) % lc
    sh = 1
    while sh < lc:
        acum = acum + jnp.where(local >= sh, pltpu.roll(acum, sh, axis=0), 0.0)
        sh *= 2
    acum_ref[...] = acum
    acumT_ref[...] = acum.T
    dtpT_ref[...] = dtp.T
    return tail


def _ssd_heads(g, mask, cb, yoff, xact_ref, acum_ref, acumT_ref, dtpT_ref, dsk_ref):
    r = xact_ref.shape[0]
    first = lax.broadcasted_iota(jnp.int32, (r, LANES), 1) < P_B
    ys = []
    for pair in range(R_B // 2):
        c0 = g * GP + pair * LANES
        xs = xact_ref[:, c0:c0 + LANES]
        xs_bf = xs.astype(BF16)
        colbs, ds = [], []
        for rr in range(2):
            hh = g * R_B + pair * 2 + rr
            colb = jnp.broadcast_to(acum_ref[:, hh:hh + 1], (r, r))
            seg = colb - acumT_ref[hh:hh + 1, :]
            m = cb * jnp.exp(jnp.where(mask, seg, -jnp.inf)) * dtpT_ref[hh:hh + 1, :]
            colbs.append(colb)
            ds.append(_dot(m.astype(BF16), xs_bf))
        y = jnp.where(first, ds[0], ds[1])
        y = y + jnp.exp(jnp.where(first, colbs[0], colbs[1])) * yoff[:, pair * LANES:(pair + 1) * LANES]
        y = y + dsk_ref[:, c0:c0 + LANES] * xs
        ys.append(y)
    return jnp.concatenate(ys, axis=1)


def _ssd_gate_norm(y, z, ng):
    yz = y * _silu(z)
    ms = jnp.mean(yz * yz, axis=-1, keepdims=True)
    return (yz * lax.rsqrt(ms + EPS) * ng).astype(BF16)


def _head_rows(vals):
    return jnp.concatenate([jnp.broadcast_to(v, (P_B, N_B)) for v in vals], axis=0)


def _ssd_prompt_kernel(proj_ref, cw_ref, cb_ref, dtb_ref, alog_ref, dsk_ref, ng_ref,
                       ob_ref, cout_ref, hout_ref,
                       cs_ref, hs_ref, xact_ref, acum_ref, acumT_ref, dtpT_ref):
    c = pl.program_id(1)
    lc = CHUNK

    @pl.when(c == 0)
    def _():
        cs_ref[:, 0:HIST, :] = jnp.zeros((1, HIST, CONV_DIM), F32)
        hs_ref[...] = jnp.zeros(hs_ref.shape, F32)

    tail = _ssd_stage_a(proj_ref, cw_ref, cb_ref, dtb_ref, alog_ref, cs_ref, xact_ref, acum_ref,
                        acumT_ref, dtpT_ref, nb=1, lc=lc)
    cs_ref[:, HIST - 3:HIST, :] = tail

    @pl.when(c == pl.num_programs(1) - 1)
    def _():
        cout_ref[...] = tail[0]

    li = lax.broadcasted_iota(jnp.int32, (lc, lc), 0)
    si = lax.broadcasted_iota(jnp.int32, (lc, lc), 1)
    mask = li >= si
    acum_t = acumT_ref[...]
    wt = dtpT_ref[...] * jnp.exp(acum_t[:, lc - 1:lc] - acum_t)
    for g in range(G_B):
        bg = xact_ref[:, D_INNER + g * N_B:D_INNER + (g + 1) * N_B].astype(BF16)
        cg = xact_ref[:, D_INNER + GN + g * N_B:D_INNER + GN + (g + 1) * N_B].astype(BF16)
        cb = _dot_nt(cg, bg)
        hg = hs_ref[g]
        yoff = _dot_nt(cg, hg.astype(BF16))
        y = _ssd_heads(g, mask, cb, yoff, xact_ref, acum_ref, acumT_ref, dtpT_ref, dsk_ref)
        cols = slice(g * GP, (g + 1) * GP)
        ob_ref[:, cols] = _ssd_gate_norm(y, proj_ref[:, cols], ng_ref[:, cols])
        xs_t = xact_ref[:, cols].T
        xw = jnp.concatenate(
            [xs_t[rr * P_B:(rr + 1) * P_B, :] * wt[g * R_B + rr:g * R_B + rr + 1, :] for rr in range(R_B)],
            axis=0)
        inc = _dot(xw.astype(BF16), bg)
        dec = _head_rows([jnp.exp(acum_t[g * R_B + rr:g * R_B + rr + 1, lc - 1:lc]) for rr in range(R_B)])
        hs_ref[g] = dec * hg + inc

    @pl.when(c == pl.num_programs(1) - 1)
    def _():
        hout_ref[...] = hs_ref[...]


def _ssd_prompt(proj, cw, cb, dtb, alog, dsk, ng, batch, seq):
    t, n = proj.shape
    nc = seq // CHUNK
    vec = lambda w: pl.BlockSpec((1, w), lambda b, c: (0, 0))
    return pl.pallas_call(
        _ssd_prompt_kernel,
        grid=(batch, nc),
        in_specs=[pl.BlockSpec((CHUNK, n), lambda b, c: (b * nc + c, 0)),
                  pl.BlockSpec((CONV_W, CONV_DIM), lambda b, c: (0, 0)),
                  vec(CONV_DIM), vec(LANES), vec(LANES), vec(D_INNER), vec(D_INNER)],
        out_specs=[pl.BlockSpec((CHUNK, D_INNER), lambda b, c: (b * nc + c, 0)),
                   pl.BlockSpec((None, CONV_W - 1, CONV_DIM), lambda b, c: (b, 0, 0)),
                   pl.BlockSpec((None, G_B, GP, N_B), lambda b, c: (b, 0, 0, 0))],
        out_shape=[jax.ShapeDtypeStruct((t, D_INNER), BF16),
                   jax.ShapeDtypeStruct((batch, CONV_W - 1, CONV_DIM), F32),
                   jax.ShapeDtypeStruct((batch, G_B, GP, N_B), F32)],
        scratch_shapes=[pltpu.VMEM((1, HIST + CHUNK, CONV_DIM), F32),
                        pltpu.VMEM((G_B, GP, N_B), F32),
                        pltpu.VMEM((CHUNK, CONV_DIM), F32),
                        pltpu.VMEM((CHUNK, LANES), F32),
                        pltpu.VMEM((LANES, CHUNK), F32),
                        pltpu.VMEM((LANES, CHUNK), F32)],
        compiler_params=_params(("arbitrary", "arbitrary"), 48),
        name="ssd_prompt",
    )(proj, cw, cb, dtb, alog, dsk, ng)


def _ssd_sample_kernel(*refs, nb, lc, chained):
    (proj_ref, cw_ref, cb_ref, dtb_ref, alog_ref, dsk_ref, ng_ref, cin_ref, hin_ref) = refs[:9]
    n_alias = 2 if chained else 1
    (ob_ref, cout_ref, hout_ref,
     cs_ref, xact_ref, acum_ref, acumT_ref, dtpT_ref, yoff_ref) = refs[9 + n_alias:]
    g = pl.program_id(1)
    r = nb * lc

    @pl.when(g == 0)
    def _():
        cs_ref[:, HIST - 3:HIST, :] = cin_ref[...]
        tail = _ssd_stage_a(proj_ref, cw_ref, cb_ref, dtb_ref, alog_ref, cs_ref, xact_ref, acum_ref,
                            acumT_ref, dtpT_ref, nb=nb, lc=lc)
        cout_ref[...] = tail

    li = lax.broadcasted_iota(jnp.int32, (r, r), 0)
    si = lax.broadcasted_iota(jnp.int32, (r, r), 1)
    mask = (li >= si) & ((li // lc) == (si // lc))
    lane_seq = lax.broadcasted_iota(jnp.int32, (GP, r), 1) // lc
    acum_t = acumT_ref[...]
    dtp_t = dtpT_ref[...]

    def branch(gg):
        bg = xact_ref[:, D_INNER + gg * N_B:D_INNER + (gg + 1) * N_B].astype(BF16)
        cg = xact_ref[:, D_INNER + GN + gg * N_B:D_INNER + GN + (gg + 1) * N_B].astype(BF16)
        cb = _dot_nt(cg, bg)
        cols = slice(gg * GP, (gg + 1) * GP)
        xs_t = xact_ref[:, cols].T
        win = 2 * SUBLANES
        for b in range(nb):
            h0 = hin_ref[b, 0]
            w0 = (b * lc // win) * win
            yo = _dot_nt(cg[w0:w0 + win], h0.astype(BF16))
            yoff_ref[b * lc:(b + 1) * lc, :] = yo[b * lc - w0:b * lc - w0 + lc]
            last = b * lc + lc - 1
            xw = jnp.concatenate(
                [xs_t[rr * P_B:(rr + 1) * P_B, :]
                 * (dtp_t[gg * R_B + rr:gg * R_B + rr + 1, :]
                    * jnp.exp(acum_t[gg * R_B + rr:gg * R_B + rr + 1, last:last + 1]
                              - acum_t[gg * R_B + rr:gg * R_B + rr + 1, :]))
                 for rr in range(R_B)], axis=0)
            xw = jnp.where(lane_seq == b, xw, 0.0)
            inc = _dot(xw.astype(BF16), bg)
            dec = _head_rows([jnp.exp(acum_t[gg * R_B + rr:gg * R_B + rr + 1, last:last + 1])
                              for rr in range(R_B)])
            hout_ref[b, 0] = dec * h0 + inc
        y = _ssd_heads(gg, mask, cb, yoff_ref[...], xact_ref, acum_ref, acumT_ref, dtpT_ref, dsk_ref)
        ob_ref[...] = _ssd_gate_norm(y, proj_ref[:, cols], ng_ref[:, cols])

    for gg in range(G_B):
        pl.when(g == gg)(functools.partial(branch, gg))


def _ssd_sample(proj, row0, lc, cw, cb, dtb, alog, dsk, ng, conv_state, ssm_state, layer, ob, ssm_out,
                *, nb=16):
    n = proj.shape[1]
    batch = (proj.shape[0] - row0) // lc
    r = nb * lc
    blk0 = row0 // r
    sb0 = layer * (batch // nb)
    chained = ssm_out is not None
    vec = lambda w: pl.BlockSpec((1, w), lambda i, g: (0, 0))
    state_spec = pl.BlockSpec((nb, 1, GP, N_B), lambda i, g: (sb0 + i, g, 0, 0))
    hbm = pl.BlockSpec(memory_space=pl.ANY)
    args = [proj, cw, cb, dtb, alog, dsk, ng, conv_state, ssm_state, ob] + ([ssm_out] if chained else [])
    aliases = {9: 0, 10: 2} if chained else {9: 0}
    return pl.pallas_call(
        functools.partial(_ssd_sample_kernel, nb=nb, lc=lc, chained=chained),
        grid=(batch // nb, G_B),
        in_specs=[pl.BlockSpec((r, n), lambda i, g: (blk0 + i, 0)),
                  pl.BlockSpec((CONV_W, CONV_DIM), lambda i, g: (0, 0)),
                  vec(CONV_DIM), vec(LANES), vec(LANES), vec(D_INNER), vec(D_INNER),
                  pl.BlockSpec((nb, CONV_W - 1, CONV_DIM), lambda i, g: (sb0 + i, 0, 0)),
                  state_spec, hbm] + ([hbm] if chained else []),
        out_specs=[pl.BlockSpec((r, GP), lambda i, g: (blk0 + i, g)),
                   pl.BlockSpec((nb, CONV_W - 1, CONV_DIM), lambda i, g: (i, 0, 0)),
                   state_spec],
        out_shape=[jax.ShapeDtypeStruct(ob.shape, BF16),
                   jax.ShapeDtypeStruct((batch, CONV_W - 1, CONV_DIM), F32),
                   jax.ShapeDtypeStruct(ssm_state.shape, F32)],
        input_output_aliases=aliases,
        scratch_shapes=[pltpu.VMEM((nb, HIST + lc, CONV_DIM), F32),
                        pltpu.VMEM((r, CONV_DIM), F32),
                        pltpu.VMEM((r, LANES), F32),
                        pltpu.VMEM((LANES, r), F32),
                        pltpu.VMEM((LANES, r), F32),
                        pltpu.VMEM((r, GP), F32)],
        compiler_params=_params(("arbitrary", "arbitrary"), 48),
        name="ssd_sample",
    )(*args)


def _mid_kernel(h_ref, oa_ref, ob_ref, x_ref, gt_ref, sc_ref, sh_ref, n2_ref,
                wg_ref, wpa_ref, wpb_ref, wo_ref, rw_ref, rb_ref,
                x1_ref, h2_ref, gates_ref, idx_ref, rank_ref, cnt_ref, carry_ref):
    tm, d = h2_ref.shape
    gab = _dot(h_ref[...], wg_ref[...])
    m = (_sigmoid(gab[:, :d]) * _dot(oa_ref[...], wpa_ref[...])
         + _sigmoid(gab[:, d:]) * _dot(ob_ref[...], wpb_ref[...]))
    x1 = x_ref[...] + gt_ref[...] * _dot(m.astype(BF16), wo_ref[...])
    x1_ref[...] = x1
    ms = jnp.mean(x1 * x1, axis=-1, keepdims=True)
    h2 = x1 * lax.rsqrt(ms + EPS) * n2_ref[...] * (1.0 + sc_ref[...]) + sh_ref[...]
    h2_ref[...] = h2

    logits = jnp.dot(h2, rw_ref[...], preferred_element_type=F32,
                     precision=lax.Precision.HIGHEST) + rb_ref[...]
    lane = lax.broadcasted_iota(jnp.int32, logits.shape, 1).astype(F32)
    vals, idxs = [], []
    for _ in range(TOP_K):
        mx = jnp.max(logits, axis=-1, keepdims=True)
        ix = jnp.min(jnp.where(logits == mx, lane, float(LANES)), axis=-1, keepdims=True)
        vals.append(mx)
        idxs.append(ix)
        logits = jnp.where(lane == ix, -jnp.inf, logits)
    es = [jnp.exp(v - vals[0]) for v in vals]
    tot = es[0] + es[1] + es[2] + es[3]
    gates = jnp.zeros(logits.shape, F32)
    idx = jnp.zeros(logits.shape, F32)
    for k in range(TOP_K):
        gates = jnp.where(lane == float(k), es[k] / tot, gates)
        idx = jnp.where(lane == float(k), idxs[k], idx)
    gates_ref[...] = gates
    idx_ref[...] = idx.astype(jnp.int32)

    @pl.when(pl.program_id(0) == 0)
    def _():
        carry_ref[...] = jnp.zeros(carry_ref.shape, F32)

    earlier = (lax.broadcasted_iota(jnp.int32, (tm, tm), 1)
               < lax.broadcasted_iota(jnp.int32, (tm, tm), 0)).astype(BF16)
    onehots = [(lane == idxs[k]).astype(F32) for k in range(TOP_K)]
    prefix = _dot(earlier, jnp.concatenate(onehots, axis=1).astype(BF16))
    base = carry_ref[...]
    rank = jnp.zeros(logits.shape, F32)
    for k in range(TOP_K):
        before = prefix[:, k * LANES:(k + 1) * LANES] + base
        rank = jnp.where(lane == float(k), jnp.sum(onehots[k] * before, axis=-1, keepdims=True), rank)
        base = base + jnp.sum(onehots[k], axis=0, keepdims=True)
    carry_ref[...] = base
    rank_ref[...] = rank.astype(jnp.int32)
    cnt_ref[...] = base.astype(jnp.int32)


def _mid(tok, layer, h, oa, ob, x, mod, n2, wg, wpa, wpb, wo, rw, rb):
    t, d = h.shape
    tm = tok.tm
    row = lambda w: pl.BlockSpec((tm, w), lambda i: (i, 0))
    full = lambda a: pl.BlockSpec(a.shape, lambda i: (0,) * a.ndim)
    n2 = n2.reshape(1, d)
    return pl.pallas_call(
        _mid_kernel,
        grid=(tok.n_tiles,),
        in_specs=[row(d), row(d), row(2 * d), row(d),
                  tok.mod_spec(layer, K_GT1, d), tok.mod_spec(layer, K_SC2, d), tok.mod_spec(layer, K_SH2, d),
                  full(n2), full(wg), full(wpa), full(wpb), full(wo), full(rw), full(rb)],
        out_specs=[row(d), row(d), row(LANES), row(LANES), row(LANES),
                   pl.BlockSpec((1, LANES), lambda i: (0, 0))],
        out_shape=[jax.ShapeDtypeStruct((t, d), F32),
                   jax.ShapeDtypeStruct((t, d), F32),
                   jax.ShapeDtypeStruct((t, LANES), F32),
                   jax.ShapeDtypeStruct((t, LANES), jnp.int32),
                   jax.ShapeDtypeStruct((t, LANES), jnp.int32),
                   jax.ShapeDtypeStruct((1, LANES), jnp.int32)],
        scratch_shapes=[pltpu.VMEM((1, LANES), F32)],
        compiler_params=_params(("arbitrary",), 56),
        name="mid",
    )(h, oa, ob, x, mod, mod, mod, n2, wg, wpa, wpb, wo, rw, rb)


def _dispatch_kernel(fill0_ref, filln_ref, n_used_ref, pos_ref, h_ref, xb_ref, zrow, sem, zsem):
    i = pl.program_id(0)
    tm = h_ref.shape[0]
    n_e = filln_ref.shape[0]
    blk = zrow.shape[0]
    n_blocks = xb_ref.shape[0] // blk

    def scatter(j, c):
        for u in range(DMA_UNROLL):
            t = j * DMA_UNROLL + u
            for k in range(TOP_K):
                _row_copy(h_ref, t, xb_ref, pos_ref[k, t], sem).start()
        return c
    lax.fori_loop(0, tm // DMA_UNROLL, scatter, 0)

    @pl.when(i == pl.num_programs(0) - 1)
    def _():
        zrow[...] = jnp.zeros(zrow.shape, F32)

        def fill_expert(e, c):
            def fill(r, c2):
                _row_copy(zrow, 0, xb_ref, fill0_ref[e] + r, zsem).start()
                return c2
            return lax.fori_loop(0, filln_ref[e], fill, c)
        lax.fori_loop(0, n_e, fill_expert, 0)

        def drain_expert(e, c):
            def drain(r, c2):
                _row_copy(zrow, 0, xb_ref, 0, zsem).wait()
                return c2
            return lax.fori_loop(0, filln_ref[e], drain, c)
        lax.fori_loop(0, n_e, drain_expert, 0)

        def block_copy(b):
            return pltpu.make_async_copy(zrow, xb_ref.at[pl.ds(pl.multiple_of(b * blk, blk), blk)], zsem)

        def fill_block(b, c):
            block_copy(b).start()
            return c
        lax.fori_loop(n_used_ref[0], n_blocks, fill_block, 0)

        def drain_block(b, c):
            block_copy(b).wait()
            return c
        lax.fori_loop(n_used_ref[0], n_blocks, drain_block, 0)

    for k in range(TOP_K):
        pltpu.make_async_copy(h_ref, xb_ref.at[pl.ds(0, tm)], sem).wait()


def _dispatch(h2, pos, fill0, filln, n_used, rows, *, tm, blk):
    t, d = h2.shape
    grid_spec = pltpu.PrefetchScalarGridSpec(
        num_scalar_prefetch=3,
        grid=(t // tm,),
        in_specs=[pl.BlockSpec((None, TOP_K, tm), lambda i, f0, fn, nu: (i, 0, 0), memory_space=pltpu.SMEM),
                  pl.BlockSpec((tm, d), lambda i, f0, fn, nu: (i, 0))],
        out_specs=pl.BlockSpec(memory_space=pl.ANY),
        scratch_shapes=[pltpu.VMEM((blk, d), F32), pltpu.SemaphoreType.DMA(()),
                        pltpu.SemaphoreType.DMA(())])
    return pl.pallas_call(
        _dispatch_kernel,
        grid_spec=grid_spec,
        out_shape=jax.ShapeDtypeStruct((rows, d), F32),
        compiler_params=_params(("arbitrary",), 32),
        name="dispatch",
    )(fill0, filln, n_used, pos, h2)


def _moe_kernel(blk_e_ref, blk_first_ref, n_used_ref, x_ref, wgu_ref, bgu_ref, wdn_ref, bdn_ref, o_ref,
                wgu_bf, wdn_bf):
    i = pl.program_id(0)
    d_ff = wdn_bf.shape[0]

    @pl.when(blk_first_ref[i] == 1)
    def _():
        wgu_bf[...] = wgu_ref[...].astype(BF16)
        wdn_bf[...] = wdn_ref[...].astype(BF16)

    @pl.when(i < n_used_ref[0])
    def _():
        gu = _dot(x_ref[...].astype(BF16), wgu_bf[...]) + bgu_ref[...]
        g = jnp.minimum(gu[:, :d_ff], SWIGLU_LIMIT)
        u = jnp.clip(gu[:, d_ff:], -SWIGLU_LIMIT, SWIGLU_LIMIT)
        act = g * _sigmoid(SWIGLU_ALPHA * g) * (u + 1.0)
        o_ref[...] = _dot(act.astype(BF16), wdn_bf[...]) + bdn_ref[...]

    @pl.when(i >= n_used_ref[0])
    def _():
        o_ref[...] = jnp.zeros(o_ref.shape, F32)


def _moe(xb, blk_e, blk_first, n_used, w_gu, b_gu, w_dn, b_dn, e0, *, tm):
    rows, d = xb.shape
    _, _, d_gu = w_gu.shape
    d_ff = w_dn.shape[1]
    expert = lambda i, be, bf, nu: (e0 + be[i], 0, 0)
    grid_spec = pltpu.PrefetchScalarGridSpec(
        num_scalar_prefetch=3,
        grid=(rows // tm,),
        in_specs=[pl.BlockSpec((tm, d), lambda i, be, bf, nu: (jnp.maximum(jnp.minimum(i, nu[0] - 1), 0), 0)),
                  pl.BlockSpec((None, d, d_gu), expert),
                  pl.BlockSpec((None, 1, d_gu), expert),
                  pl.BlockSpec((None, d_ff, d), expert),
                  pl.BlockSpec((None, 1, d), expert)],
        out_specs=pl.BlockSpec((tm, d), lambda i, be, bf, nu: (i, 0)),
        scratch_shapes=[pltpu.VMEM((d, d_gu), BF16), pltpu.VMEM((d_ff, d), BF16)])
    return pl.pallas_call(
        _moe_kernel,
        grid_spec=grid_spec,
        out_shape=jax.ShapeDtypeStruct((rows, d), F32),
        compiler_params=_params(("arbitrary",), 56),
        name="moe",
    )(blk_e, blk_first, n_used, xb, w_gu, b_gu, w_dn, b_dn)


def _route(idx, rank, counts, n_e, tm, tm_tok):
    t = idx.shape[0]
    padded = (counts + tm - 1) // tm * tm
    pad_end = jnp.cumsum(padded)
    pad_start = pad_end - padded
    experts = jnp.arange(n_e, dtype=jnp.int32)
    pos = jnp.sum(jnp.where(idx[..., None] == experts, pad_start, 0), axis=-1) + rank
    pos = pos.reshape(t // tm_tok, tm_tok, TOP_K).transpose(0, 2, 1).astype(jnp.int32)
    n_blocks = -(-t * TOP_K // tm) + n_e
    blk_start = jnp.arange(n_blocks, dtype=jnp.int32) * tm
    blk_e = jnp.minimum(jnp.sum((blk_start[:, None] >= pad_end[None, :]).astype(jnp.int32), axis=1),
                        n_e - 1).astype(jnp.int32)
    blk_first = jnp.concatenate([jnp.ones((1,), jnp.int32),
                                 (blk_e[1:] != blk_e[:-1]).astype(jnp.int32)])
    n_used = (pad_end[-1:] // tm).astype(jnp.int32)
    return (pos, blk_e, blk_first, n_used, (pad_start + counts).astype(jnp.int32),
            (padded - counts).astype(jnp.int32), n_blocks * tm)


def kernel(x_prompt, x_sample, c_prompt, c_sample, state_ssm, state_conv, norm1_g, norm2_g, final_g,
           w_ada, b_ada, w_in, a_ln_g, a_ln_b, a_ws, a_bs, b_conv_w, b_conv_b, b_dt_bias, b_a_log, b_d,
           b_norm_g, w_proj_a, w_proj_b, w_out, router_w, router_b, w_gu, b_gu, w_dn, b_dn):
    bp, seq, d = x_prompt.shape
    bs, lc_s, _ = x_sample.shape
    depth = w_ada.shape[0]
    n_e = router_w.shape[-1]
    heads = b_d.shape[-1]
    moe_tm = 256
    tok = _Tokens(bp, seq, bs, lc_s, tm=256)
    t_p, t_s, t = tok.t_p, tok.t_s, tok.t

    d_a = d
    c0 = 2 * d_a
    c1 = c0 + D_INNER + CONV_DIM + heads
    w_in_bf = w_in.astype(BF16)
    w_uv = w_in_bf[:, :, :c0]
    w_ssd = jnp.pad(w_in_bf[:, :, c0:c1], ((0, 0), (0, 0), (0, LANES - heads)))
    w_g = w_in_bf[:, :, c1:]
    w_pa, w_pb, w_o = w_proj_a.astype(BF16), w_proj_b.astype(BF16), w_out.astype(BF16)
    rw = jnp.pad(router_w, ((0, 0), (0, 0), (0, LANES - n_e)))
    rb = jnp.pad(router_b, ((0, 0), (0, LANES - n_e)), constant_values=-jnp.inf).reshape(depth, 1, LANES)
    causal = jnp.tril(jnp.ones((CHUNK, CHUNK), bool))
    ws_p = jnp.where(causal, a_ws, 0.0)
    blk = jnp.where(jnp.tril(jnp.ones((lc_s, lc_s), bool)), a_ws[:, :, :lc_s, :lc_s], 0.0)
    eye = jnp.eye(CHUNK // lc_s, dtype=F32)
    ws_s = jnp.einsum("ab,lgts->lgatbs", eye, blk).reshape(depth, G_A, CHUNK, CHUNK)
    wmix = jnp.stack([ws_p, ws_s], axis=1).astype(BF16)
    dg = d_a // G_A
    bias_p = jnp.repeat(jnp.swapaxes(a_bs, 1, 2), dg, axis=2)
    bias_s = jnp.tile(bias_p[:, :lc_s], (1, CHUNK // lc_s, 1))
    bias = jnp.stack([bias_p, bias_s], axis=1)
    dtb = jnp.pad(b_dt_bias, ((0, 0), (0, LANES - heads))).reshape(depth, 1, LANES)
    alog = jnp.pad(b_a_log, ((0, 0), (0, LANES - heads))).reshape(depth, 1, LANES)
    dsk = jnp.repeat(b_d, P_B, axis=1).reshape(depth, 1, D_INNER)
    cb = b_conv_b.reshape(depth, 1, CONV_DIM)
    ng = b_norm_g.reshape(depth, 1, D_INNER)
    w_gu_all = w_gu.reshape(depth * n_e, d, w_gu.shape[-1])
    b_gu_all = b_gu.reshape(depth * n_e, 1, b_gu.shape[-1])
    w_dn_all = w_dn.reshape(depth * n_e, w_dn.shape[-2], d)
    b_dn_all = b_dn.reshape(depth * n_e, 1, d)
    ssm_in = state_ssm.reshape(depth * bs, G_B, GP, N_B)
    conv_in = state_conv.reshape(depth * bs, CONV_W - 1, state_conv.shape[-1])

    mod_all = _ada(jnp.concatenate([c_prompt, c_sample], axis=0), w_ada, b_ada)
    mod_all = mod_all.reshape(depth, bp + bs, N_MOD, d).transpose(0, 2, 1, 3)
    mod = jnp.concatenate([jnp.repeat(mod_all[:, :, :bp], tok.tm, axis=2),
                           jnp.repeat(mod_all[:, :, bp:], lc_s, axis=2)], axis=2)

    x = jnp.concatenate([x_prompt.reshape(t_p, d), x_sample.reshape(t_s, d)], axis=0)

    h = _comb(tok, x, norm1_g[0], mod, layer_n=0)[0]
    ssm_p, conv_p, conv_s, v_s = [], [], [], []
    ssm_s = None
    y_final = None
    for l in range(depth):
        oa, v_rows = _gmlp(h, w_uv[l], a_ln_g[l], a_ln_b[l], wmix[l], bias[l], t_p)
        proj = _proj(h, w_ssd[l])
        ob, cp, sp = _ssd_prompt(proj, b_conv_w[l], cb[l], dtb[l], alog[l], dsk[l], ng[l], bp, seq)
        ob, cs, ssm_s = _ssd_sample(proj, t_p, lc_s, b_conv_w[l], cb[l], dtb[l], alog[l], dsk[l], ng[l],
                                    conv_in, ssm_in, l, ob, ssm_s)
        x1, h2, gates, idx, rank, cnt = _mid(tok, l, h, oa, ob, x, mod, norm2_g[l], w_g[l], w_pa[l],
                                             w_pb[l], w_o[l], rw[l], rb[l])
        pos, blk_e, blk_first, n_used, fill0, filln, rows = _route(
            idx[:, :TOP_K], rank[:, :TOP_K], cnt[0, :n_e], n_e, moe_tm, tok.tm)
        xb = _dispatch(h2, pos, fill0, filln, n_used, rows, tm=tok.tm, blk=moe_tm)
        yb = _moe(xb, blk_e, blk_first, n_used, w_gu_all, b_gu_all, w_dn_all, b_dn_all, l * n_e, tm=moe_tm)
        if l + 1 < depth:
            x, h = _comb(tok, x1, norm1_g[l + 1], mod, layer_n=l + 1, yb=yb, pos=pos, gates=gates, layer_gt=l)
        else:
            y_final = _comb(tok, x1, final_g, mod, yb=yb, pos=pos, gates=gates, layer_gt=l)[0]
        ssm_p.append(sp.reshape(bp, heads, P_B, N_B))
        conv_p.append(cp)
        conv_s.append(cs)
        v_s.append(v_rows.reshape(bs, lc_s, d))
    return (y_final[:t_p].reshape(bp, seq, d), y_final[t_p:].reshape(bs, lc_s, d),
            jnp.stack(ssm_p), jnp.stack(conv_p), ssm_s.reshape(depth, bs, heads, P_B, N_B),
            jnp.stack(conv_s), jnp.stack(v_s))
```

```python
import functools

import jax
import jax.numpy as jnp
from jax import lax
from jax.experimental import pallas as pl
from jax.experimental.pallas import tpu as pltpu

F32 = jnp.float32
BF16 = jnp.bfloat16

EPS = 1e-6
SUBLANES = 8
LANES = 128
CHUNK = 128
G_A = 8
G_B = 8
R_B = 4
P_B = 64
N_B = 128
CONV_W = 4
TOP_K = 4
SWIGLU_LIMIT = 7.0
SWIGLU_ALPHA = 1.702
MIB = 2 ** 20
N_MOD = 6
K_SH1, K_SC1, K_GT1, K_SH2, K_SC2, K_GT2 = range(N_MOD)
DMA_UNROLL = 8


def _params(sem, vmem_mib):
    return pltpu.CompilerParams(dimension_semantics=sem, vmem_limit_bytes=vmem_mib * MIB)


def _sigmoid(x):
    return 1.0 / (1.0 + jnp.exp(-x))


def _silu(x):
    return x * _sigmoid(x)


def _gelu(x):
    return 0.5 * x * (1.0 + lax.erf(x * (0.5 ** 0.5)))


def _softplus(x):
    return jnp.maximum(x, 0.0) + jnp.log(1.0 + jnp.exp(-jnp.abs(x)))


def _dot(a, b):
    return jnp.dot(a, b, preferred_element_type=F32)


def _dot_nt(a, b):
    return lax.dot_general(a, b, (((1,), (1,)), ((), ())), preferred_element_type=F32)


class _Tokens:
    def __init__(self, bp, seq, bs, lc_s, tm):
        self.tm = tm
        self.t_p, self.t_s = bp * seq, bs * lc_s
        self.t = self.t_p + self.t_s
        self.n_tiles = self.t // tm
        self.n_p = self.t_p // tm
        self.per_seq = seq // tm
        self.bp = bp

    def mod_block(self, i):
        return jnp.where(i < self.n_p, i // self.per_seq, self.bp + i - self.n_p)

    def mod_spec(self, layer, k, d):
        return pl.BlockSpec((None, None, self.tm, d), lambda i: (layer, k, self.mod_block(i), 0))


def _ada_kernel(c_ref, w_ref, b_ref, o_ref):
    c = c_ref[...]
    o_ref[...] = _dot(_silu(c).astype(BF16), w_ref[...].astype(BF16)) + b_ref[...]


def _ada(c_all, w_ada, b_ada):
    depth, d, n = w_ada.shape
    nc = c_all.shape[0]
    tn = 1536
    return pl.pallas_call(
        _ada_kernel,
        grid=(depth, n // tn),
        in_specs=[pl.BlockSpec((nc, d), lambda l, j: (0, 0)),
                  pl.BlockSpec((None, d, tn), lambda l, j: (l, 0, j)),
                  pl.BlockSpec((None, 1, tn), lambda l, j: (l, 0, j))],
        out_specs=pl.BlockSpec((None, nc, tn), lambda l, j: (l, 0, j)),
        out_shape=jax.ShapeDtypeStruct((depth, nc, n), F32),
        compiler_params=_params(("arbitrary", "arbitrary"), 40),
        name="ada",
    )(c_all, w_ada, b_ada.reshape(depth, 1, n))


def _row_copy(src_ref, src_row, dst_ref, dst_row, sem):
    return pltpu.make_async_copy(src_ref.at[pl.ds(src_row, 1)], dst_ref.at[pl.ds(dst_row, 1)], sem)


def _comb_kernel(*refs, combine, final):
    it = iter(refs)
    if combine:
        pos_ref, posn_ref = next(it), next(it)
    x_ref = next(it)
    if combine:
        yb_ref, g_ref, gt_ref = next(it), next(it), next(it)
    n_ref = next(it)
    if not final:
        sc_ref, sh_ref = next(it), next(it)
    if combine and not final:
        xo_ref = next(it)
    o_ref = next(it)
    if combine:
        ybuf, sem = next(it), next(it)

    x = x_ref[...]
    if combine:
        i = pl.program_id(0)
        n = pl.num_programs(0)
        tm = ybuf.shape[2]
        slot = i % 2

        def gather(p_ref, s):
            def body(j, c):
                for u in range(DMA_UNROLL):
                    t = j * DMA_UNROLL + u
                    for k in range(TOP_K):
                        _row_copy(yb_ref, p_ref[k, t], ybuf.at[s, k], t, sem.at[s]).start()
                return c
            lax.fori_loop(0, tm // DMA_UNROLL, body, 0)

        @pl.when(i == 0)
        def _():
            gather(pos_ref, 0)

        @pl.when(i + 1 < n)
        def _():
            gather(posn_ref, 1 - slot)

        for k in range(TOP_K):
            pltpu.make_async_copy(yb_ref.at[pl.ds(0, tm)], ybuf.at[slot, k], sem.at[slot]).wait()

        g = g_ref[...]
        acc = g[:, 0:1] * ybuf[slot, 0]
        for k in range(1, TOP_K):
            acc = acc + g[:, k:k + 1] * ybuf[slot, k]
        x = x + gt_ref[...] * acc
        if not final:
            xo_ref[...] = x
    ms = jnp.mean(x * x, axis=-1, keepdims=True)
    xn = x * lax.rsqrt(ms + EPS)
    if final:
        o_ref[...] = xn * n_ref[...]
    else:
        o_ref[...] = (xn * n_ref[...] * (1.0 + sc_ref[...]) + sh_ref[...]).astype(BF16)


def _comb(tok, x, norm_g, mod=None, layer_n=None, yb=None, pos=None, gates=None, layer_gt=None):
    t, d = x.shape
    tm = tok.tm
    combine = yb is not None
    final = layer_n is None
    row = lambda w: pl.BlockSpec((tm, w), lambda i: (i, 0))

    args, specs, scratch = [], [], []
    if combine:
        args += [pos, pos]
        specs += [pl.BlockSpec((None, TOP_K, tm), lambda i: (i, 0, 0), memory_space=pltpu.SMEM),
                  pl.BlockSpec((None, TOP_K, tm), lambda i: (jnp.minimum(i + 1, tok.n_tiles - 1), 0, 0),
                               memory_space=pltpu.SMEM)]
        scratch = [pltpu.VMEM((2, TOP_K, tm, d), F32), pltpu.SemaphoreType.DMA((2,))]
    args.append(x)
    specs.append(row(d))
    if combine:
        args += [yb, gates, mod]
        specs += [pl.BlockSpec(memory_space=pl.ANY), row(LANES), tok.mod_spec(layer_gt, K_GT2, d)]
    args.append(norm_g.reshape(1, d))
    specs.append(pl.BlockSpec((1, d), lambda i: (0, 0)))
    if not final:
        args += [mod, mod]
        specs += [tok.mod_spec(layer_n, K_SC1, d), tok.mod_spec(layer_n, K_SH1, d)]
    out_shape, out_specs = [], []
    if combine and not final:
        out_shape.append(jax.ShapeDtypeStruct((t, d), F32))
        out_specs.append(row(d))
    out_shape.append(jax.ShapeDtypeStruct((t, d), F32 if final else BF16))
    out_specs.append(row(d))
    return pl.pallas_call(
        functools.partial(_comb_kernel, combine=combine, final=final),
        grid=(tok.n_tiles,),
        in_specs=specs, out_specs=out_specs, out_shape=out_shape,
        scratch_shapes=scratch,
        compiler_params=_params(("arbitrary",), 48),
        name="comb",
    )(*args)


def _gmlp_kernel(h_ref, w_ref, lng_ref, lnb_ref, wmix_ref, bias_ref, o_ref, v_ref, *, n_prompt_tiles):
    i = pl.program_id(0)
    tm, d = o_ref.shape
    uv = _dot(h_ref[...], w_ref[...])
    u = _gelu(uv[:, :d])
    v = _gelu(uv[:, d:])
    mu = jnp.mean(v, axis=-1, keepdims=True)
    vc = v - mu
    var = jnp.mean(vc * vc, axis=-1, keepdims=True)
    vn = vc * lax.rsqrt(var + EPS) * lng_ref[...] + lnb_ref[...]

    @pl.when(i >= n_prompt_tiles)
    def _():
        v_ref[...] = vn

    vb = vn.astype(BF16)
    dg = d // G_A
    for c in range(tm // CHUNK):
        rows = slice(c * CHUNK, (c + 1) * CHUNK)
        for g in range(G_A):
            cols = slice(g * dg, (g + 1) * dg)
            s = _dot(wmix_ref[g], vb[rows, cols]) + bias_ref[:, cols]
            o_ref[rows, cols] = (u[rows, cols] * s).astype(BF16)


def _gmlp(h, w_uv, ln_g, ln_b, wmix, bias, t_prompt, *, tm=512):
    t, d = h.shape
    n_p = t_prompt // tm
    t_s = t - t_prompt
    sel = lambda i: jnp.where(i >= n_p, 1, 0)
    return pl.pallas_call(
        functools.partial(_gmlp_kernel, n_prompt_tiles=n_p),
        grid=(t // tm,),
        in_specs=[pl.BlockSpec((tm, d), lambda i: (i, 0)),
                  pl.BlockSpec((d, 2 * d), lambda i: (0, 0)),
                  pl.BlockSpec((1, d), lambda i: (0, 0)),
                  pl.BlockSpec((1, d), lambda i: (0, 0)),
                  pl.BlockSpec((None, G_A, CHUNK, CHUNK), lambda i: (sel(i), 0, 0, 0)),
                  pl.BlockSpec((None, CHUNK, d), lambda i: (sel(i), 0, 0))],
        out_specs=[pl.BlockSpec((tm, d), lambda i: (i, 0)),
                   pl.BlockSpec((tm, d), lambda i: (jnp.maximum(i - n_p, 0), 0))],
        out_shape=[jax.ShapeDtypeStruct((t, d), BF16),
                   jax.ShapeDtypeStruct((t_s, d), F32)],
        compiler_params=_params(("arbitrary",), 48),
        name="gmlp",
    )(h, w_uv, ln_g.reshape(1, d), ln_b.reshape(1, d), wmix, bias)


def _proj_kernel(x_ref, w_ref, o_ref):
    o_ref[...] = _dot(x_ref[...], w_ref[...])


def _proj(x, w, *, tm=256):
    t, d = x.shape
    n = w.shape[1]
    return pl.pallas_call(
        _proj_kernel,
        grid=(t // tm,),
        in_specs=[pl.BlockSpec((tm, d), lambda i: (i, 0)),
                  pl.BlockSpec((d, n), lambda i: (0, 0))],
        out_specs=pl.BlockSpec((tm, n), lambda i: (i, 0)),
        out_shape=jax.ShapeDtypeStruct((t, n), F32),
        compiler_params=_params(("arbitrary",), 56),
        name="inproj",
    )(x, w)


D_INNER = G_B * R_B * P_B
GN = G_B * N_B
CONV_DIM = D_INNER + 2 * GN
GP = R_B * P_B
HIST = SUBLANES


def _ssd_stage_a(proj_ref, cw_ref, cb_ref, dtb_ref, alog_ref, cs_ref, xact_ref, acum_ref, acumT_ref,
                 dtpT_ref, *, nb, lc):
    r = nb * lc
    cs_ref[:, HIST:HIST + lc, :] = proj_ref[:, D_INNER:D_INNER + CONV_DIM].reshape(nb, lc, CONV_DIM)
    for j in range(CONV_DIM // LANES):
        cols = slice(j * LANES, (j + 1) * LANES)
        full = cs_ref[:, :, cols]
        conv = cb_ref[:, cols] + full[:, HIST:, :] * cw_ref[CONV_W - 1:CONV_W, cols]
        for s in range(1, CONV_W):
            back = pltpu.roll(full, s, axis=1)[:, HIST:, :]
            conv = conv + back * cw_ref[CONV_W - 1 - s:CONV_W - s, cols]
        xact_ref[:, cols] = _silu(conv).reshape(r, LANES)
    tail = cs_ref[:, HIST + lc - 3:HIST + lc, :]

    dt = proj_ref[:, D_INNER + CONV_DIM:D_INNER + CONV_DIM + LANES]
    dtp = _softplus(dt + dtb_ref[...])
    acum = dtp * (-jnp.exp(alog_ref[...]))
    local = lax.broadcasted_iota(jnp.int32, (r, LANES), 0) % lc
    sh = 1
    while sh < lc:
        acum = acum + jnp.where(local >= sh, pltpu.roll(acum, sh, axis=0), 0.0)
        sh *= 2
    acum_ref[...] = acum
    acumT_ref[...] = acum.T
    dtpT_ref[...] = dtp.T
    return tail


def _ssd_heads(g, mask, cb, yoff, xact_ref, acum_ref, acumT_ref, dtpT_ref, dsk_ref):
    r = xact_ref.shape[0]
    first = lax.broadcasted_iota(jnp.int32, (r, LANES), 1) < P_B
    ys = []
    for pair in range(R_B // 2):
        c0 = g * GP + pair * LANES
        xs = xact_ref[:, c0:c0 + LANES]
        xs_bf = xs.astype(BF16)
        colbs, ds = [], []
        for rr in range(2):
            hh = g * R_B + pair * 2 + rr
            colb = jnp.broadcast_to(acum_ref[:, hh:hh + 1], (r, r))
            seg = colb - acumT_ref[hh:hh + 1, :]
            m = cb * jnp.exp(jnp.where(mask, seg, -jnp.inf)) * dtpT_ref[hh:hh + 1, :]
            colbs.append(colb)
            ds.append(_dot(m.astype(BF16), xs_bf))
        y = jnp.where(first, ds[0], ds[1])
        y = y + jnp.exp(jnp.where(first, colbs[0], colbs[1])) * yoff[:, pair * LANES:(pair + 1) * LANES]
        y = y + dsk_ref[:, c0:c0 + LANES] * xs
        ys.append(y)
    return jnp.concatenate(ys, axis=1)


def _ssd_gate_norm(y, z, ng):
    yz = y * _silu(z)
    ms = jnp.mean(yz * yz, axis=-1, keepdims=True)
    return (yz * lax.rsqrt(ms + EPS) * ng).astype(BF16)


def _head_rows(vals):
    return jnp.concatenate([jnp.broadcast_to(v, (P_B, N_B)) for v in vals], axis=0)


def _ssd_prompt_kernel(h_ref, w_ref, cw_ref, cb_ref, dtb_ref, alog_ref, dsk_ref, ng_ref,
                       ob_ref, cout_ref, hout_ref,
                       proj_ref, cs_ref, hs_ref, xact_ref, acum_ref, acumT_ref, dtpT_ref):
    c = pl.program_id(1)
    lc = CHUNK

    @pl.when(c == 0)
    def _():
        cs_ref[:, 0:HIST, :] = jnp.zeros((1, HIST, CONV_DIM), F32)
        hs_ref[...] = jnp.zeros(hs_ref.shape, F32)

    proj_ref[...] = _dot(h_ref[...], w_ref[...])

    tail = _ssd_stage_a(proj_ref, cw_ref, cb_ref, dtb_ref, alog_ref, cs_ref, xact_ref, acum_ref,
                        acumT_ref, dtpT_ref, nb=1, lc=lc)
    cs_ref[:, HIST - 3:HIST, :] = tail

    @pl.when(c == pl.num_programs(1) - 1)
    def _():
        cout_ref[...] = tail[0]

    li = lax.broadcasted_iota(jnp.int32, (lc, lc), 0)
    si = lax.broadcasted_iota(jnp.int32, (lc, lc), 1)
    mask = li >= si
    acum_t = acumT_ref[...]
    wt = dtpT_ref[...] * jnp.exp(acum_t[:, lc - 1:lc] - acum_t)
    for g in range(G_B):
        bg = xact_ref[:, D_INNER + g * N_B:D_INNER + (g + 1) * N_B].astype(BF16)
        cg = xact_ref[:, D_INNER + GN + g * N_B:D_INNER + GN + (g + 1) * N_B].astype(BF16)
        cb = _dot_nt(cg, bg)
        hg = hs_ref[g]
        yoff = _dot_nt(cg, hg.astype(BF16))
        y = _ssd_heads(g, mask, cb, yoff, xact_ref, acum_ref, acumT_ref, dtpT_ref, dsk_ref)
        cols = slice(g * GP, (g + 1) * GP)
        ob_ref[:, cols] = _ssd_gate_norm(y, proj_ref[:, cols], ng_ref[:, cols])
        xs_t = xact_ref[:, cols].T
        xw = jnp.concatenate(
            [xs_t[rr * P_B:(rr + 1) * P_B, :] * wt[g * R_B + rr:g * R_B + rr + 1, :] for rr in range(R_B)],
            axis=0)
        inc = _dot(xw.astype(BF16), bg)
        dec = _head_rows([jnp.exp(acum_t[g * R_B + rr:g * R_B + rr + 1, lc - 1:lc]) for rr in range(R_B)])
        hs_ref[g] = dec * hg + inc

    @pl.when(c == pl.num_programs(1) - 1)
    def _():
        hout_ref[...] = hs_ref[...]


def _ssd_prompt(h, w, cw, cb, dtb, alog, dsk, ng, batch, seq):
    t, d = h.shape
    n = w.shape[1]
    nc = seq // CHUNK
    vec = lambda w: pl.BlockSpec((1, w), lambda b, c: (0, 0))
    return pl.pallas_call(
        _ssd_prompt_kernel,
        grid=(batch, nc),
        in_specs=[pl.BlockSpec((CHUNK, d), lambda b, c: (b * nc + c, 0)),
                  pl.BlockSpec((d, n), lambda b, c: (0, 0)),
                  pl.BlockSpec((CONV_W, CONV_DIM), lambda b, c: (0, 0)),
                  vec(CONV_DIM), vec(LANES), vec(LANES), vec(D_INNER), vec(D_INNER)],
        out_specs=[pl.BlockSpec((CHUNK, D_INNER), lambda b, c: (b * nc + c, 0)),
                   pl.BlockSpec((None, CONV_W - 1, CONV_DIM), lambda b, c: (b, 0, 0)),
                   pl.BlockSpec((None, G_B, GP, N_B), lambda b, c: (b, 0, 0, 0))],
        out_shape=[jax.ShapeDtypeStruct((t, D_INNER), BF16),
                   jax.ShapeDtypeStruct((batch, CONV_W - 1, CONV_DIM), F32),
                   jax.ShapeDtypeStruct((batch, G_B, GP, N_B), F32)],
        scratch_shapes=[pltpu.VMEM((CHUNK, n), F32),
                        pltpu.VMEM((1, HIST + CHUNK, CONV_DIM), F32),
                        pltpu.VMEM((G_B, GP, N_B), F32),
                        pltpu.VMEM((CHUNK, CONV_DIM), F32),
                        pltpu.VMEM((CHUNK, LANES), F32),
                        pltpu.VMEM((LANES, CHUNK), F32),
                        pltpu.VMEM((LANES, CHUNK), F32)],
        compiler_params=_params(("arbitrary", "arbitrary"), 56),
        name="ssd_prompt",
    )(h, w, cw, cb, dtb, alog, dsk, ng)


def _ssd_sample_kernel(*refs, nb, lc, chained):
    (proj_ref, cw_ref, cb_ref, dtb_ref, alog_ref, dsk_ref, ng_ref, cin_ref, hin_ref) = refs[:9]
    n_alias = 2 if chained else 1
    (ob_ref, cout_ref, hout_ref,
     cs_ref, xact_ref, acum_ref, acumT_ref, dtpT_ref, yoff_ref) = refs[9 + n_alias:]
    g = pl.program_id(1)
    r = nb * lc

    @pl.when(g == 0)
    def _():
        cs_ref[:, HIST - 3:HIST, :] = cin_ref[...]
        tail = _ssd_stage_a(proj_ref, cw_ref, cb_ref, dtb_ref, alog_ref, cs_ref, xact_ref, acum_ref,
                            acumT_ref, dtpT_ref, nb=nb, lc=lc)
        cout_ref[...] = tail

    li = lax.broadcasted_iota(jnp.int32, (r, r), 0)
    si = lax.broadcasted_iota(jnp.int32, (r, r), 1)
    mask = (li >= si) & ((li // lc) == (si // lc))
    lane_seq = lax.broadcasted_iota(jnp.int32, (GP, r), 1) // lc
    acum_t = acumT_ref[...]
    dtp_t = dtpT_ref[...]

    def branch(gg):
        bg = xact_ref[:, D_INNER + gg * N_B:D_INNER + (gg + 1) * N_B].astype(BF16)
        cg = xact_ref[:, D_INNER + GN + gg * N_B:D_INNER + GN + (gg + 1) * N_B].astype(BF16)
        cb = _dot_nt(cg, bg)
        cols = slice(gg * GP, (gg + 1) * GP)
        xs_t = xact_ref[:, cols].T
        win = 2 * SUBLANES
        for b in range(nb):
            h0 = hin_ref[b, 0]
            w0 = (b * lc // win) * win
            yo = _dot_nt(cg[w0:w0 + win], h0.astype(BF16))
            yoff_ref[b * lc:(b + 1) * lc, :] = yo[b * lc - w0:b * lc - w0 + lc]
            last = b * lc + lc - 1
            xw = jnp.concatenate(
                [xs_t[rr * P_B:(rr + 1) * P_B, :]
                 * (dtp_t[gg * R_B + rr:gg * R_B + rr + 1, :]
                    * jnp.exp(acum_t[gg * R_B + rr:gg * R_B + rr + 1, last:last + 1]
                              - acum_t[gg * R_B + rr:gg * R_B + rr + 1, :]))
                 for rr in range(R_B)], axis=0)
            xw = jnp.where(lane_seq == b, xw, 0.0)
            inc = _dot(xw.astype(BF16), bg)
            dec = _head_rows([jnp.exp(acum_t[gg * R_B + rr:gg * R_B + rr + 1, last:last + 1])
                              for rr in range(R_B)])
            hout_ref[b, 0] = dec * h0 + inc
        y = _ssd_heads(gg, mask, cb, yoff_ref[...], xact_ref, acum_ref, acumT_ref, dtpT_ref, dsk_ref)
        ob_ref[...] = _ssd_gate_norm(y, proj_ref[:, cols], ng_ref[:, cols])

    for gg in range(G_B):
        pl.when(g == gg)(functools.partial(branch, gg))


def _ssd_sample(proj, row0, lc, cw, cb, dtb, alog, dsk, ng, conv_state, ssm_state, layer, ob, ssm_out,
                *, nb=16):
    n = proj.shape[1]
    batch = proj.shape[0] // lc
    r = nb * lc
    blk0 = row0 // r
    sb0 = layer * (batch // nb)
    chained = ssm_out is not None
    vec = lambda w: pl.BlockSpec((1, w), lambda i, g: (0, 0))
    state_spec = pl.BlockSpec((nb, 1, GP, N_B), lambda i, g: (sb0 + i, g, 0, 0))
    hbm = pl.BlockSpec(memory_space=pl.ANY)
    args = [proj, cw, cb, dtb, alog, dsk, ng, conv_state, ssm_state, ob] + ([ssm_out] if chained else [])
    aliases = {9: 0, 10: 2} if chained else {9: 0}
    return pl.pallas_call(
        functools.partial(_ssd_sample_kernel, nb=nb, lc=lc, chained=chained),
        grid=(batch // nb, G_B),
        in_specs=[pl.BlockSpec((r, n), lambda i, g: (i, 0)),
                  pl.BlockSpec((CONV_W, CONV_DIM), lambda i, g: (0, 0)),
                  vec(CONV_DIM), vec(LANES), vec(LANES), vec(D_INNER), vec(D_INNER),
                  pl.BlockSpec((nb, CONV_W - 1, CONV_DIM), lambda i, g: (sb0 + i, 0, 0)),
                  state_spec, hbm] + ([hbm] if chained else []),
        out_specs=[pl.BlockSpec((r, GP), lambda i, g: (blk0 + i, g)),
                   pl.BlockSpec((nb, CONV_W - 1, CONV_DIM), lambda i, g: (i, 0, 0)),
                   state_spec],
        out_shape=[jax.ShapeDtypeStruct(ob.shape, BF16),
                   jax.ShapeDtypeStruct((batch, CONV_W - 1, CONV_DIM), F32),
                   jax.ShapeDtypeStruct(ssm_state.shape, F32)],
        input_output_aliases=aliases,
        scratch_shapes=[pltpu.VMEM((nb, HIST + lc, CONV_DIM), F32),
                        pltpu.VMEM((r, CONV_DIM), F32),
                        pltpu.VMEM((r, LANES), F32),
                        pltpu.VMEM((LANES, r), F32),
                        pltpu.VMEM((LANES, r), F32),
                        pltpu.VMEM((r, GP), F32)],
        compiler_params=_params(("arbitrary", "arbitrary"), 48),
        name="ssd_sample",
    )(*args)


def _mid_kernel(h_ref, oa_ref, ob_ref, x_ref, gt_ref, sc_ref, sh_ref, n2_ref,
                wg_ref, wpa_ref, wpb_ref, wo_ref, rw_ref, rb_ref,
                x1_ref, h2_ref, gates_ref, idx_ref, rank_ref, cnt_ref, carry_ref):
    tm, d = h2_ref.shape
    gab = _dot(h_ref[...], wg_ref[...])
    m = (_sigmoid(gab[:, :d]) * _dot(oa_ref[...], wpa_ref[...])
         + _sigmoid(gab[:, d:]) * _dot(ob_ref[...], wpb_ref[...]))
    x1 = x_ref[...] + gt_ref[...] * _dot(m.astype(BF16), wo_ref[...])
    x1_ref[...] = x1
    ms = jnp.mean(x1 * x1, axis=-1, keepdims=True)
    h2 = x1 * lax.rsqrt(ms + EPS) * n2_ref[...] * (1.0 + sc_ref[...]) + sh_ref[...]
    h2_ref[...] = h2

    logits = _dot(h2.astype(BF16), rw_ref[...]) + rb_ref[...]
    lane = lax.broadcasted_iota(jnp.int32, logits.shape, 1).astype(F32)
    vals, idxs = [], []
    for _ in range(TOP_K):
        mx = jnp.max(logits, axis=-1, keepdims=True)
        ix = jnp.min(jnp.where(logits == mx, lane, float(LANES)), axis=-1, keepdims=True)
        vals.append(mx)
        idxs.append(ix)
        logits = jnp.where(lane == ix, -jnp.inf, logits)
    es = [jnp.exp(v - vals[0]) for v in vals]
    tot = es[0] + es[1] + es[2] + es[3]
    gates = jnp.zeros(logits.shape, F32)
    idx = jnp.zeros(logits.shape, F32)
    for k in range(TOP_K):
        gates = jnp.where(lane == float(k), es[k] / tot, gates)
        idx = jnp.where(lane == float(k), idxs[k], idx)
    gates_ref[...] = gates
    idx_ref[...] = idx.astype(jnp.int32)

    @pl.when(pl.program_id(0) == 0)
    def _():
        carry_ref[...] = jnp.zeros(carry_ref.shape, F32)

    earlier = (lax.broadcasted_iota(jnp.int32, (tm, tm), 1)
               < lax.broadcasted_iota(jnp.int32, (tm, tm), 0)).astype(BF16)
    onehots = [(lane == idxs[k]).astype(F32) for k in range(TOP_K)]
    prefix = _dot(earlier, jnp.concatenate(onehots, axis=1).astype(BF16))
    base = carry_ref[...]
    rank = jnp.zeros(logits.shape, F32)
    for k in range(TOP_K):
        before = prefix[:, k * LANES:(k + 1) * LANES] + base
        rank = jnp.where(lane == float(k), jnp.sum(onehots[k] * before, axis=-1, keepdims=True), rank)
        base = base + jnp.sum(onehots[k], axis=0, keepdims=True)
    carry_ref[...] = base
    rank_ref[...] = rank.astype(jnp.int32)
    cnt_ref[...] = base.astype(jnp.int32)


def _mid(tok, layer, h, oa, ob, x, mod, n2, wg, wpa, wpb, wo, rw, rb):
    t, d = h.shape
    tm = tok.tm
    row = lambda w: pl.BlockSpec((tm, w), lambda i: (i, 0))
    full = lambda a: pl.BlockSpec(a.shape, lambda i: (0,) * a.ndim)
    n2 = n2.reshape(1, d)
    return pl.pallas_call(
        _mid_kernel,
        grid=(tok.n_tiles,),
        in_specs=[row(d), row(d), row(2 * d), row(d),
                  tok.mod_spec(layer, K_GT1, d), tok.mod_spec(layer, K_SC2, d), tok.mod_spec(layer, K_SH2, d),
                  full(n2), full(wg), full(wpa), full(wpb), full(wo), full(rw), full(rb)],
        out_specs=[row(d), row(d), row(LANES), row(LANES), row(LANES),
                   pl.BlockSpec((1, LANES), lambda i: (0, 0))],
        out_shape=[jax.ShapeDtypeStruct((t, d), F32),
                   jax.ShapeDtypeStruct((t, d), F32),
                   jax.ShapeDtypeStruct((t, LANES), F32),
                   jax.ShapeDtypeStruct((t, LANES), jnp.int32),
                   jax.ShapeDtypeStruct((t, LANES), jnp.int32),
                   jax.ShapeDtypeStruct((1, LANES), jnp.int32)],
        scratch_shapes=[pltpu.VMEM((1, LANES), F32)],
        compiler_params=_params(("arbitrary",), 56),
        name="mid",
    )(h, oa, ob, x, mod, mod, mod, n2, wg, wpa, wpb, wo, rw, rb)


def _dispatch_kernel(fill0_ref, filln_ref, n_used_ref, pos_ref, h_ref, xb_ref, zrow, sem, zsem):
    i = pl.program_id(0)
    tm = h_ref.shape[0]
    n_e = filln_ref.shape[0]
    blk = zrow.shape[0]
    n_blocks = xb_ref.shape[0] // blk

    def scatter(j, c):
        for u in range(DMA_UNROLL):
            t = j * DMA_UNROLL + u
            for k in range(TOP_K):
                _row_copy(h_ref, t, xb_ref, pos_ref[k, t], sem).start()
        return c
    lax.fori_loop(0, tm // DMA_UNROLL, scatter, 0)

    @pl.when(i == pl.num_programs(0) - 1)
    def _():
        zrow[...] = jnp.zeros(zrow.shape, F32)

        def fill_expert(e, c):
            def fill(r, c2):
                _row_copy(zrow, 0, xb_ref, fill0_ref[e] + r, zsem).start()
                return c2
            return lax.fori_loop(0, filln_ref[e], fill, c)
        lax.fori_loop(0, n_e, fill_expert, 0)

        def drain_expert(e, c):
            def drain(r, c2):
                _row_copy(zrow, 0, xb_ref, 0, zsem).wait()
                return c2
            return lax.fori_loop(0, filln_ref[e], drain, c)
        lax.fori_loop(0, n_e, drain_expert, 0)

        def block_copy(b):
            return pltpu.make_async_copy(zrow, xb_ref.at[pl.ds(pl.multiple_of(b * blk, blk), blk)], zsem)

        def fill_block(b, c):
            block_copy(b).start()
            return c
        lax.fori_loop(n_used_ref[0], n_blocks, fill_block, 0)

        def drain_block(b, c):
            block_copy(b).wait()
            return c
        lax.fori_loop(n_used_ref[0], n_blocks, drain_block, 0)

    for k in range(TOP_K):
        pltpu.make_async_copy(h_ref, xb_ref.at[pl.ds(0, tm)], sem).wait()


def _dispatch(h2, pos, fill0, filln, n_used, rows, *, tm, blk):
    t, d = h2.shape
    grid_spec = pltpu.PrefetchScalarGridSpec(
        num_scalar_prefetch=3,
        grid=(t // tm,),
        in_specs=[pl.BlockSpec((None, TOP_K, tm), lambda i, f0, fn, nu: (i, 0, 0), memory_space=pltpu.SMEM),
                  pl.BlockSpec((tm, d), lambda i, f0, fn, nu: (i, 0))],
        out_specs=pl.BlockSpec(memory_space=pl.ANY),
        scratch_shapes=[pltpu.VMEM((blk, d), F32), pltpu.SemaphoreType.DMA(()),
                        pltpu.SemaphoreType.DMA(())])
    return pl.pallas_call(
        _dispatch_kernel,
        grid_spec=grid_spec,
        out_shape=jax.ShapeDtypeStruct((rows, d), F32),
        compiler_params=_params(("arbitrary",), 32),
        name="dispatch",
    )(fill0, filln, n_used, pos, h2)


def _moe_kernel(blk_e_ref, blk_first_ref, n_used_ref, x_ref, wgu_ref, bgu_ref, wdn_ref, bdn_ref, o_ref,
                wgu_bf, wdn_bf):
    i = pl.program_id(0)
    d_ff = wdn_bf.shape[0]

    @pl.when(blk_first_ref[i] == 1)
    def _():
        wgu_bf[...] = wgu_ref[...].astype(BF16)
        wdn_bf[...] = wdn_ref[...].astype(BF16)

    @pl.when(i < n_used_ref[0])
    def _():
        gu = _dot(x_ref[...].astype(BF16), wgu_bf[...]) + bgu_ref[...]
        g = jnp.minimum(gu[:, :d_ff], SWIGLU_LIMIT)
        u = jnp.clip(gu[:, d_ff:], -SWIGLU_LIMIT, SWIGLU_LIMIT)
        act = g * _sigmoid(SWIGLU_ALPHA * g) * (u + 1.0)
        o_ref[...] = _dot(act.astype(BF16), wdn_bf[...]) + bdn_ref[...]

    @pl.when(i >= n_used_ref[0])
    def _():
        o_ref[...] = jnp.zeros(o_ref.shape, F32)


def _moe(xb, blk_e, blk_first, n_used, w_gu, b_gu, w_dn, b_dn, e0, *, tm):
    rows, d = xb.shape
    _, _, d_gu = w_gu.shape
    d_ff = w_dn.shape[1]
    expert = lambda i, be, bf, nu: (e0 + be[i], 0, 0)
    grid_spec = pltpu.PrefetchScalarGridSpec(
        num_scalar_prefetch=3,
        grid=(rows // tm,),
        in_specs=[pl.BlockSpec((tm, d), lambda i, be, bf, nu: (jnp.maximum(jnp.minimum(i, nu[0] - 1), 0), 0)),
                  pl.BlockSpec((None, d, d_gu), expert),
                  pl.BlockSpec((None, 1, d_gu), expert),
                  pl.BlockSpec((None, d_ff, d), expert),
                  pl.BlockSpec((None, 1, d), expert)],
        out_specs=pl.BlockSpec((tm, d), lambda i, be, bf, nu: (i, 0)),
        scratch_shapes=[pltpu.VMEM((d, d_gu), BF16), pltpu.VMEM((d_ff, d), BF16)])
    return pl.pallas_call(
        _moe_kernel,
        grid_spec=grid_spec,
        out_shape=jax.ShapeDtypeStruct((rows, d), F32),
        compiler_params=_params(("arbitrary",), 56),
        name="moe",
    )(blk_e, blk_first, n_used, xb, w_gu, b_gu, w_dn, b_dn)


def _route(idx, rank, counts, n_e, tm, tm_tok):
    t = idx.shape[0]
    padded = (counts + tm - 1) // tm * tm
    pad_end = jnp.cumsum(padded)
    pad_start = pad_end - padded
    experts = jnp.arange(n_e, dtype=jnp.int32)
    pos = jnp.sum(jnp.where(idx[..., None] == experts, pad_start, 0), axis=-1) + rank
    pos = pos.reshape(t // tm_tok, tm_tok, TOP_K).transpose(0, 2, 1).astype(jnp.int32)
    n_blocks = -(-t * TOP_K // tm) + n_e
    blk_start = jnp.arange(n_blocks, dtype=jnp.int32) * tm
    blk_e = jnp.minimum(jnp.sum((blk_start[:, None] >= pad_end[None, :]).astype(jnp.int32), axis=1),
                        n_e - 1).astype(jnp.int32)
    blk_first = jnp.concatenate([jnp.ones((1,), jnp.int32),
                                 (blk_e[1:] != blk_e[:-1]).astype(jnp.int32)])
    n_used = (pad_end[-1:] // tm).astype(jnp.int32)
    return (pos, blk_e, blk_first, n_used, (pad_start + counts).astype(jnp.int32),
            (padded - counts).astype(jnp.int32), n_blocks * tm)


def kernel(x_prompt, x_sample, c_prompt, c_sample, state_ssm, state_conv, norm1_g, norm2_g, final_g,
           w_ada, b_ada, w_in, a_ln_g, a_ln_b, a_ws, a_bs, b_conv_w, b_conv_b, b_dt_bias, b_a_log, b_d,
           b_norm_g, w_proj_a, w_proj_b, w_out, router_w, router_b, w_gu, b_gu, w_dn, b_dn):
    bp, seq, d = x_prompt.shape
    bs, lc_s, _ = x_sample.shape
    depth = w_ada.shape[0]
    n_e = router_w.shape[-1]
    heads = b_d.shape[-1]
    moe_tm = 256
    tok = _Tokens(bp, seq, bs, lc_s, tm=256)
    t_p, t_s, t = tok.t_p, tok.t_s, tok.t

    d_a = d
    c0 = 2 * d_a
    c1 = c0 + D_INNER + CONV_DIM + heads
    w_in_bf = w_in.astype(BF16)
    w_uv = w_in_bf[:, :, :c0]
    w_ssd = jnp.pad(w_in_bf[:, :, c0:c1], ((0, 0), (0, 0), (0, LANES - heads)))
    w_g = w_in_bf[:, :, c1:]
    w_pa, w_pb, w_o = w_proj_a.astype(BF16), w_proj_b.astype(BF16), w_out.astype(BF16)
    rw = jnp.pad(router_w, ((0, 0), (0, 0), (0, LANES - n_e))).astype(BF16)
    rb = jnp.pad(router_b, ((0, 0), (0, LANES - n_e)), constant_values=-jnp.inf).reshape(depth, 1, LANES)
    causal = jnp.tril(jnp.ones((CHUNK, CHUNK), bool))
    ws_p = jnp.where(causal, a_ws, 0.0)
    blk = jnp.where(jnp.tril(jnp.ones((lc_s, lc_s), bool)), a_ws[:, :, :lc_s, :lc_s], 0.0)
    eye = jnp.eye(CHUNK // lc_s, dtype=F32)
    ws_s = jnp.einsum("ab,lgts->lgatbs", eye, blk).reshape(depth, G_A, CHUNK, CHUNK)
    wmix = jnp.stack([ws_p, ws_s], axis=1).astype(BF16)
    dg = d_a // G_A
    bias_p = jnp.repeat(jnp.swapaxes(a_bs, 1, 2), dg, axis=2)
    bias_s = jnp.tile(bias_p[:, :lc_s], (1, CHUNK // lc_s, 1))
    bias = jnp.stack([bias_p, bias_s], axis=1)
    dtb = jnp.pad(b_dt_bias, ((0, 0), (0, LANES - heads))).reshape(depth, 1, LANES)
    alog = jnp.pad(b_a_log, ((0, 0), (0, LANES - heads))).reshape(depth, 1, LANES)
    dsk = jnp.repeat(b_d, P_B, axis=1).reshape(depth, 1, D_INNER)
    cb = b_conv_b.reshape(depth, 1, CONV_DIM)
    ng = b_norm_g.reshape(depth, 1, D_INNER)
    w_gu_all = w_gu.reshape(depth * n_e, d, w_gu.shape[-1])
    b_gu_all = b_gu.reshape(depth * n_e, 1, b_gu.shape[-1])
    w_dn_all = w_dn.reshape(depth * n_e, w_dn.shape[-2], d)
    b_dn_all = b_dn.reshape(depth * n_e, 1, d)
    ssm_in = state_ssm.reshape(depth * bs, G_B, GP, N_B)
    conv_in = state_conv.reshape(depth * bs, CONV_W - 1, state_conv.shape[-1])

    mod_all = _ada(jnp.concatenate([c_prompt, c_sample], axis=0), w_ada, b_ada)
    mod_all = mod_all.reshape(depth, bp + bs, N_MOD, d).transpose(0, 2, 1, 3)
    mod = jnp.concatenate([jnp.repeat(mod_all[:, :, :bp], tok.tm, axis=2),
                           jnp.repeat(mod_all[:, :, bp:], lc_s, axis=2)], axis=2)

    x = jnp.concatenate([x_prompt.reshape(t_p, d), x_sample.reshape(t_s, d)], axis=0)

    h = _comb(tok, x, norm1_g[0], mod, layer_n=0)[0]
    ssm_p, conv_p, conv_s, v_s = [], [], [], []
    ssm_s = None
    y_final = None
    for l in range(depth):
        oa, v_rows = _gmlp(h, w_uv[l], a_ln_g[l], a_ln_b[l], wmix[l], bias[l], t_p)
        ob, cp, sp = _ssd_prompt(h, w_ssd[l], b_conv_w[l], cb[l], dtb[l], alog[l], dsk[l], ng[l], bp, seq)
        proj_s = _proj(h[t_p:], w_ssd[l])
        ob, cs, ssm_s = _ssd_sample(proj_s, t_p, lc_s, b_conv_w[l], cb[l], dtb[l], alog[l], dsk[l], ng[l],
                                    conv_in, ssm_in, l, ob, ssm_s)
        x1, h2, gates, idx, rank, cnt = _mid(tok, l, h, oa, ob, x, mod, norm2_g[l], w_g[l], w_pa[l],
                                             w_pb[l], w_o[l], rw[l], rb[l])
        pos, blk_e, blk_first, n_used, fill0, filln, rows = _route(
            idx[:, :TOP_K], rank[:, :TOP_K], cnt[0, :n_e], n_e, moe_tm, tok.tm)
        xb = _dispatch(h2, pos, fill0, filln, n_used, rows, tm=tok.tm, blk=moe_tm)
        yb = _moe(xb, blk_e, blk_first, n_used, w_gu_all, b_gu_all, w_dn_all, b_dn_all, l * n_e, tm=moe_tm)
        if l + 1 < depth:
            x, h = _comb(tok, x1, norm1_g[l + 1], mod, layer_n=l + 1, yb=yb, pos=pos, gates=gates, layer_gt=l)
        else:
            y_final = _comb(tok, x1, final_g, mod, yb=yb, pos=pos, gates=gates, layer_gt=l)[0]
        ssm_p.append(sp.reshape(bp, heads, P_B, N_B))
        conv_p.append(cp)
        conv_s.append(cs)
        v_s.append(v_rows.reshape(bs, lc_s, d))
    return (y_final[:t_p].reshape(bp, seq, d), y_final[t_p:].reshape(bs, lc_s, d),
            jnp.stack(ssm_p), jnp.stack(conv_p), ssm_s.reshape(depth, bs, heads, P_B, N_B),
            jnp.stack(conv_s), jnp.stack(v_s))
```

```python
import functools

import jax
import jax.numpy as jnp
from jax import lax
from jax.experimental import pallas as pl
from jax.experimental.pallas import tpu as pltpu

F32 = jnp.float32
BF16 = jnp.bfloat16

EPS = 1e-6
SUBLANES = 8
LANES = 128
CHUNK = 128
G_A = 8
G_B = 8
R_B = 4
P_B = 64
N_B = 128
CONV_W = 4
TOP_K = 4
SWIGLU_LIMIT = 7.0
SWIGLU_ALPHA = 1.702
MIB = 2 ** 20
N_MOD = 6
K_SH1, K_SC1, K_GT1, K_SH2, K_SC2, K_GT2 = range(N_MOD)
DMA_UNROLL = 8


def _params(sem, vmem_mib):
    return pltpu.CompilerParams(dimension_semantics=sem, vmem_limit_bytes=vmem_mib * MIB)


def _sigmoid(x):
    return 1.0 / (1.0 + jnp.exp(-x))


def _silu(x):
    return x * _sigmoid(x)


def _gelu(x):
    return 0.5 * x * (1.0 + lax.erf(x * (0.5 ** 0.5)))


def _softplus(x):
    return jnp.maximum(x, 0.0) + jnp.log(1.0 + jnp.exp(-jnp.abs(x)))


def _dot(a, b):
    return jnp.dot(a, b, preferred_element_type=F32)


def _dot_nt(a, b):
    return lax.dot_general(a, b, (((1,), (1,)), ((), ())), preferred_element_type=F32)


class _Tokens:
    def __init__(self, bp, seq, bs, lc_s, tm):
        self.tm = tm
        self.t_p, self.t_s = bp * seq, bs * lc_s
        self.t = self.t_p + self.t_s
        self.n_tiles = self.t // tm
        self.n_p = self.t_p // tm
        self.per_seq = seq // tm
        self.bp = bp

    def mod_specs(self, layer, k, d):
        return [pl.BlockSpec((None, None, self.bp, d), lambda i: (layer, k, 0, 0)),
                pl.BlockSpec((None, None, self.tm, d),
                             lambda i: (layer, k, jnp.maximum(i - self.n_p, 0), 0))]

    def mod_value(self, refs):
        mp_ref, ms_ref = refs
        i = pl.program_id(0)
        seq_row = mp_ref[pl.ds(jnp.minimum(i // self.per_seq, self.bp - 1), 1), :]
        return jnp.where(i < self.n_p, seq_row, ms_ref[...])


def _ada_kernel(c_ref, w_ref, b_ref, o_ref):
    c = c_ref[...]
    o_ref[...] = _dot(_silu(c).astype(BF16), w_ref[...].astype(BF16)) + b_ref[...]


def _ada(c_all, w_ada, b_ada):
    depth, d, n = w_ada.shape
    nc = c_all.shape[0]
    tn = 1536
    return pl.pallas_call(
        _ada_kernel,
        grid=(depth, n // tn),
        in_specs=[pl.BlockSpec((nc, d), lambda l, j: (0, 0)),
                  pl.BlockSpec((None, d, tn), lambda l, j: (l, 0, j)),
                  pl.BlockSpec((None, 1, tn), lambda l, j: (l, 0, j))],
        out_specs=pl.BlockSpec((None, nc, tn), lambda l, j: (l, 0, j)),
        out_shape=jax.ShapeDtypeStruct((depth, nc, n), F32),
        compiler_params=_params(("arbitrary", "arbitrary"), 40),
        name="ada",
    )(c_all, w_ada, b_ada.reshape(depth, 1, n))


def _row_copy(src_ref, src_row, dst_ref, dst_row, sem):
    return pltpu.make_async_copy(src_ref.at[pl.ds(src_row, 1)], dst_ref.at[pl.ds(dst_row, 1)], sem)


def _comb_kernel(*refs, tok, combine, final):
    it = iter(refs)
    if combine:
        pos_ref, posn_ref = next(it), next(it)
    x_ref = next(it)
    if combine:
        yb_ref, g_ref, gt_refs = next(it), next(it), (next(it), next(it))
    n_ref = next(it)
    if not final:
        sc_refs, sh_refs = (next(it), next(it)), (next(it), next(it))
    if combine and not final:
        xo_ref = next(it)
    o_ref = next(it)
    if combine:
        ybuf, sem = next(it), next(it)

    x = x_ref[...]
    if combine:
        i = pl.program_id(0)
        n = pl.num_programs(0)
        tm = ybuf.shape[2]
        slot = i % 2

        def gather(p_ref, s):
            def body(j, c):
                for u in range(DMA_UNROLL):
                    t = j * DMA_UNROLL + u
                    for k in range(TOP_K):
                        _row_copy(yb_ref, p_ref[k, t], ybuf.at[s, k], t, sem.at[s]).start(
                            priority=(u * TOP_K + k) % 2)
                return c
            lax.fori_loop(0, tm // DMA_UNROLL, body, 0)

        @pl.when(i == 0)
        def _():
            gather(pos_ref, 0)

        @pl.when(i + 1 < n)
        def _():
            gather(posn_ref, 1 - slot)

        for k in range(TOP_K):
            pltpu.make_async_copy(yb_ref.at[pl.ds(0, tm)], ybuf.at[slot, k], sem.at[slot]).wait()

        g = g_ref[...]
        acc = g[:, 0:1] * ybuf[slot, 0]
        for k in range(1, TOP_K):
            acc = acc + g[:, k:k + 1] * ybuf[slot, k]
        x = x + tok.mod_value(gt_refs) * acc
        if not final:
            xo_ref[...] = x
    ms = jnp.mean(x * x, axis=-1, keepdims=True)
    xn = x * lax.rsqrt(ms + EPS)
    if final:
        o_ref[...] = xn * n_ref[...]
    else:
        o_ref[...] = (xn * n_ref[...] * (1.0 + tok.mod_value(sc_refs)) + tok.mod_value(sh_refs)).astype(BF16)


def _comb(tok, x, norm_g, mod, layer_n=None, yb=None, pos=None, gates=None, layer_gt=None):
    t, d = x.shape
    tm = tok.tm
    combine = yb is not None
    final = layer_n is None
    row = lambda w: pl.BlockSpec((tm, w), lambda i: (i, 0))

    args, specs, scratch = [], [], []
    if combine:
        args += [pos, pos]
        specs += [pl.BlockSpec((None, TOP_K, tm), lambda i: (i, 0, 0), memory_space=pltpu.SMEM),
                  pl.BlockSpec((None, TOP_K, tm), lambda i: (jnp.minimum(i + 1, tok.n_tiles - 1), 0, 0),
                               memory_space=pltpu.SMEM)]
        scratch = [pltpu.VMEM((2, TOP_K, tm, d), F32), pltpu.SemaphoreType.DMA((2,))]
    args.append(x)
    specs.append(row(d))
    if combine:
        args += [yb, gates, *mod]
        specs += [pl.BlockSpec(memory_space=pl.ANY), row(LANES), *tok.mod_specs(layer_gt, K_GT2, d)]
    args.append(norm_g.reshape(1, d))
    specs.append(pl.BlockSpec((1, d), lambda i: (0, 0)))
    if not final:
        args += [*mod, *mod]
        specs += [*tok.mod_specs(layer_n, K_SC1, d), *tok.mod_specs(layer_n, K_SH1, d)]
    out_shape, out_specs = [], []
    if combine and not final:
        out_shape.append(jax.ShapeDtypeStruct((t, d), F32))
        out_specs.append(row(d))
    out_shape.append(jax.ShapeDtypeStruct((t, d), F32 if final else BF16))
    out_specs.append(row(d))
    return pl.pallas_call(
        functools.partial(_comb_kernel, tok=tok, combine=combine, final=final),
        grid=(tok.n_tiles,),
        in_specs=specs, out_specs=out_specs, out_shape=out_shape,
        scratch_shapes=scratch,
        compiler_params=_params(("arbitrary",), 48),
        name="comb",
    )(*args)


def _gmlp_kernel(h_ref, w_ref, lng_ref, lnb_ref, wmix_ref, bias_ref, o_ref, v_ref, *, n_prompt_tiles):
    i = pl.program_id(0)
    tm, d = o_ref.shape
    uv = _dot(h_ref[...], w_ref[...])
    u = _gelu(uv[:, :d])
    v = _gelu(uv[:, d:])
    mu = jnp.mean(v, axis=-1, keepdims=True)
    vc = v - mu
    var = jnp.mean(vc * vc, axis=-1, keepdims=True)
    vn = vc * lax.rsqrt(var + EPS) * lng_ref[...] + lnb_ref[...]

    @pl.when(i >= n_prompt_tiles)
    def _():
        v_ref[...] = vn

    vb = vn.astype(BF16)
    dg = d // G_A
    for c in range(tm // CHUNK):
        rows = slice(c * CHUNK, (c + 1) * CHUNK)
        for g in range(G_A):
            cols = slice(g * dg, (g + 1) * dg)
            s = _dot(wmix_ref[g], vb[rows, cols]) + bias_ref[:, cols]
            o_ref[rows, cols] = (u[rows, cols] * s).astype(BF16)


def _gmlp(h, w_uv, ln_g, ln_b, wmix, bias, t_prompt, *, tm=512):
    t, d = h.shape
    n_p = t_prompt // tm
    t_s = t - t_prompt
    sel = lambda i: jnp.where(i >= n_p, 1, 0)
    return pl.pallas_call(
        functools.partial(_gmlp_kernel, n_prompt_tiles=n_p),
        grid=(t // tm,),
        in_specs=[pl.BlockSpec((tm, d), lambda i: (i, 0)),
                  pl.BlockSpec((d, 2 * d), lambda i: (0, 0)),
                  pl.BlockSpec((1, d), lambda i: (0, 0)),
                  pl.BlockSpec((1, d), lambda i: (0, 0)),
                  pl.BlockSpec((None, G_A, CHUNK, CHUNK), lambda i: (sel(i), 0, 0, 0)),
                  pl.BlockSpec((None, CHUNK, d), lambda i: (sel(i), 0, 0))],
        out_specs=[pl.BlockSpec((tm, d), lambda i: (i, 0)),
                   pl.BlockSpec((tm, d), lambda i: (jnp.maximum(i - n_p, 0), 0))],
        out_shape=[jax.ShapeDtypeStruct((t, d), BF16),
                   jax.ShapeDtypeStruct((t_s, d), F32)],
        compiler_params=_params(("arbitrary",), 48),
        name="gmlp",
    )(h, w_uv, ln_g.reshape(1, d), ln_b.reshape(1, d), wmix, bias)


def _proj_kernel(x_ref, w_ref, o_ref):
    o_ref[...] = _dot(x_ref[...], w_ref[...])


def _proj(x, w, *, tm=256):
    t, d = x.shape
    n = w.shape[1]
    return pl.pallas_call(
        _proj_kernel,
        grid=(t // tm,),
        in_specs=[pl.BlockSpec((tm, d), lambda i: (i, 0)),
                  pl.BlockSpec((d, n), lambda i: (0, 0))],
        out_specs=pl.BlockSpec((tm, n), lambda i: (i, 0)),
        out_shape=jax.ShapeDtypeStruct((t, n), F32),
        compiler_params=_params(("arbitrary",), 56),
        name="inproj",
    )(x, w)


D_INNER = G_B * R_B * P_B
GN = G_B * N_B
CONV_DIM = D_INNER + 2 * GN
GP = R_B * P_B
HIST = SUBLANES


def _ssd_stage_a(proj_ref, cw_ref, cb_ref, dtb_ref, alog_ref, cs_ref, xact_ref, acum_ref, acumT_ref,
                 dtpT_ref, *, nb, lc):
    r = nb * lc
    cs_ref[:, HIST:HIST + lc, :] = proj_ref[:, D_INNER:D_INNER + CONV_DIM].reshape(nb, lc, CONV_DIM)
    for j in range(CONV_DIM // LANES):
        cols = slice(j * LANES, (j + 1) * LANES)
        full = cs_ref[:, :, cols]
        conv = cb_ref[:, cols] + full[:, HIST:, :] * cw_ref[CONV_W - 1:CONV_W, cols]
        for s in range(1, CONV_W):
            back = pltpu.roll(full, s, axis=1)[:, HIST:, :]
            conv = conv + back * cw_ref[CONV_W - 1 - s:CONV_W - s, cols]
        xact_ref[:, cols] = _silu(conv).reshape(r, LANES)
    tail = cs_ref[:, HIST + lc - 3:HIST + lc, :]

    dt = proj_ref[:, D_INNER + CONV_DIM:D_INNER + CONV_DIM + LANES]
    dtp = _softplus(dt + dtb_ref[...])
    acum = dtp * (-jnp.exp(alog_ref[...]))
    local = lax.broadcasted_iota(jnp.int32, (r, LANES), 0) % lc
    sh = 1
    while sh < lc:
        acum = acum + jnp.where(local >= sh, pltpu.roll(acum, sh, axis=0), 0.0)
        sh *= 2
    acum_ref[...] = acum
    acumT_ref[...] = acum.T
    dtpT_ref[...] = dtp.T
    return tail


def _ssd_heads(g, mask, cb, yoff, xact_ref, acum_ref, acumT_ref, dtpT_ref, dsk_ref):
    r = xact_ref.shape[0]
    first = lax.broadcasted_iota(jnp.int32, (r, LANES), 1) < P_B
    ys = []
    for pair in range(R_B // 2):
        c0 = g * GP + pair * LANES
        xs = xact_ref[:, c0:c0 + LANES]
        xs_bf = xs.astype(BF16)
        colbs, ds = [], []
        for rr in range(2):
            hh = g * R_B + pair * 2 + rr
            colb = jnp.broadcast_to(acum_ref[:, hh:hh + 1], (r, r))
            seg = colb - acumT_ref[hh:hh + 1, :]
            m = cb * jnp.exp(jnp.where(mask, seg, -jnp.inf)) * dtpT_ref[hh:hh + 1, :]
            colbs.append(colb)
            ds.append(_dot(m.astype(BF16), xs_bf))
        y = jnp.where(first, ds[0], ds[1])
        y = y + jnp.exp(jnp.where(first, colbs[0], colbs[1])) * yoff[:, pair * LANES:(pair + 1) * LANES]
        y = y + dsk_ref[:, c0:c0 + LANES] * xs
        ys.append(y)
    return jnp.concatenate(ys, axis=1)


def _ssd_gate_norm(y, z, ng):
    yz = y * _silu(z)
    ms = jnp.mean(yz * yz, axis=-1, keepdims=True)
    return (yz * lax.rsqrt(ms + EPS) * ng).astype(BF16)


def _head_rows(vals):
    return jnp.concatenate([jnp.broadcast_to(v, (P_B, N_B)) for v in vals], axis=0)


def _ssd_prompt_kernel(h_ref, w_ref, cw_ref, cb_ref, dtb_ref, alog_ref, dsk_ref, ng_ref,
                       ob_ref, cout_ref, hout_ref,
                       proj_ref, cs_ref, hs_ref, xact_ref, acum_ref, acumT_ref, dtpT_ref):
    c = pl.program_id(1)
    lc = CHUNK

    @pl.when(c == 0)
    def _():
        cs_ref[:, 0:HIST, :] = jnp.zeros((1, HIST, CONV_DIM), F32)
        hs_ref[...] = jnp.zeros(hs_ref.shape, F32)

    proj_ref[...] = _dot(h_ref[...], w_ref[...])

    tail = _ssd_stage_a(proj_ref, cw_ref, cb_ref, dtb_ref, alog_ref, cs_ref, xact_ref, acum_ref,
                        acumT_ref, dtpT_ref, nb=1, lc=lc)
    cs_ref[:, HIST - 3:HIST, :] = tail

    @pl.when(c == pl.num_programs(1) - 1)
    def _():
        cout_ref[...] = tail[0]

    li = lax.broadcasted_iota(jnp.int32, (lc, lc), 0)
    si = lax.broadcasted_iota(jnp.int32, (lc, lc), 1)
    mask = li >= si
    acum_t = acumT_ref[...]
    wt = dtpT_ref[...] * jnp.exp(acum_t[:, lc - 1:lc] - acum_t)
    for g in range(G_B):
        bg = xact_ref[:, D_INNER + g * N_B:D_INNER + (g + 1) * N_B].astype(BF16)
        cg = xact_ref[:, D_INNER + GN + g * N_B:D_INNER + GN + (g + 1) * N_B].astype(BF16)
        cb = _dot_nt(cg, bg)
        hg = hs_ref[g]
        yoff = _dot_nt(cg, hg.astype(BF16))
        y = _ssd_heads(g, mask, cb, yoff, xact_ref, acum_ref, acumT_ref, dtpT_ref, dsk_ref)
        cols = slice(g * GP, (g + 1) * GP)
        ob_ref[:, cols] = _ssd_gate_norm(y, proj_ref[:, cols], ng_ref[:, cols])
        xs_t = xact_ref[:, cols].T
        xw = jnp.concatenate(
            [xs_t[rr * P_B:(rr + 1) * P_B, :] * wt[g * R_B + rr:g * R_B + rr + 1, :] for rr in range(R_B)],
            axis=0)
        inc = _dot(xw.astype(BF16), bg)
        dec = _head_rows([jnp.exp(acum_t[g * R_B + rr:g * R_B + rr + 1, lc - 1:lc]) for rr in range(R_B)])
        hs_ref[g] = dec * hg + inc

    @pl.when(c == pl.num_programs(1) - 1)
    def _():
        hout_ref[...] = hs_ref[...]


def _ssd_prompt(h, w, cw, cb, dtb, alog, dsk, ng, batch, seq):
    t, d = h.shape
    n = w.shape[1]
    nc = seq // CHUNK
    vec = lambda w: pl.BlockSpec((1, w), lambda b, c: (0, 0))
    return pl.pallas_call(
        _ssd_prompt_kernel,
        grid=(batch, nc),
        in_specs=[pl.BlockSpec((CHUNK, d), lambda b, c: (b * nc + c, 0)),
                  pl.BlockSpec((d, n), lambda b, c: (0, 0)),
                  pl.BlockSpec((CONV_W, CONV_DIM), lambda b, c: (0, 0)),
                  vec(CONV_DIM), vec(LANES), vec(LANES), vec(D_INNER), vec(D_INNER)],
        out_specs=[pl.BlockSpec((CHUNK, D_INNER), lambda b, c: (b * nc + c, 0)),
                   pl.BlockSpec((None, CONV_W - 1, CONV_DIM), lambda b, c: (b, 0, 0)),
                   pl.BlockSpec((None, G_B, GP, N_B), lambda b, c: (b, 0, 0, 0))],
        out_shape=[jax.ShapeDtypeStruct((t, D_INNER), BF16),
                   jax.ShapeDtypeStruct((batch, CONV_W - 1, CONV_DIM), F32),
                   jax.ShapeDtypeStruct((batch, G_B, GP, N_B), F32)],
        scratch_shapes=[pltpu.VMEM((CHUNK, n), F32),
                        pltpu.VMEM((1, HIST + CHUNK, CONV_DIM), F32),
                        pltpu.VMEM((G_B, GP, N_B), F32),
                        pltpu.VMEM((CHUNK, CONV_DIM), F32),
                        pltpu.VMEM((CHUNK, LANES), F32),
                        pltpu.VMEM((LANES, CHUNK), F32),
                        pltpu.VMEM((LANES, CHUNK), F32)],
        compiler_params=_params(("arbitrary", "arbitrary"), 56),
        name="ssd_prompt",
    )(h, w, cw, cb, dtb, alog, dsk, ng)


def _ssd_sample_kernel(*refs, nb, lc, chained):
    (proj_ref, cw_ref, cb_ref, dtb_ref, alog_ref, dsk_ref, ng_ref, cin_ref, hin_ref) = refs[:9]
    n_alias = 2 if chained else 1
    (ob_ref, cout_ref, hout_ref,
     cs_ref, xact_ref, acum_ref, acumT_ref, dtpT_ref, yoff_ref) = refs[9 + n_alias:]
    g = pl.program_id(1)
    r = nb * lc

    @pl.when(g == 0)
    def _():
        cs_ref[:, HIST - 3:HIST, :] = cin_ref[...]
        tail = _ssd_stage_a(proj_ref, cw_ref, cb_ref, dtb_ref, alog_ref, cs_ref, xact_ref, acum_ref,
                            acumT_ref, dtpT_ref, nb=nb, lc=lc)
        cout_ref[...] = tail

    li = lax.broadcasted_iota(jnp.int32, (r, r), 0)
    si = lax.broadcasted_iota(jnp.int32, (r, r), 1)
    mask = (li >= si) & ((li // lc) == (si // lc))
    lane_seq = lax.broadcasted_iota(jnp.int32, (GP, r), 1) // lc
    acum_t = acumT_ref[...]
    dtp_t = dtpT_ref[...]

    def branch(gg):
        bg = xact_ref[:, D_INNER + gg * N_B:D_INNER + (gg + 1) * N_B].astype(BF16)
        cg = xact_ref[:, D_INNER + GN + gg * N_B:D_INNER + GN + (gg + 1) * N_B].astype(BF16)
        cb = _dot_nt(cg, bg)
        cols = slice(gg * GP, (gg + 1) * GP)
        xs_t = xact_ref[:, cols].T
        win = 2 * SUBLANES
        for b in range(nb):
            h0 = hin_ref[b, 0]
            w0 = (b * lc // win) * win
            yo = _dot_nt(cg[w0:w0 + win], h0.astype(BF16))
            yoff_ref[b * lc:(b + 1) * lc, :] = yo[b * lc - w0:b * lc - w0 + lc]
            last = b * lc + lc - 1
            xw = jnp.concatenate(
                [xs_t[rr * P_B:(rr + 1) * P_B, :]
                 * (dtp_t[gg * R_B + rr:gg * R_B + rr + 1, :]
                    * jnp.exp(acum_t[gg * R_B + rr:gg * R_B + rr + 1, last:last + 1]
                              - acum_t[gg * R_B + rr:gg * R_B + rr + 1, :]))
                 for rr in range(R_B)], axis=0)
            xw = jnp.where(lane_seq == b, xw, 0.0)
            inc = _dot(xw.astype(BF16), bg)
            dec = _head_rows([jnp.exp(acum_t[gg * R_B + rr:gg * R_B + rr + 1, last:last + 1])
                              for rr in range(R_B)])
            hout_ref[b, 0] = dec * h0 + inc
        y = _ssd_heads(gg, mask, cb, yoff_ref[...], xact_ref, acum_ref, acumT_ref, dtpT_ref, dsk_ref)
        ob_ref[...] = _ssd_gate_norm(y, proj_ref[:, cols], ng_ref[:, cols])

    for gg in range(G_B):
        pl.when(g == gg)(functools.partial(branch, gg))


def _ssd_sample(proj, row0, lc, cw, cb, dtb, alog, dsk, ng, conv_state, ssm_state, layer, ob, ssm_out,
                *, nb=16):
    n = proj.shape[1]
    batch = proj.shape[0] // lc
    r = nb * lc
    blk0 = row0 // r
    sb0 = layer * (batch // nb)
    chained = ssm_out is not None
    vec = lambda w: pl.BlockSpec((1, w), lambda i, g: (0, 0))
    state_spec = pl.BlockSpec((nb, 1, GP, N_B), lambda i, g: (sb0 + i, g, 0, 0))
    hbm = pl.BlockSpec(memory_space=pl.ANY)
    args = [proj, cw, cb, dtb, alog, dsk, ng, conv_state, ssm_state, ob] + ([ssm_out] if chained else [])
    aliases = {9: 0, 10: 2} if chained else {9: 0}
    return pl.pallas_call(
        functools.partial(_ssd_sample_kernel, nb=nb, lc=lc, chained=chained),
        grid=(batch // nb, G_B),
        in_specs=[pl.BlockSpec((r, n), lambda i, g: (i, 0)),
                  pl.BlockSpec((CONV_W, CONV_DIM), lambda i, g: (0, 0)),
                  vec(CONV_DIM), vec(LANES), vec(LANES), vec(D_INNER), vec(D_INNER),
                  pl.BlockSpec((nb, CONV_W - 1, CONV_DIM), lambda i, g: (sb0 + i, 0, 0)),
                  state_spec, hbm] + ([hbm] if chained else []),
        out_specs=[pl.BlockSpec((r, GP), lambda i, g: (blk0 + i, g)),
                   pl.BlockSpec((nb, CONV_W - 1, CONV_DIM), lambda i, g: (i, 0, 0)),
                   state_spec],
        out_shape=[jax.ShapeDtypeStruct(ob.shape, BF16),
                   jax.ShapeDtypeStruct((batch, CONV_W - 1, CONV_DIM), F32),
                   jax.ShapeDtypeStruct(ssm_state.shape, F32)],
        input_output_aliases=aliases,
        scratch_shapes=[pltpu.VMEM((nb, HIST + lc, CONV_DIM), F32),
                        pltpu.VMEM((r, CONV_DIM), F32),
                        pltpu.VMEM((r, LANES), F32),
                        pltpu.VMEM((LANES, r), F32),
                        pltpu.VMEM((LANES, r), F32),
                        pltpu.VMEM((r, GP), F32)],
        compiler_params=_params(("arbitrary", "arbitrary"), 48),
        name="ssd_sample",
    )(*args)


def _mid_kernel(h_ref, oa_ref, ob_ref, x_ref, gtp_ref, gts_ref, scp_ref, scs_ref, shp_ref, shs_ref, n2_ref,
                wg_ref, wpa_ref, wpb_ref, wo_ref, rw_ref, rb_ref,
                x1_ref, h2_ref, gates_ref, idx_ref, rank_ref, cnt_ref, carry_ref, *, tok):
    tm, d = h2_ref.shape
    gab = _dot(h_ref[...], wg_ref[...])
    m = (_sigmoid(gab[:, :d]) * _dot(oa_ref[...], wpa_ref[...])
         + _sigmoid(gab[:, d:]) * _dot(ob_ref[...], wpb_ref[...]))
    x1 = x_ref[...] + tok.mod_value((gtp_ref, gts_ref)) * _dot(m.astype(BF16), wo_ref[...])
    x1_ref[...] = x1
    ms = jnp.mean(x1 * x1, axis=-1, keepdims=True)
    h2 = (x1 * lax.rsqrt(ms + EPS) * n2_ref[...] * (1.0 + tok.mod_value((scp_ref, scs_ref)))
          + tok.mod_value((shp_ref, shs_ref)))
    h2_ref[...] = h2

    logits = _dot(h2.astype(BF16), rw_ref[...]) + rb_ref[...]
    lane = lax.broadcasted_iota(jnp.int32, logits.shape, 1).astype(F32)
    vals, idxs = [], []
    for _ in range(TOP_K):
        mx = jnp.max(logits, axis=-1, keepdims=True)
        ix = jnp.min(jnp.where(logits == mx, lane, float(LANES)), axis=-1, keepdims=True)
        vals.append(mx)
        idxs.append(ix)
        logits = jnp.where(lane == ix, -jnp.inf, logits)
    es = [jnp.exp(v - vals[0]) for v in vals]
    tot = es[0] + es[1] + es[2] + es[3]
    gates = jnp.zeros(logits.shape, F32)
    idx = jnp.zeros(logits.shape, F32)
    for k in range(TOP_K):
        gates = jnp.where(lane == float(k), es[k] / tot, gates)
        idx = jnp.where(lane == float(k), idxs[k], idx)
    gates_ref[...] = gates
    idx_ref[...] = idx.astype(jnp.int32)

    @pl.when(pl.program_id(0) == 0)
    def _():
        carry_ref[...] = jnp.zeros(carry_ref.shape, F32)

    earlier = (lax.broadcasted_iota(jnp.int32, (tm, tm), 1)
               < lax.broadcasted_iota(jnp.int32, (tm, tm), 0)).astype(BF16)
    onehots = [(lane == idxs[k]).astype(F32) for k in range(TOP_K)]
    prefix = _dot(earlier, jnp.concatenate(onehots, axis=1).astype(BF16))
    base = carry_ref[...]
    rank = jnp.zeros(logits.shape, F32)
    for k in range(TOP_K):
        before = prefix[:, k * LANES:(k + 1) * LANES] + base
        rank = jnp.where(lane == float(k), jnp.sum(onehots[k] * before, axis=-1, keepdims=True), rank)
        base = base + jnp.sum(onehots[k], axis=0, keepdims=True)
    carry_ref[...] = base
    rank_ref[...] = rank.astype(jnp.int32)
    cnt_ref[...] = base.astype(jnp.int32)


def _mid(tok, layer, h, oa, ob, x, mod, n2, wg, wpa, wpb, wo, rw, rb):
    t, d = h.shape
    tm = tok.tm
    row = lambda w: pl.BlockSpec((tm, w), lambda i: (i, 0))
    full = lambda a: pl.BlockSpec(a.shape, lambda i: (0,) * a.ndim)
    n2 = n2.reshape(1, d)
    return pl.pallas_call(
        functools.partial(_mid_kernel, tok=tok),
        grid=(tok.n_tiles,),
        in_specs=[row(d), row(d), row(2 * d), row(d),
                  *tok.mod_specs(layer, K_GT1, d), *tok.mod_specs(layer, K_SC2, d),
                  *tok.mod_specs(layer, K_SH2, d),
                  full(n2), full(wg), full(wpa), full(wpb), full(wo), full(rw), full(rb)],
        out_specs=[row(d), row(d), row(LANES), row(LANES), row(LANES),
                   pl.BlockSpec((1, LANES), lambda i: (0, 0))],
        out_shape=[jax.ShapeDtypeStruct((t, d), F32),
                   jax.ShapeDtypeStruct((t, d), F32),
                   jax.ShapeDtypeStruct((t, LANES), F32),
                   jax.ShapeDtypeStruct((t, LANES), jnp.int32),
                   jax.ShapeDtypeStruct((t, LANES), jnp.int32),
                   jax.ShapeDtypeStruct((1, LANES), jnp.int32)],
        scratch_shapes=[pltpu.VMEM((1, LANES), F32)],
        compiler_params=_params(("arbitrary",), 56),
        name="mid",
    )(h, oa, ob, x, *mod, *mod, *mod, n2, wg, wpa, wpb, wo, rw, rb)


def _dispatch_kernel(fill0_ref, filln_ref, n_used_ref, pos_ref, h_ref, xb_ref, zrow, sem, zsem):
    i = pl.program_id(0)
    tm = h_ref.shape[0]
    n_e = filln_ref.shape[0]
    blk = zrow.shape[0]
    n_blocks = xb_ref.shape[0] // blk

    def scatter(j, c):
        for u in range(DMA_UNROLL):
            t = j * DMA_UNROLL + u
            for k in range(TOP_K):
                _row_copy(h_ref, t, xb_ref, pos_ref[k, t], sem).start(priority=(u * TOP_K + k) % 2)
        return c
    lax.fori_loop(0, tm // DMA_UNROLL, scatter, 0)

    @pl.when(i == pl.num_programs(0) - 1)
    def _():
        zrow[...] = jnp.zeros(zrow.shape, F32)

        def fill_expert(e, c):
            def fill(r, c2):
                _row_copy(zrow, 0, xb_ref, fill0_ref[e] + r, zsem).start()
                return c2
            return lax.fori_loop(0, filln_ref[e], fill, c)
        lax.fori_loop(0, n_e, fill_expert, 0)

        def drain_expert(e, c):
            def drain(r, c2):
                _row_copy(zrow, 0, xb_ref, 0, zsem).wait()
                return c2
            return lax.fori_loop(0, filln_ref[e], drain, c)
        lax.fori_loop(0, n_e, drain_expert, 0)

        def block_copy(b):
            return pltpu.make_async_copy(zrow, xb_ref.at[pl.ds(pl.multiple_of(b * blk, blk), blk)], zsem)

        def fill_block(b, c):
            block_copy(b).start()
            return c
        lax.fori_loop(n_used_ref[0], n_blocks, fill_block, 0)

        def drain_block(b, c):
            block_copy(b).wait()
            return c
        lax.fori_loop(n_used_ref[0], n_blocks, drain_block, 0)

    for k in range(TOP_K):
        pltpu.make_async_copy(h_ref, xb_ref.at[pl.ds(0, tm)], sem).wait()


def _dispatch(h2, pos, fill0, filln, n_used, rows, *, tm, blk):
    t, d = h2.shape
    grid_spec = pltpu.PrefetchScalarGridSpec(
        num_scalar_prefetch=3,
        grid=(t // tm,),
        in_specs=[pl.BlockSpec((None, TOP_K, tm), lambda i, f0, fn, nu: (i, 0, 0), memory_space=pltpu.SMEM),
                  pl.BlockSpec((tm, d), lambda i, f0, fn, nu: (i, 0))],
        out_specs=pl.BlockSpec(memory_space=pl.ANY),
        scratch_shapes=[pltpu.VMEM((blk, d), F32), pltpu.SemaphoreType.DMA(()),
                        pltpu.SemaphoreType.DMA(())])
    return pl.pallas_call(
        _dispatch_kernel,
        grid_spec=grid_spec,
        out_shape=jax.ShapeDtypeStruct((rows, d), F32),
        compiler_params=_params(("arbitrary",), 32),
        name="dispatch",
    )(fill0, filln, n_used, pos, h2)


def _moe_kernel(blk_e_ref, blk_first_ref, n_used_ref, x_ref, wgu_ref, bgu_ref, wdn_ref, bdn_ref, o_ref,
                wgu_bf, wdn_bf):
    i = pl.program_id(0)
    d_ff = wdn_bf.shape[0]

    @pl.when(blk_first_ref[i] == 1)
    def _():
        wgu_bf[...] = wgu_ref[...].astype(BF16)
        wdn_bf[...] = wdn_ref[...].astype(BF16)

    @pl.when(i < n_used_ref[0])
    def _():
        gu = _dot(x_ref[...].astype(BF16), wgu_bf[...]) + bgu_ref[...]
        g = jnp.minimum(gu[:, :d_ff], SWIGLU_LIMIT)
        u = jnp.clip(gu[:, d_ff:], -SWIGLU_LIMIT, SWIGLU_LIMIT)
        act = g * _sigmoid(SWIGLU_ALPHA * g) * (u + 1.0)
        o_ref[...] = _dot(act.astype(BF16), wdn_bf[...]) + bdn_ref[...]

    @pl.when(i >= n_used_ref[0])
    def _():
        o_ref[...] = jnp.zeros(o_ref.shape, F32)


def _moe(xb, blk_e, blk_first, n_used, w_gu, b_gu, w_dn, b_dn, e0, *, tm):
    rows, d = xb.shape
    _, _, d_gu = w_gu.shape
    d_ff = w_dn.shape[1]
    expert = lambda i, be, bf, nu: (e0 + be[i], 0, 0)
    grid_spec = pltpu.PrefetchScalarGridSpec(
        num_scalar_prefetch=3,
        grid=(rows // tm,),
        in_specs=[pl.BlockSpec((tm, d), lambda i, be, bf, nu: (jnp.maximum(jnp.minimum(i, nu[0] - 1), 0), 0)),
                  pl.BlockSpec((None, d, d_gu), expert),
                  pl.BlockSpec((None, 1, d_gu), expert),
                  pl.BlockSpec((None, d_ff, d), expert),
                  pl.BlockSpec((None, 1, d), expert)],
        out_specs=pl.BlockSpec((tm, d), lambda i, be, bf, nu: (i, 0)),
        scratch_shapes=[pltpu.VMEM((d, d_gu), BF16), pltpu.VMEM((d_ff, d), BF16)])
    return pl.pallas_call(
        _moe_kernel,
        grid_spec=grid_spec,
        out_shape=jax.ShapeDtypeStruct((rows, d), F32),
        compiler_params=_params(("arbitrary",), 56),
        name="moe",
    )(blk_e, blk_first, n_used, xb, w_gu, b_gu, w_dn, b_dn)


def _route(idx, rank, counts, n_e, tm, tm_tok):
    t = idx.shape[0]
    padded = (counts + tm - 1) // tm * tm
    pad_end = jnp.cumsum(padded)
    pad_start = pad_end - padded
    experts = jnp.arange(n_e, dtype=jnp.int32)
    pos = jnp.sum(jnp.where(idx[..., None] == experts, pad_start, 0), axis=-1) + rank
    pos = pos.reshape(t // tm_tok, tm_tok, TOP_K).transpose(0, 2, 1).astype(jnp.int32)
    n_blocks = -(-t * TOP_K // tm) + n_e
    blk_start = jnp.arange(n_blocks, dtype=jnp.int32) * tm
    blk_e = jnp.minimum(jnp.sum((blk_start[:, None] >= pad_end[None, :]).astype(jnp.int32), axis=1),
                        n_e - 1).astype(jnp.int32)
    blk_first = jnp.concatenate([jnp.ones((1,), jnp.int32),
                                 (blk_e[1:] != blk_e[:-1]).astype(jnp.int32)])
    n_used = (pad_end[-1:] // tm).astype(jnp.int32)
    return (pos, blk_e, blk_first, n_used, (pad_start + counts).astype(jnp.int32),
            (padded - counts).astype(jnp.int32), n_blocks * tm)


def kernel(x_prompt, x_sample, c_prompt, c_sample, state_ssm, state_conv, norm1_g, norm2_g, final_g,
           w_ada, b_ada, w_in, a_ln_g, a_ln_b, a_ws, a_bs, b_conv_w, b_conv_b, b_dt_bias, b_a_log, b_d,
           b_norm_g, w_proj_a, w_proj_b, w_out, router_w, router_b, w_gu, b_gu, w_dn, b_dn):
    bp, seq, d = x_prompt.shape
    bs, lc_s, _ = x_sample.shape
    depth = w_ada.shape[0]
    n_e = router_w.shape[-1]
    heads = b_d.shape[-1]
    moe_tm = 256
    tok = _Tokens(bp, seq, bs, lc_s, tm=256)
    t_p, t_s, t = tok.t_p, tok.t_s, tok.t

    d_a = d
    c0 = 2 * d_a
    c1 = c0 + D_INNER + CONV_DIM + heads
    w_in_bf = w_in.astype(BF16)
    w_uv = w_in_bf[:, :, :c0]
    w_ssd = jnp.pad(w_in_bf[:, :, c0:c1], ((0, 0), (0, 0), (0, LANES - heads)))
    w_g = w_in_bf[:, :, c1:]
    w_pa, w_pb, w_o = w_proj_a.astype(BF16), w_proj_b.astype(BF16), w_out.astype(BF16)
    rw = jnp.pad(router_w, ((0, 0), (0, 0), (0, LANES - n_e))).astype(BF16)
    rb = jnp.pad(router_b, ((0, 0), (0, LANES - n_e)), constant_values=-jnp.inf).reshape(depth, 1, LANES)
    causal = jnp.tril(jnp.ones((CHUNK, CHUNK), bool))
    ws_p = jnp.where(causal, a_ws, 0.0)
    blk = jnp.where(jnp.tril(jnp.ones((lc_s, lc_s), bool)), a_ws[:, :, :lc_s, :lc_s], 0.0)
    eye = jnp.eye(CHUNK // lc_s, dtype=F32)
    ws_s = jnp.einsum("ab,lgts->lgatbs", eye, blk).reshape(depth, G_A, CHUNK, CHUNK)
    wmix = jnp.stack([ws_p, ws_s], axis=1).astype(BF16)
    dg = d_a // G_A
    bias_p = jnp.repeat(jnp.swapaxes(a_bs, 1, 2), dg, axis=2)
    bias_s = jnp.tile(bias_p[:, :lc_s], (1, CHUNK // lc_s, 1))
    bias = jnp.stack([bias_p, bias_s], axis=1)
    dtb = jnp.pad(b_dt_bias, ((0, 0), (0, LANES - heads))).reshape(depth, 1, LANES)
    alog = jnp.pad(b_a_log, ((0, 0), (0, LANES - heads))).reshape(depth, 1, LANES)
    dsk = jnp.repeat(b_d, P_B, axis=1).reshape(depth, 1, D_INNER)
    cb = b_conv_b.reshape(depth, 1, CONV_DIM)
    ng = b_norm_g.reshape(depth, 1, D_INNER)
    w_gu_all = w_gu.reshape(depth * n_e, d, w_gu.shape[-1])
    b_gu_all = b_gu.reshape(depth * n_e, 1, b_gu.shape[-1])
    w_dn_all = w_dn.reshape(depth * n_e, w_dn.shape[-2], d)
    b_dn_all = b_dn.reshape(depth * n_e, 1, d)
    ssm_in = state_ssm.reshape(depth * bs, G_B, GP, N_B)
    conv_in = state_conv.reshape(depth * bs, CONV_W - 1, state_conv.shape[-1])

    mod_all = _ada(jnp.concatenate([c_prompt, c_sample], axis=0), w_ada, b_ada)
    mod_all = mod_all.reshape(depth, bp + bs, N_MOD, d).transpose(0, 2, 1, 3)
    mod = (mod_all[:, :, :bp], jnp.repeat(mod_all[:, :, bp:], lc_s, axis=2))

    x = jnp.concatenate([x_prompt.reshape(t_p, d), x_sample.reshape(t_s, d)], axis=0)

    h = _comb(tok, x, norm1_g[0], mod, layer_n=0)[0]
    ssm_p, conv_p, conv_s, v_s = [], [], [], []
    ssm_s = None
    y_final = None
    for l in range(depth):
        oa, v_rows = _gmlp(h, w_uv[l], a_ln_g[l], a_ln_b[l], wmix[l], bias[l], t_p)
        ob, cp, sp = _ssd_prompt(h, w_ssd[l], b_conv_w[l], cb[l], dtb[l], alog[l], dsk[l], ng[l], bp, seq)
        proj_s = _proj(h[t_p:], w_ssd[l])
        ob, cs, ssm_s = _ssd_sample(proj_s, t_p, lc_s, b_conv_w[l], cb[l], dtb[l], alog[l], dsk[l], ng[l],
                                    conv_in, ssm_in, l, ob, ssm_s)
        x1, h2, gates, idx, rank, cnt = _mid(tok, l, h, oa, ob, x, mod, norm2_g[l], w_g[l], w_pa[l],
                                             w_pb[l], w_o[l], rw[l], rb[l])
        pos, blk_e, blk_first, n_used, fill0, filln, rows = _route(
            idx[:, :TOP_K], rank[:, :TOP_K], cnt[0, :n_e], n_e, moe_tm, tok.tm)
        xb = _dispatch(h2, pos, fill0, filln, n_used, rows, tm=tok.tm, blk=moe_tm)
        yb = _moe(xb, blk_e, blk_first, n_used, w_gu_all, b_gu_all, w_dn_all, b_dn_all, l * n_e, tm=moe_tm)
        if l + 1 < depth:
            x, h = _comb(tok, x1, norm1_g[l + 1], mod, layer_n=l + 1, yb=yb, pos=pos, gates=gates, layer_gt=l)
        else:
            y_final = _comb(tok, x1, final_g, mod, yb=yb, pos=pos, gates=gates, layer_gt=l)[0]
        ssm_p.append(sp.reshape(bp, heads, P_B, N_B))
        conv_p.append(cp)
        conv_s.append(cs)
        v_s.append(v_rows.reshape(bs, lc_s, d))
    return (y_final[:t_p].reshape(bp, seq, d), y_final[t_p:].reshape(bs, lc_s, d),
            jnp.stack(ssm_p), jnp.stack(conv_p), ssm_s.reshape(depth, bs, heads, P_B, N_B),
            jnp.stack(conv_s), jnp.stack(v_s))
```

```python
import functools

import jax
import jax.numpy as jnp
from jax import lax
from jax.experimental import pallas as pl
from jax.experimental.pallas import tpu as pltpu

F32 = jnp.float32
BF16 = jnp.bfloat16

EPS = 1e-6
SUBLANES = 8
LANES = 128
CHUNK = 128
G_A = 8
G_B = 8
R_B = 4
P_B = 64
N_B = 128
CONV_W = 4
TOP_K = 4
SWIGLU_LIMIT = 7.0
SWIGLU_ALPHA = 1.702
MIB = 2 ** 20
N_MOD = 6
K_SH1, K_SC1, K_GT1, K_SH2, K_SC2, K_GT2 = range(N_MOD)
POS_TOKENS = LANES // TOP_K


def _params(sem, vmem_mib):
    return pltpu.CompilerParams(dimension_semantics=sem, vmem_limit_bytes=vmem_mib * MIB)


def _sigmoid(x):
    return 1.0 / (1.0 + jnp.exp(-x))


def _silu(x):
    return x * _sigmoid(x)


def _gelu(x):
    return 0.5 * x * (1.0 + lax.erf(x * (0.5 ** 0.5)))


def _softplus(x):
    return jnp.maximum(x, 0.0) + jnp.log(1.0 + jnp.exp(-jnp.abs(x)))


def _dot(a, b):
    return jnp.dot(a, b, preferred_element_type=F32)


def _dot_nt(a, b):
    return lax.dot_general(a, b, (((1,), (1,)), ((), ())), preferred_element_type=F32)


class _Tokens:
    def __init__(self, bp, seq, bs, lc_s, tm):
        self.tm = tm
        self.t_p, self.t_s = bp * seq, bs * lc_s
        self.t = self.t_p + self.t_s
        self.n_tiles = self.t // tm
        self.n_p = self.t_p // tm
        self.per_seq = seq // tm
        self.bp = bp

    def mod_specs(self, layer, k, d):
        return [pl.BlockSpec((None, None, self.bp, d), lambda i: (layer, k, 0, 0)),
                pl.BlockSpec((None, None, self.tm, d),
                             lambda i: (layer, k, jnp.maximum(i - self.n_p, 0), 0))]

    def mod_value(self, refs):
        mp_ref, ms_ref = refs
        i = pl.program_id(0)
        seq_row = mp_ref[pl.ds(jnp.minimum(i // self.per_seq, self.bp - 1), 1), :]
        return jnp.where(i < self.n_p, seq_row, ms_ref[...])


def _ada_kernel(c_ref, w_ref, b_ref, o_ref):
    c = c_ref[...]
    o_ref[...] = _dot(_silu(c).astype(BF16), w_ref[...].astype(BF16)) + b_ref[...]


def _ada(c_all, w_ada, b_ada):
    depth, d, n = w_ada.shape
    nc = c_all.shape[0]
    tn = 1536
    return pl.pallas_call(
        _ada_kernel,
        grid=(depth, n // tn),
        in_specs=[pl.BlockSpec((nc, d), lambda l, j: (0, 0)),
                  pl.BlockSpec((None, d, tn), lambda l, j: (l, 0, j)),
                  pl.BlockSpec((None, 1, tn), lambda l, j: (l, 0, j))],
        out_specs=pl.BlockSpec((None, nc, tn), lambda l, j: (l, 0, j)),
        out_shape=jax.ShapeDtypeStruct((depth, nc, n), F32),
        compiler_params=_params(("arbitrary", "arbitrary"), 40),
        name="ada",
    )(c_all, w_ada, b_ada.reshape(depth, 1, n))


def _row_copy(src_ref, src_row, dst_ref, dst_row, sem):
    return pltpu.make_async_copy(src_ref.at[pl.ds(src_row, 1)], dst_ref.at[pl.ds(dst_row, 1)], sem)


def _comb_kernel(*refs, tok, combine, final):
    it = iter(refs)
    if combine:
        pos_ref, posn_ref = next(it), next(it)
    x_ref = next(it)
    if combine:
        yb_ref, g_ref, gt_refs = next(it), next(it), (next(it), next(it))
    n_ref = next(it)
    if not final:
        sc_refs, sh_refs = (next(it), next(it)), (next(it), next(it))
    if combine and not final:
        xo_ref = next(it)
    o_ref = next(it)
    if combine:
        ybuf, sem = next(it), next(it)

    i = pl.program_id(0)
    tm = x_ref.shape[0]

    def gather_row(p_ref, s, j):
        for u in range(POS_TOKENS):
            for k in range(TOP_K):
                _row_copy(yb_ref, p_ref[j, u * TOP_K + k], ybuf.at[s, k], j * POS_TOKENS + u, sem.at[s]).start(
                    priority=k % 2)

    def drain(s):
        for k in range(TOP_K):
            pltpu.make_async_copy(yb_ref.at[pl.ds(0, tm)], ybuf.at[s, k], sem.at[s]).wait()

    def step(slot):
        if combine:
            drain(slot)
            gt = tok.mod_value(gt_refs)
            g = g_ref[...]
        if not final:
            sc, sh = tok.mod_value(sc_refs), tok.mod_value(sh_refs)
        for j in range(tm // POS_TOKENS):
            rows = slice(j * POS_TOKENS, (j + 1) * POS_TOKENS)
            xc = x_ref[rows, :]
            if combine:
                gather_row(posn_ref, 1 - slot, j)
                acc = g[rows, 0:1] * ybuf[slot, 0, rows, :]
                for k in range(1, TOP_K):
                    acc = acc + g[rows, k:k + 1] * ybuf[slot, k, rows, :]
                xc = xc + gt[rows] * acc
                if not final:
                    xo_ref[rows, :] = xc
            ms = jnp.mean(xc * xc, axis=-1, keepdims=True)
            xn = xc * lax.rsqrt(ms + EPS)
            if final:
                o_ref[rows, :] = xn * n_ref[...]
            else:
                o_ref[rows, :] = (xn * n_ref[...] * (1.0 + sc[rows]) + sh[rows]).astype(BF16)
        if combine:
            @pl.when(i == pl.num_programs(0) - 1)
            def _():
                drain(1 - slot)

    if not combine:
        step(0)
        return

    @pl.when(i == 0)
    def _():
        for j in range(tm // POS_TOKENS):
            gather_row(pos_ref, 0, j)

    for slot in range(2):
        pl.when(i % 2 == slot)(functools.partial(step, slot))


def _comb(tok, x, norm_g, mod, layer_n=None, yb=None, pos=None, gates=None, layer_gt=None):
    t, d = x.shape
    tm = tok.tm
    combine = yb is not None
    final = layer_n is None
    row = lambda w: pl.BlockSpec((tm, w), lambda i: (i, 0))

    args, specs, scratch = [], [], []
    if combine:
        args += [pos, pos]
        pos_rows = tm // POS_TOKENS
        specs += [pl.BlockSpec((None, pos_rows, LANES), lambda i: (i, 0, 0), memory_space=pltpu.SMEM),
                  pl.BlockSpec((None, pos_rows, LANES), lambda i: (jnp.minimum(i + 1, tok.n_tiles - 1), 0, 0),
                               memory_space=pltpu.SMEM)]
        scratch = [pltpu.VMEM((2, TOP_K, tm, d), F32), pltpu.SemaphoreType.DMA((2,))]
    args.append(x)
    specs.append(row(d))
    if combine:
        args += [yb, gates, *mod]
        specs += [pl.BlockSpec(memory_space=pl.ANY), row(LANES), *tok.mod_specs(layer_gt, K_GT2, d)]
    args.append(norm_g.reshape(1, d))
    specs.append(pl.BlockSpec((1, d), lambda i: (0, 0)))
    if not final:
        args += [*mod, *mod]
        specs += [*tok.mod_specs(layer_n, K_SC1, d), *tok.mod_specs(layer_n, K_SH1, d)]
    out_shape, out_specs = [], []
    if combine and not final:
        out_shape.append(jax.ShapeDtypeStruct((t, d), F32))
        out_specs.append(row(d))
    out_shape.append(jax.ShapeDtypeStruct((t, d), F32 if final else BF16))
    out_specs.append(row(d))
    return pl.pallas_call(
        functools.partial(_comb_kernel, tok=tok, combine=combine, final=final),
        grid=(tok.n_tiles,),
        in_specs=specs, out_specs=out_specs, out_shape=out_shape,
        scratch_shapes=scratch,
        compiler_params=_params(("arbitrary",), 48),
        name="comb",
    )(*args)


def _gmlp_kernel(h_ref, w_ref, lng_ref, lnb_ref, wmix_ref, bias_ref, o_ref, v_ref, *, n_prompt_tiles):
    i = pl.program_id(0)
    tm, d = o_ref.shape
    uv = _dot(h_ref[...], w_ref[...])
    u = _gelu(uv[:, :d])
    v = _gelu(uv[:, d:])
    mu = jnp.mean(v, axis=-1, keepdims=True)
    vc = v - mu
    var = jnp.mean(vc * vc, axis=-1, keepdims=True)
    vn = vc * lax.rsqrt(var + EPS) * lng_ref[...] + lnb_ref[...]

    @pl.when(i >= n_prompt_tiles)
    def _():
        v_ref[...] = vn

    vb = vn.astype(BF16)
    dg = d // G_A
    for c in range(tm // CHUNK):
        rows = slice(c * CHUNK, (c + 1) * CHUNK)
        for g in range(G_A):
            cols = slice(g * dg, (g + 1) * dg)
            s = _dot(wmix_ref[g], vb[rows, cols]) + bias_ref[:, cols]
            o_ref[rows, cols] = (u[rows, cols] * s).astype(BF16)


def _gmlp(h, w_uv, ln_g, ln_b, wmix, bias, t_prompt, *, tm=512):
    t, d = h.shape
    n_p = t_prompt // tm
    t_s = t - t_prompt
    sel = lambda i: jnp.where(i >= n_p, 1, 0)
    return pl.pallas_call(
        functools.partial(_gmlp_kernel, n_prompt_tiles=n_p),
        grid=(t // tm,),
        in_specs=[pl.BlockSpec((tm, d), lambda i: (i, 0)),
                  pl.BlockSpec((d, 2 * d), lambda i: (0, 0)),
                  pl.BlockSpec((1, d), lambda i: (0, 0)),
                  pl.BlockSpec((1, d), lambda i: (0, 0)),
                  pl.BlockSpec((None, G_A, CHUNK, CHUNK), lambda i: (sel(i), 0, 0, 0)),
                  pl.BlockSpec((None, CHUNK, d), lambda i: (sel(i), 0, 0))],
        out_specs=[pl.BlockSpec((tm, d), lambda i: (i, 0)),
                   pl.BlockSpec((tm, d), lambda i: (jnp.maximum(i - n_p, 0), 0))],
        out_shape=[jax.ShapeDtypeStruct((t, d), BF16),
                   jax.ShapeDtypeStruct((t_s, d), F32)],
        compiler_params=_params(("arbitrary",), 48),
        name="gmlp",
    )(h, w_uv, ln_g.reshape(1, d), ln_b.reshape(1, d), wmix, bias)


def _proj_kernel(x_ref, w_ref, o_ref):
    o_ref[...] = _dot(x_ref[...], w_ref[...])


def _proj(x, w, *, tm=256):
    t, d = x.shape
    n = w.shape[1]
    return pl.pallas_call(
        _proj_kernel,
        grid=(t // tm,),
        in_specs=[pl.BlockSpec((tm, d), lambda i: (i, 0)),
                  pl.BlockSpec((d, n), lambda i: (0, 0))],
        out_specs=pl.BlockSpec((tm, n), lambda i: (i, 0)),
        out_shape=jax.ShapeDtypeStruct((t, n), F32),
        compiler_params=_params(("arbitrary",), 56),
        name="inproj",
    )(x, w)


D_INNER = G_B * R_B * P_B
GN = G_B * N_B
CONV_DIM = D_INNER + 2 * GN
GP = R_B * P_B
HIST = SUBLANES


def _ssd_stage_a(proj_ref, cw_ref, cb_ref, dtb_ref, alog_ref, cs_ref, xact_ref, acum_ref, acumT_ref,
                 dtpT_ref, *, nb, lc):
    r = nb * lc
    cs_ref[:, HIST:HIST + lc, :] = proj_ref[:, D_INNER:D_INNER + CONV_DIM].reshape(nb, lc, CONV_DIM)
    for j in range(CONV_DIM // LANES):
        cols = slice(j * LANES, (j + 1) * LANES)
        full = cs_ref[:, :, cols]
        conv = cb_ref[:, cols] + full[:, HIST:, :] * cw_ref[CONV_W - 1:CONV_W, cols]
        for s in range(1, CONV_W):
            back = pltpu.roll(full, s, axis=1)[:, HIST:, :]
            conv = conv + back * cw_ref[CONV_W - 1 - s:CONV_W - s, cols]
        xact_ref[:, cols] = _silu(conv).reshape(r, LANES)
    tail = cs_ref[:, HIST + lc - 3:HIST + lc, :]

    dt = proj_ref[:, D_INNER + CONV_DIM:D_INNER + CONV_DIM + LANES]
    dtp = _softplus(dt + dtb_ref[...])
    acum = dtp * (-jnp.exp(alog_ref[...]))
    local = lax.broadcasted_iota(jnp.int32, (r, LANES), 0) % lc
    sh = 1
    while sh < lc:
        acum = acum + jnp.where(local >= sh, pltpu.roll(acum, sh, axis=0), 0.0)
        sh *= 2
    acum_ref[...] = acum
    acumT_ref[...] = acum.T
    dtpT_ref[...] = dtp.T
    return tail


def _ssd_heads(g, mask, cb, yoff, xact_ref, acum_ref, acumT_ref, dtpT_ref, dsk_ref):
    r = xact_ref.shape[0]
    first = lax.broadcasted_iota(jnp.int32, (r, LANES), 1) < P_B
    ys = []
    for pair in range(R_B // 2):
        c0 = g * GP + pair * LANES
        xs = xact_ref[:, c0:c0 + LANES]
        xs_bf = xs.astype(BF16)
        colbs, ds = [], []
        for rr in range(2):
            hh = g * R_B + pair * 2 + rr
            colb = jnp.broadcast_to(acum_ref[:, hh:hh + 1], (r, r))
            seg = colb - acumT_ref[hh:hh + 1, :]
            m = cb * jnp.exp(jnp.where(mask, seg, -jnp.inf)) * dtpT_ref[hh:hh + 1, :]
            colbs.append(colb)
            ds.append(_dot(m.astype(BF16), xs_bf))
        y = jnp.where(first, ds[0], ds[1])
        y = y + jnp.exp(jnp.where(first, colbs[0], colbs[1])) * yoff[:, pair * LANES:(pair + 1) * LANES]
        y = y + dsk_ref[:, c0:c0 + LANES] * xs
        ys.append(y)
    return jnp.concatenate(ys, axis=1)


def _ssd_gate_norm(y, z, ng):
    yz = y * _silu(z)
    ms = jnp.mean(yz * yz, axis=-1, keepdims=True)
    return (yz * lax.rsqrt(ms + EPS) * ng).astype(BF16)


def _head_rows(vals):
    return jnp.concatenate([jnp.broadcast_to(v, (P_B, N_B)) for v in vals], axis=0)


def _ssd_prompt_kernel(h_ref, w_ref, cw_ref, cb_ref, dtb_ref, alog_ref, dsk_ref, ng_ref,
                       ob_ref, cout_ref, hout_ref,
                       proj_ref, cs_ref, hs_ref, xact_ref, acum_ref, acumT_ref, dtpT_ref):
    c = pl.program_id(1)
    lc = CHUNK

    @pl.when(c == 0)
    def _():
        cs_ref[:, 0:HIST, :] = jnp.zeros((1, HIST, CONV_DIM), F32)
        hs_ref[...] = jnp.zeros(hs_ref.shape, F32)

    proj_ref[...] = _dot(h_ref[...], w_ref[...])

    tail = _ssd_stage_a(proj_ref, cw_ref, cb_ref, dtb_ref, alog_ref, cs_ref, xact_ref, acum_ref,
                        acumT_ref, dtpT_ref, nb=1, lc=lc)
    cs_ref[:, HIST - 3:HIST, :] = tail

    @pl.when(c == pl.num_programs(1) - 1)
    def _():
        cout_ref[...] = tail[0]

    li = lax.broadcasted_iota(jnp.int32, (lc, lc), 0)
    si = lax.broadcasted_iota(jnp.int32, (lc, lc), 1)
    mask = li >= si
    acum_t = acumT_ref[...]
    wt = dtpT_ref[...] * jnp.exp(acum_t[:, lc - 1:lc] - acum_t)
    for g in range(G_B):
        bg = xact_ref[:, D_INNER + g * N_B:D_INNER + (g + 1) * N_B].astype(BF16)
        cg = xact_ref[:, D_INNER + GN + g * N_B:D_INNER + GN + (g + 1) * N_B].astype(BF16)
        cb = _dot_nt(cg, bg)
        hg = hs_ref[g]
        yoff = _dot_nt(cg, hg.astype(BF16))
        y = _ssd_heads(g, mask, cb, yoff, xact_ref, acum_ref, acumT_ref, dtpT_ref, dsk_ref)
        cols = slice(g * GP, (g + 1) * GP)
        ob_ref[:, cols] = _ssd_gate_norm(y, proj_ref[:, cols], ng_ref[:, cols])
        xs_t = xact_ref[:, cols].T
        xw = jnp.concatenate(
            [xs_t[rr * P_B:(rr + 1) * P_B, :] * wt[g * R_B + rr:g * R_B + rr + 1, :] for rr in range(R_B)],
            axis=0)
        inc = _dot(xw.astype(BF16), bg)
        dec = _head_rows([jnp.exp(acum_t[g * R_B + rr:g * R_B + rr + 1, lc - 1:lc]) for rr in range(R_B)])
        hs_ref[g] = dec * hg + inc

    @pl.when(c == pl.num_programs(1) - 1)
    def _():
        hout_ref[...] = hs_ref[...]


def _ssd_prompt(h, w, cw, cb, dtb, alog, dsk, ng, batch, seq):
    t, d = h.shape
    n = w.shape[1]
    nc = seq // CHUNK
    vec = lambda w: pl.BlockSpec((1, w), lambda b, c: (0, 0))
    return pl.pallas_call(
        _ssd_prompt_kernel,
        grid=(batch, nc),
        in_specs=[pl.BlockSpec((CHUNK, d), lambda b, c: (b * nc + c, 0)),
                  pl.BlockSpec((d, n), lambda b, c: (0, 0)),
                  pl.BlockSpec((CONV_W, CONV_DIM), lambda b, c: (0, 0)),
                  vec(CONV_DIM), vec(LANES), vec(LANES), vec(D_INNER), vec(D_INNER)],
        out_specs=[pl.BlockSpec((CHUNK, D_INNER), lambda b, c: (b * nc + c, 0)),
                   pl.BlockSpec((None, CONV_W - 1, CONV_DIM), lambda b, c: (b, 0, 0)),
                   pl.BlockSpec((None, G_B, GP, N_B), lambda b, c: (b, 0, 0, 0))],
        out_shape=[jax.ShapeDtypeStruct((t, D_INNER), BF16),
                   jax.ShapeDtypeStruct((batch, CONV_W - 1, CONV_DIM), F32),
                   jax.ShapeDtypeStruct((batch, G_B, GP, N_B), F32)],
        scratch_shapes=[pltpu.VMEM((CHUNK, n), F32),
                        pltpu.VMEM((1, HIST + CHUNK, CONV_DIM), F32),
                        pltpu.VMEM((G_B, GP, N_B), F32),
                        pltpu.VMEM((CHUNK, CONV_DIM), F32),
                        pltpu.VMEM((CHUNK, LANES), F32),
                        pltpu.VMEM((LANES, CHUNK), F32),
                        pltpu.VMEM((LANES, CHUNK), F32)],
        compiler_params=_params(("arbitrary", "arbitrary"), 56),
        name="ssd_prompt",
    )(h, w, cw, cb, dtb, alog, dsk, ng)


def _ssd_sample_kernel(*refs, nb, lc, chained):
    (proj_ref, cw_ref, cb_ref, dtb_ref, alog_ref, dsk_ref, ng_ref, cin_ref, hin_ref) = refs[:9]
    n_alias = 2 if chained else 1
    (ob_ref, cout_ref, hout_ref,
     cs_ref, xact_ref, acum_ref, acumT_ref, dtpT_ref, yoff_ref) = refs[9 + n_alias:]
    g = pl.program_id(1)
    r = nb * lc

    @pl.when(g == 0)
    def _():
        cs_ref[:, HIST - 3:HIST, :] = cin_ref[...]
        tail = _ssd_stage_a(proj_ref, cw_ref, cb_ref, dtb_ref, alog_ref, cs_ref, xact_ref, acum_ref,
                            acumT_ref, dtpT_ref, nb=nb, lc=lc)
        cout_ref[...] = tail

    li = lax.broadcasted_iota(jnp.int32, (r, r), 0)
    si = lax.broadcasted_iota(jnp.int32, (r, r), 1)
    mask = (li >= si) & ((li // lc) == (si // lc))
    lane_seq = lax.broadcasted_iota(jnp.int32, (GP, r), 1) // lc
    acum_t = acumT_ref[...]
    dtp_t = dtpT_ref[...]

    def branch(gg):
        bg = xact_ref[:, D_INNER + gg * N_B:D_INNER + (gg + 1) * N_B].astype(BF16)
        cg = xact_ref[:, D_INNER + GN + gg * N_B:D_INNER + GN + (gg + 1) * N_B].astype(BF16)
        cb = _dot_nt(cg, bg)
        cols = slice(gg * GP, (gg + 1) * GP)
        xs_t = xact_ref[:, cols].T
        win = 2 * SUBLANES
        for b in range(nb):
            h0 = hin_ref[b, 0]
            w0 = (b * lc // win) * win
            yo = _dot_nt(cg[w0:w0 + win], h0.astype(BF16))
            yoff_ref[b * lc:(b + 1) * lc, :] = yo[b * lc - w0:b * lc - w0 + lc]
            last = b * lc + lc - 1
            xw = jnp.concatenate(
                [xs_t[rr * P_B:(rr + 1) * P_B, :]
                 * (dtp_t[gg * R_B + rr:gg * R_B + rr + 1, :]
                    * jnp.exp(acum_t[gg * R_B + rr:gg * R_B + rr + 1, last:last + 1]
                              - acum_t[gg * R_B + rr:gg * R_B + rr + 1, :]))
                 for rr in range(R_B)], axis=0)
            xw = jnp.where(lane_seq == b, xw, 0.0)
            inc = _dot(xw.astype(BF16), bg)
            dec = _head_rows([jnp.exp(acum_t[gg * R_B + rr:gg * R_B + rr + 1, last:last + 1])
                              for rr in range(R_B)])
            hout_ref[b, 0] = dec * h0 + inc
        y = _ssd_heads(gg, mask, cb, yoff_ref[...], xact_ref, acum_ref, acumT_ref, dtpT_ref, dsk_ref)
        ob_ref[...] = _ssd_gate_norm(y, proj_ref[:, cols], ng_ref[:, cols])

    for gg in range(G_B):
        pl.when(g == gg)(functools.partial(branch, gg))


def _ssd_sample(proj, row0, lc, cw, cb, dtb, alog, dsk, ng, conv_state, ssm_state, layer, ob, ssm_out,
                *, nb=16):
    n = proj.shape[1]
    batch = proj.shape[0] // lc
    r = nb * lc
    blk0 = row0 // r
    sb0 = layer * (batch // nb)
    chained = ssm_out is not None
    vec = lambda w: pl.BlockSpec((1, w), lambda i, g: (0, 0))
    state_spec = pl.BlockSpec((nb, 1, GP, N_B), lambda i, g: (sb0 + i, g, 0, 0))
    hbm = pl.BlockSpec(memory_space=pl.ANY)
    args = [proj, cw, cb, dtb, alog, dsk, ng, conv_state, ssm_state, ob] + ([ssm_out] if chained else [])
    aliases = {9: 0, 10: 2} if chained else {9: 0}
    return pl.pallas_call(
        functools.partial(_ssd_sample_kernel, nb=nb, lc=lc, chained=chained),
        grid=(batch // nb, G_B),
        in_specs=[pl.BlockSpec((r, n), lambda i, g: (i, 0)),
                  pl.BlockSpec((CONV_W, CONV_DIM), lambda i, g: (0, 0)),
                  vec(CONV_DIM), vec(LANES), vec(LANES), vec(D_INNER), vec(D_INNER),
                  pl.BlockSpec((nb, CONV_W - 1, CONV_DIM), lambda i, g: (sb0 + i, 0, 0)),
                  state_spec, hbm] + ([hbm] if chained else []),
        out_specs=[pl.BlockSpec((r, GP), lambda i, g: (blk0 + i, g)),
                   pl.BlockSpec((nb, CONV_W - 1, CONV_DIM), lambda i, g: (i, 0, 0)),
                   state_spec],
        out_shape=[jax.ShapeDtypeStruct(ob.shape, BF16),
                   jax.ShapeDtypeStruct((batch, CONV_W - 1, CONV_DIM), F32),
                   jax.ShapeDtypeStruct(ssm_state.shape, F32)],
        input_output_aliases=aliases,
        scratch_shapes=[pltpu.VMEM((nb, HIST + lc, CONV_DIM), F32),
                        pltpu.VMEM((r, CONV_DIM), F32),
                        pltpu.VMEM((r, LANES), F32),
                        pltpu.VMEM((LANES, r), F32),
                        pltpu.VMEM((LANES, r), F32),
                        pltpu.VMEM((r, GP), F32)],
        compiler_params=_params(("arbitrary", "arbitrary"), 48),
        name="ssd_sample",
    )(*args)


def _mid_kernel(h_ref, oa_ref, ob_ref, x_ref, gtp_ref, gts_ref, scp_ref, scs_ref, shp_ref, shs_ref, n2_ref,
                wg_ref, wpa_ref, wpb_ref, wo_ref, rw_ref, rb_ref,
                x1_ref, h2_ref, gates_ref, idx_ref, rank_ref, cnt_ref, carry_ref, *, tok):
    tm, d = h2_ref.shape
    gab = _dot(h_ref[...], wg_ref[...])
    m = (_sigmoid(gab[:, :d]) * _dot(oa_ref[...], wpa_ref[...])
         + _sigmoid(gab[:, d:]) * _dot(ob_ref[...], wpb_ref[...]))
    x1 = x_ref[...] + tok.mod_value((gtp_ref, gts_ref)) * _dot(m.astype(BF16), wo_ref[...])
    x1_ref[...] = x1
    ms = jnp.mean(x1 * x1, axis=-1, keepdims=True)
    h2 = (x1 * lax.rsqrt(ms + EPS) * n2_ref[...] * (1.0 + tok.mod_value((scp_ref, scs_ref)))
          + tok.mod_value((shp_ref, shs_ref)))
    h2_ref[...] = h2

    logits = _dot(h2.astype(BF16), rw_ref[...]) + rb_ref[...]
    lane = lax.broadcasted_iota(jnp.int32, logits.shape, 1).astype(F32)
    vals, idxs = [], []
    for _ in range(TOP_K):
        mx = jnp.max(logits, axis=-1, keepdims=True)
        ix = jnp.min(jnp.where(logits == mx, lane, float(LANES)), axis=-1, keepdims=True)
        vals.append(mx)
        idxs.append(ix)
        logits = jnp.where(lane == ix, -jnp.inf, logits)
    es = [jnp.exp(v - vals[0]) for v in vals]
    tot = es[0] + es[1] + es[2] + es[3]
    gates = jnp.zeros(logits.shape, F32)
    idx = jnp.zeros(logits.shape, F32)
    for k in range(TOP_K):
        gates = jnp.where(lane == float(k), es[k] / tot, gates)
        idx = jnp.where(lane == float(k), idxs[k], idx)
    gates_ref[...] = gates
    idx_ref[...] = idx.astype(jnp.int32)

    @pl.when(pl.program_id(0) == 0)
    def _():
        carry_ref[...] = jnp.zeros(carry_ref.shape, F32)

    earlier = (lax.broadcasted_iota(jnp.int32, (tm, tm), 1)
               < lax.broadcasted_iota(jnp.int32, (tm, tm), 0)).astype(BF16)
    onehots = [(lane == idxs[k]).astype(F32) for k in range(TOP_K)]
    prefix = _dot(earlier, jnp.concatenate(onehots, axis=1).astype(BF16))
    base = carry_ref[...]
    rank = jnp.zeros(logits.shape, F32)
    for k in range(TOP_K):
        before = prefix[:, k * LANES:(k + 1) * LANES] + base
        rank = jnp.where(lane == float(k), jnp.sum(onehots[k] * before, axis=-1, keepdims=True), rank)
        base = base + jnp.sum(onehots[k], axis=0, keepdims=True)
    carry_ref[...] = base
    rank_ref[...] = rank.astype(jnp.int32)
    cnt_ref[...] = base.astype(jnp.int32)


def _mid(tok, layer, h, oa, ob, x, mod, n2, wg, wpa, wpb, wo, rw, rb):
    t, d = h.shape
    tm = tok.tm
    row = lambda w: pl.BlockSpec((tm, w), lambda i: (i, 0))
    full = lambda a: pl.BlockSpec(a.shape, lambda i: (0,) * a.ndim)
    n2 = n2.reshape(1, d)
    return pl.pallas_call(
        functools.partial(_mid_kernel, tok=tok),
        grid=(tok.n_tiles,),
        in_specs=[row(d), row(d), row(2 * d), row(d),
                  *tok.mod_specs(layer, K_GT1, d), *tok.mod_specs(layer, K_SC2, d),
                  *tok.mod_specs(layer, K_SH2, d),
                  full(n2), full(wg), full(wpa), full(wpb), full(wo), full(rw), full(rb)],
        out_specs=[row(d), row(d), row(LANES), row(LANES), row(LANES),
                   pl.BlockSpec((1, LANES), lambda i: (0, 0))],
        out_shape=[jax.ShapeDtypeStruct((t, d), F32),
                   jax.ShapeDtypeStruct((t, d), F32),
                   jax.ShapeDtypeStruct((t, LANES), F32),
                   jax.ShapeDtypeStruct((t, LANES), jnp.int32),
                   jax.ShapeDtypeStruct((t, LANES), jnp.int32),
                   jax.ShapeDtypeStruct((1, LANES), jnp.int32)],
        scratch_shapes=[pltpu.VMEM((1, LANES), F32)],
        compiler_params=_params(("arbitrary",), 56),
        name="mid",
    )(h, oa, ob, x, *mod, *mod, *mod, n2, wg, wpa, wpb, wo, rw, rb)


def _dispatch_kernel(fill0_ref, filln_ref, n_used_ref, pos_ref, h_ref, xb_ref, zrow, sem, zsem):
    i = pl.program_id(0)
    tm = h_ref.shape[0]
    n_e = filln_ref.shape[0]
    blk = zrow.shape[0]
    n_blocks = xb_ref.shape[0] // blk

    for t in range(tm):
        for k in range(TOP_K):
            _row_copy(h_ref, t, xb_ref, pos_ref[t // POS_TOKENS, (t % POS_TOKENS) * TOP_K + k], sem).start(
                priority=k % 2)

    @pl.when(i == pl.num_programs(0) - 1)
    def _():
        zrow[...] = jnp.zeros(zrow.shape, F32)

        def fill_expert(e, c):
            def fill(r, c2):
                _row_copy(zrow, 0, xb_ref, fill0_ref[e] + r, zsem).start()
                return c2
            return lax.fori_loop(0, filln_ref[e], fill, c)
        lax.fori_loop(0, n_e, fill_expert, 0)

        def drain_expert(e, c):
            def drain(r, c2):
                _row_copy(zrow, 0, xb_ref, 0, zsem).wait()
                return c2
            return lax.fori_loop(0, filln_ref[e], drain, c)
        lax.fori_loop(0, n_e, drain_expert, 0)

        def block_copy(b):
            return pltpu.make_async_copy(zrow, xb_ref.at[pl.ds(pl.multiple_of(b * blk, blk), blk)], zsem)

        def fill_block(b, c):
            block_copy(b).start()
            return c
        lax.fori_loop(n_used_ref[0], n_blocks, fill_block, 0)

        def drain_block(b, c):
            block_copy(b).wait()
            return c
        lax.fori_loop(n_used_ref[0], n_blocks, drain_block, 0)

    for k in range(TOP_K):
        pltpu.make_async_copy(h_ref, xb_ref.at[pl.ds(0, tm)], sem).wait()


def _dispatch(h2, pos, fill0, filln, n_used, rows, *, tm, blk):
    t, d = h2.shape
    grid_spec = pltpu.PrefetchScalarGridSpec(
        num_scalar_prefetch=3,
        grid=(t // tm,),
        in_specs=[pl.BlockSpec((None, tm // POS_TOKENS, LANES), lambda i, f0, fn, nu: (i, 0, 0),
                               memory_space=pltpu.SMEM),
                  pl.BlockSpec((tm, d), lambda i, f0, fn, nu: (i, 0))],
        out_specs=pl.BlockSpec(memory_space=pl.ANY),
        scratch_shapes=[pltpu.VMEM((blk, d), F32), pltpu.SemaphoreType.DMA(()),
                        pltpu.SemaphoreType.DMA(())])
    return pl.pallas_call(
        _dispatch_kernel,
        grid_spec=grid_spec,
        out_shape=jax.ShapeDtypeStruct((rows, d), F32),
        compiler_params=_params(("arbitrary",), 32),
        name="dispatch",
    )(fill0, filln, n_used, pos, h2)


def _moe_kernel(blk_e_ref, blk_first_ref, n_used_ref, x_ref, wgu_ref, bgu_ref, wdn_ref, bdn_ref, o_ref,
                wgu_bf, wdn_bf):
    i = pl.program_id(0)
    d_ff = wdn_bf.shape[0]

    @pl.when(blk_first_ref[i] == 1)
    def _():
        wgu_bf[...] = wgu_ref[...].astype(BF16)
        wdn_bf[...] = wdn_ref[...].astype(BF16)

    @pl.when(i < n_used_ref[0])
    def _():
        gu = _dot(x_ref[...].astype(BF16), wgu_bf[...]) + bgu_ref[...]
        g = jnp.minimum(gu[:, :d_ff], SWIGLU_LIMIT)
        u = jnp.clip(gu[:, d_ff:], -SWIGLU_LIMIT, SWIGLU_LIMIT)
        act = g * _sigmoid(SWIGLU_ALPHA * g) * (u + 1.0)
        o_ref[...] = _dot(act.astype(BF16), wdn_bf[...]) + bdn_ref[...]

    @pl.when(i >= n_used_ref[0])
    def _():
        o_ref[...] = jnp.zeros(o_ref.shape, F32)


def _moe(xb, blk_e, blk_first, n_used, w_gu, b_gu, w_dn, b_dn, e0, *, tm):
    rows, d = xb.shape
    _, _, d_gu = w_gu.shape
    d_ff = w_dn.shape[1]
    expert = lambda i, be, bf, nu: (e0 + be[i], 0, 0)
    grid_spec = pltpu.PrefetchScalarGridSpec(
        num_scalar_prefetch=3,
        grid=(rows // tm,),
        in_specs=[pl.BlockSpec((tm, d), lambda i, be, bf, nu: (jnp.maximum(jnp.minimum(i, nu[0] - 1), 0), 0)),
                  pl.BlockSpec((None, d, d_gu), expert),
                  pl.BlockSpec((None, 1, d_gu), expert),
                  pl.BlockSpec((None, d_ff, d), expert),
                  pl.BlockSpec((None, 1, d), expert)],
        out_specs=pl.BlockSpec((tm, d), lambda i, be, bf, nu: (i, 0)),
        scratch_shapes=[pltpu.VMEM((d, d_gu), BF16), pltpu.VMEM((d_ff, d), BF16)])
    return pl.pallas_call(
        _moe_kernel,
        grid_spec=grid_spec,
        out_shape=jax.ShapeDtypeStruct((rows, d), F32),
        compiler_params=_params(("arbitrary",), 56),
        name="moe",
    )(blk_e, blk_first, n_used, xb, w_gu, b_gu, w_dn, b_dn)


def _route(idx, rank, counts, n_e, tm, tm_tok):
    t = idx.shape[0]
    padded = (counts + tm - 1) // tm * tm
    pad_end = jnp.cumsum(padded)
    pad_start = pad_end - padded
    experts = jnp.arange(n_e, dtype=jnp.int32)
    pos = jnp.sum(jnp.where(idx[..., None] == experts, pad_start, 0), axis=-1) + rank
    pos = pos.reshape(t // tm_tok, tm_tok // POS_TOKENS, LANES).astype(jnp.int32)
    n_blocks = -(-t * TOP_K // tm) + n_e
    blk_start = jnp.arange(n_blocks, dtype=jnp.int32) * tm
    blk_e = jnp.minimum(jnp.sum((blk_start[:, None] >= pad_end[None, :]).astype(jnp.int32), axis=1),
                        n_e - 1).astype(jnp.int32)
    blk_first = jnp.concatenate([jnp.ones((1,), jnp.int32),
                                 (blk_e[1:] != blk_e[:-1]).astype(jnp.int32)])
    n_used = (pad_end[-1:] // tm).astype(jnp.int32)
    return (pos, blk_e, blk_first, n_used, (pad_start + counts).astype(jnp.int32),
            (padded - counts).astype(jnp.int32), n_blocks * tm)


def kernel(x_prompt, x_sample, c_prompt, c_sample, state_ssm, state_conv, norm1_g, norm2_g, final_g,
           w_ada, b_ada, w_in, a_ln_g, a_ln_b, a_ws, a_bs, b_conv_w, b_conv_b, b_dt_bias, b_a_log, b_d,
           b_norm_g, w_proj_a, w_proj_b, w_out, router_w, router_b, w_gu, b_gu, w_dn, b_dn):
    bp, seq, d = x_prompt.shape
    bs, lc_s, _ = x_sample.shape
    depth = w_ada.shape[0]
    n_e = router_w.shape[-1]
    heads = b_d.shape[-1]
    moe_tm = 256
    tok = _Tokens(bp, seq, bs, lc_s, tm=256)
    t_p, t_s, t = tok.t_p, tok.t_s, tok.t

    d_a = d
    c0 = 2 * d_a
    c1 = c0 + D_INNER + CONV_DIM + heads
    w_in_bf = w_in.astype(BF16)
    w_uv = w_in_bf[:, :, :c0]
    w_ssd = jnp.pad(w_in_bf[:, :, c0:c1], ((0, 0), (0, 0), (0, LANES - heads)))
    w_g = w_in_bf[:, :, c1:]
    w_pa, w_pb, w_o = w_proj_a.astype(BF16), w_proj_b.astype(BF16), w_out.astype(BF16)
    rw = jnp.pad(router_w, ((0, 0), (0, 0), (0, LANES - n_e))).astype(BF16)
    rb = jnp.pad(router_b, ((0, 0), (0, LANES - n_e)), constant_values=-jnp.inf).reshape(depth, 1, LANES)
    causal = jnp.tril(jnp.ones((CHUNK, CHUNK), bool))
    ws_p = jnp.where(causal, a_ws, 0.0)
    blk = jnp.where(jnp.tril(jnp.ones((lc_s, lc_s), bool)), a_ws[:, :, :lc_s, :lc_s], 0.0)
    eye = jnp.eye(CHUNK // lc_s, dtype=F32)
    ws_s = jnp.einsum("ab,lgts->lgatbs", eye, blk).reshape(depth, G_A, CHUNK, CHUNK)
    wmix = jnp.stack([ws_p, ws_s], axis=1).astype(BF16)
    dg = d_a // G_A
    bias_p = jnp.repeat(jnp.swapaxes(a_bs, 1, 2), dg, axis=2)
    bias_s = jnp.tile(bias_p[:, :lc_s], (1, CHUNK // lc_s, 1))
    bias = jnp.stack([bias_p, bias_s], axis=1)
    dtb = jnp.pad(b_dt_bias, ((0, 0), (0, LANES - heads))).reshape(depth, 1, LANES)
    alog = jnp.pad(b_a_log, ((0, 0), (0, LANES - heads))).reshape(depth, 1, LANES)
    dsk = jnp.repeat(b_d, P_B, axis=1).reshape(depth, 1, D_INNER)
    cb = b_conv_b.reshape(depth, 1, CONV_DIM)
    ng = b_norm_g.reshape(depth, 1, D_INNER)
    w_gu_all = w_gu.reshape(depth * n_e, d, w_gu.shape[-1])
    b_gu_all = b_gu.reshape(depth * n_e, 1, b_gu.shape[-1])
    w_dn_all = w_dn.reshape(depth * n_e, w_dn.shape[-2], d)
    b_dn_all = b_dn.reshape(depth * n_e, 1, d)
    ssm_in = state_ssm.reshape(depth * bs, G_B, GP, N_B)
    conv_in = state_conv.reshape(depth * bs, CONV_W - 1, state_conv.shape[-1])

    mod_all = _ada(jnp.concatenate([c_prompt, c_sample], axis=0), w_ada, b_ada)
    mod_all = mod_all.reshape(depth, bp + bs, N_MOD, d).transpose(0, 2, 1, 3)
    mod = (mod_all[:, :, :bp], jnp.repeat(mod_all[:, :, bp:], lc_s, axis=2))

    x = jnp.concatenate([x_prompt.reshape(t_p, d), x_sample.reshape(t_s, d)], axis=0)

    h = _comb(tok, x, norm1_g[0], mod, layer_n=0)[0]
    ssm_p, conv_p, conv_s, v_s = [], [], [], []
    ssm_s = None
    y_final = None
    for l in range(depth):
        oa, v_rows = _gmlp(h, w_uv[l], a_ln_g[l], a_ln_b[l], wmix[l], bias[l], t_p)
        ob, cp, sp = _ssd_prompt(h, w_ssd[l], b_conv_w[l], cb[l], dtb[l], alog[l], dsk[l], ng[l], bp, seq)
        proj_s = _proj(h[t_p:], w_ssd[l])
        ob, cs, ssm_s = _ssd_sample(proj_s, t_p, lc_s, b_conv_w[l], cb[l], dtb[l], alog[l], dsk[l], ng[l],
                                    conv_in, ssm_in, l, ob, ssm_s)
        x1, h2, gates, idx, rank, cnt = _mid(tok, l, h, oa, ob, x, mod, norm2_g[l], w_g[l], w_pa[l],
                                             w_pb[l], w_o[l], rw[l], rb[l])
        pos, blk_e, blk_first, n_used, fill0, filln, rows = _route(
            idx[:, :TOP_K], rank[:, :TOP_K], cnt[0, :n_e], n_e, moe_tm, tok.tm)
        xb = _dispatch(h2, pos, fill0, filln, n_used, rows, tm=tok.tm, blk=moe_tm)
        yb = _moe(xb, blk_e, blk_first, n_used, w_gu_all, b_gu_all, w_dn_all, b_dn_all, l * n_e, tm=moe_tm)
        if l + 1 < depth:
            x, h = _comb(tok, x1, norm1_g[l + 1], mod, layer_n=l + 1, yb=yb, pos=pos, gates=gates, layer_gt=l)
        else:
            y_final = _comb(tok, x1, final_g, mod, yb=yb, pos=pos, gates=gates, layer_gt=l)[0]
        ssm_p.append(sp.reshape(bp, heads, P_B, N_B))
        conv_p.append(cp)
        conv_s.append(cs)
        v_s.append(v_rows.reshape(bs, lc_s, d))
    return (y_final[:t_p].reshape(bp, seq, d), y_final[t_p:].reshape(bs, lc_s, d),
            jnp.stack(ssm_p), jnp.stack(conv_p), ssm_s.reshape(depth, bs, heads, P_B, N_B),
            jnp.stack(conv_s), jnp.stack(v_s))
```

```python
import functools

import jax
import jax.numpy as jnp
from jax import lax
from jax.experimental import pallas as pl
from jax.experimental.pallas import tpu as pltpu

F32 = jnp.float32
BF16 = jnp.bfloat16

EPS = 1e-6
SUBLANES = 8
LANES = 128
CHUNK = 128
G_A = 8
G_B = 8
R_B = 4
P_B = 64
N_B = 128
CONV_W = 4
TOP_K = 4
SWIGLU_LIMIT = 7.0
SWIGLU_ALPHA = 1.702
MIB = 2 ** 20
N_MOD = 6
K_SH1, K_SC1, K_GT1, K_SH2, K_SC2, K_GT2 = range(N_MOD)
POS_TOKENS = LANES // TOP_K
MID_ROWS = 256
LOG2E = 1.4426950408889634


def _params(sem, vmem_mib):
    return pltpu.CompilerParams(dimension_semantics=sem, vmem_limit_bytes=vmem_mib * MIB)


def _sigmoid(x):
    return 1.0 / (1.0 + jnp.exp(-x))


def _silu(x):
    return x * _sigmoid(x)


def _gelu(x):
    return 0.5 * x * (1.0 + lax.erf(x * (0.5 ** 0.5)))


def _softplus(x):
    return jnp.maximum(x, 0.0) + jnp.log(1.0 + jnp.exp(-jnp.abs(x)))


def _dot(a, b):
    return jnp.dot(a, b, preferred_element_type=F32)


def _dot_nt(a, b):
    return lax.dot_general(a, b, (((1,), (1,)), ((), ())), preferred_element_type=F32)


class _Tokens:
    def __init__(self, bp, seq, bs, lc_s, tm):
        self.tm = tm
        self.t_p, self.t_s = bp * seq, bs * lc_s
        self.t = self.t_p + self.t_s
        self.n_tiles = self.t // tm
        self.n_p = self.t_p // tm
        self.per_seq = seq // tm
        self.bp = bp

    def mod_specs(self, layer, k, d):
        return [pl.BlockSpec((None, None, self.bp, d), lambda i: (layer, k, 0, 0)),
                pl.BlockSpec((None, None, self.tm, d),
                             lambda i: (layer, k, jnp.maximum(i - self.n_p, 0), 0))]

    def mod_value(self, refs):
        mp_ref, ms_ref = refs
        i = pl.program_id(0)
        seq_row = mp_ref[pl.ds(jnp.minimum(i // self.per_seq, self.bp - 1), 1), :]
        return jnp.where(i < self.n_p, seq_row, ms_ref[...])


def _ada_kernel(c_ref, w_ref, b_ref, o_ref):
    c = c_ref[...]
    o_ref[...] = _dot(_silu(c).astype(BF16), w_ref[...].astype(BF16)) + b_ref[...]


def _ada(c_all, w_ada, b_ada):
    depth, d, n = w_ada.shape
    nc = c_all.shape[0]
    tn = 1536
    return pl.pallas_call(
        _ada_kernel,
        grid=(depth, n // tn),
        in_specs=[pl.BlockSpec((nc, d), lambda l, j: (0, 0)),
                  pl.BlockSpec((None, d, tn), lambda l, j: (l, 0, j)),
                  pl.BlockSpec((None, 1, tn), lambda l, j: (l, 0, j))],
        out_specs=pl.BlockSpec((None, nc, tn), lambda l, j: (l, 0, j)),
        out_shape=jax.ShapeDtypeStruct((depth, nc, n), F32),
        compiler_params=_params(("arbitrary", "arbitrary"), 40),
        name="ada",
    )(c_all, w_ada, b_ada.reshape(depth, 1, n))


def _row_copy(src_ref, src_row, dst_ref, dst_row, sem):
    return pltpu.make_async_copy(src_ref.at[pl.ds(src_row, 1)], dst_ref.at[pl.ds(dst_row, 1)], sem)


def _comb_kernel(*refs, tok, combine, final):
    it = iter(refs)
    if combine:
        pos_ref, posn_ref = next(it), next(it)
    x_ref = next(it)
    if combine:
        yb_ref, g_ref, gt_refs = next(it), next(it), (next(it), next(it))
    n_ref = next(it)
    if not final:
        sc_refs, sh_refs = (next(it), next(it)), (next(it), next(it))
    if combine and not final:
        xo_ref = next(it)
    o_ref = next(it)
    if combine:
        ybuf, sem = next(it), next(it)

    i = pl.program_id(0)
    tm = x_ref.shape[0]

    def gather_row(p_ref, s, j):
        for u in range(POS_TOKENS):
            for k in range(TOP_K):
                _row_copy(yb_ref, p_ref[j, u * TOP_K + k], ybuf.at[s, k], j * POS_TOKENS + u, sem.at[s]).start(
                    priority=k % 2)

    def drain(s):
        for k in range(TOP_K):
            pltpu.make_async_copy(yb_ref.at[pl.ds(0, tm)], ybuf.at[s, k], sem.at[s]).wait()

    def step(slot):
        if combine:
            drain(slot)
            gt = tok.mod_value(gt_refs)
            g = g_ref[...]
        if not final:
            sc, sh = tok.mod_value(sc_refs), tok.mod_value(sh_refs)
        for j in range(tm // POS_TOKENS):
            rows = slice(j * POS_TOKENS, (j + 1) * POS_TOKENS)
            xc = x_ref[rows, :]
            if combine:
                gather_row(posn_ref, 1 - slot, j)
                acc = g[rows, 0:1] * ybuf[slot, 0, rows, :]
                for k in range(1, TOP_K):
                    acc = acc + g[rows, k:k + 1] * ybuf[slot, k, rows, :]
                xc = xc + gt[rows] * acc
                if not final:
                    xo_ref[rows, :] = xc
            ms = jnp.mean(xc * xc, axis=-1, keepdims=True)
            xn = xc * lax.rsqrt(ms + EPS)
            if final:
                o_ref[rows, :] = xn * n_ref[...]
            else:
                o_ref[rows, :] = (xn * n_ref[...] * (1.0 + sc[rows]) + sh[rows]).astype(BF16)
        if combine:
            @pl.when(i == pl.num_programs(0) - 1)
            def _():
                drain(1 - slot)

    if not combine:
        step(0)
        return

    @pl.when(i == 0)
    def _():
        for j in range(tm // POS_TOKENS):
            gather_row(pos_ref, 0, j)

    for slot in range(2):
        pl.when(i % 2 == slot)(functools.partial(step, slot))


def _comb(tok, x, norm_g, mod, layer_n=None, yb=None, pos=None, gates=None, layer_gt=None):
    t, d = x.shape
    tm = tok.tm
    combine = yb is not None
    final = layer_n is None
    row = lambda w: pl.BlockSpec((tm, w), lambda i: (i, 0))

    args, specs, scratch = [], [], []
    if combine:
        args += [pos, pos]
        pos_rows = tm // POS_TOKENS
        specs += [pl.BlockSpec((None, pos_rows, LANES), lambda i: (i, 0, 0), memory_space=pltpu.SMEM),
                  pl.BlockSpec((None, pos_rows, LANES), lambda i: (jnp.minimum(i + 1, tok.n_tiles - 1), 0, 0),
                               memory_space=pltpu.SMEM)]
        scratch = [pltpu.VMEM((2, TOP_K, tm, d), F32), pltpu.SemaphoreType.DMA((2,))]
    args.append(x)
    specs.append(row(d))
    if combine:
        args += [yb, gates, *mod]
        specs += [pl.BlockSpec(memory_space=pl.ANY), row(LANES), *tok.mod_specs(layer_gt, K_GT2, d)]
    args.append(norm_g.reshape(1, d))
    specs.append(pl.BlockSpec((1, d), lambda i: (0, 0)))
    if not final:
        args += [*mod, *mod]
        specs += [*tok.mod_specs(layer_n, K_SC1, d), *tok.mod_specs(layer_n, K_SH1, d)]
    out_shape, out_specs = [], []
    if combine and not final:
        out_shape.append(jax.ShapeDtypeStruct((t, d), F32))
        out_specs.append(row(d))
    out_shape.append(jax.ShapeDtypeStruct((t, d), F32 if final else BF16))
    out_specs.append(row(d))
    return pl.pallas_call(
        functools.partial(_comb_kernel, tok=tok, combine=combine, final=final),
        grid=(tok.n_tiles,),
        in_specs=specs, out_specs=out_specs, out_shape=out_shape,
        scratch_shapes=scratch,
        compiler_params=_params(("arbitrary",), 48),
        name="comb",
    )(*args)


def _gmlp_kernel(h_ref, w_ref, lng_ref, lnb_ref, wmix_ref, bias_ref, o_ref, v_ref, *, n_prompt_tiles):
    i = pl.program_id(0)
    tm, d = o_ref.shape
    uv = _dot(h_ref[...], w_ref[...])
    u = _gelu(uv[:, :d])
    v = _gelu(uv[:, d:])
    mu = jnp.mean(v, axis=-1, keepdims=True)
    vc = v - mu
    var = jnp.mean(vc * vc, axis=-1, keepdims=True)
    vn = vc * lax.rsqrt(var + EPS) * lng_ref[...] + lnb_ref[...]

    @pl.when(i >= n_prompt_tiles)
    def _():
        v_ref[...] = vn

    vb = vn.astype(BF16)
    dg = d // G_A
    for c in range(tm // CHUNK):
        rows = slice(c * CHUNK, (c + 1) * CHUNK)
        for g in range(G_A):
            cols = slice(g * dg, (g + 1) * dg)
            s = _dot(wmix_ref[g], vb[rows, cols]) + bias_ref[:, cols]
            o_ref[rows, cols] = (u[rows, cols] * s).astype(BF16)


def _gmlp(h, w_uv, ln_g, ln_b, wmix, bias, t_prompt, *, tm=512):
    t, d = h.shape
    n_p = t_prompt // tm
    t_s = t - t_prompt
    sel = lambda i: jnp.where(i >= n_p, 1, 0)
    return pl.pallas_call(
        functools.partial(_gmlp_kernel, n_prompt_tiles=n_p),
        grid=(t // tm,),
        in_specs=[pl.BlockSpec((tm, d), lambda i: (i, 0)),
                  pl.BlockSpec((d, 2 * d), lambda i: (0, 0)),
                  pl.BlockSpec((1, d), lambda i: (0, 0)),
                  pl.BlockSpec((1, d), lambda i: (0, 0)),
                  pl.BlockSpec((None, G_A, CHUNK, CHUNK), lambda i: (sel(i), 0, 0, 0)),
                  pl.BlockSpec((None, CHUNK, d), lambda i: (sel(i), 0, 0))],
        out_specs=[pl.BlockSpec((tm, d), lambda i: (i, 0)),
                   pl.BlockSpec((tm, d), lambda i: (jnp.maximum(i - n_p, 0), 0))],
        out_shape=[jax.ShapeDtypeStruct((t, d), BF16),
                   jax.ShapeDtypeStruct((t_s, d), F32)],
        compiler_params=_params(("arbitrary",), 48),
        name="gmlp",
    )(h, w_uv, ln_g.reshape(1, d), ln_b.reshape(1, d), wmix, bias)


def _proj_kernel(x_ref, w_ref, o_ref):
    o_ref[...] = _dot(x_ref[...], w_ref[...])


def _proj(x, w, *, tm=256):
    t, d = x.shape
    n = w.shape[1]
    return pl.pallas_call(
        _proj_kernel,
        grid=(t // tm,),
        in_specs=[pl.BlockSpec((tm, d), lambda i: (i, 0)),
                  pl.BlockSpec((d, n), lambda i: (0, 0))],
        out_specs=pl.BlockSpec((tm, n), lambda i: (i, 0)),
        out_shape=jax.ShapeDtypeStruct((t, n), F32),
        compiler_params=_params(("arbitrary",), 56),
        name="inproj",
    )(x, w)


D_INNER = G_B * R_B * P_B
GN = G_B * N_B
CONV_DIM = D_INNER + 2 * GN
GP = R_B * P_B
HIST = SUBLANES


def _ssd_stage_a(proj_ref, cw_ref, cb_ref, dtb_ref, alog_ref, cs_ref, xact_ref, acum_ref, acumT_ref,
                 dtpT_ref, *, nb, lc):
    r = nb * lc
    cs_ref[:, HIST:HIST + lc, :] = proj_ref[:, D_INNER:D_INNER + CONV_DIM].reshape(nb, lc, CONV_DIM)
    for j in range(CONV_DIM // LANES):
        cols = slice(j * LANES, (j + 1) * LANES)
        full = cs_ref[:, :, cols]
        conv = cb_ref[:, cols] + full[:, HIST:, :] * cw_ref[CONV_W - 1:CONV_W, cols]
        for s in range(1, CONV_W):
            back = pltpu.roll(full, s, axis=1)[:, HIST:, :]
            conv = conv + back * cw_ref[CONV_W - 1 - s:CONV_W - s, cols]
        xact_ref[:, cols] = _silu(conv).reshape(r, LANES)
    tail = cs_ref[:, HIST + lc - 3:HIST + lc, :]

    dt = proj_ref[:, D_INNER + CONV_DIM:D_INNER + CONV_DIM + LANES]
    dtp = _softplus(dt + dtb_ref[...])
    acum = dtp * (-jnp.exp(alog_ref[...]) * LOG2E)
    local = lax.broadcasted_iota(jnp.int32, (r, LANES), 0) % lc
    sh = 1
    while sh < lc:
        acum = acum + jnp.where(local >= sh, pltpu.roll(acum, sh, axis=0), 0.0)
        sh *= 2
    acum_ref[...] = acum
    acumT_ref[...] = acum.T
    dtpT_ref[...] = dtp.T
    return tail


def _ssd_heads(g, mask, cb, yoff, xact_ref, acum_ref, acumT_ref, dtpT_ref, dsk_ref):
    r = xact_ref.shape[0]
    first = lax.broadcasted_iota(jnp.int32, (r, LANES), 1) < P_B
    ys = []
    for pair in range(R_B // 2):
        c0 = g * GP + pair * LANES
        xs = xact_ref[:, c0:c0 + LANES]
        xs_bf = xs.astype(BF16)
        colbs, ds = [], []
        for rr in range(2):
            hh = g * R_B + pair * 2 + rr
            colb = jnp.broadcast_to(acum_ref[:, hh:hh + 1], (r, r))
            seg = colb - acumT_ref[hh:hh + 1, :]
            m = cb * jnp.exp2(jnp.where(mask, seg, -jnp.inf)) * dtpT_ref[hh:hh + 1, :]
            colbs.append(colb)
            ds.append(_dot(m.astype(BF16), xs_bf))
        y = jnp.where(first, ds[0], ds[1])
        y = y + jnp.exp2(jnp.where(first, colbs[0], colbs[1])) * yoff[:, pair * LANES:(pair + 1) * LANES]
        y = y + dsk_ref[:, c0:c0 + LANES] * xs
        ys.append(y)
    return jnp.concatenate(ys, axis=1)


def _ssd_gate_norm(y, z, ng):
    yz = y * _silu(z)
    ms = jnp.mean(yz * yz, axis=-1, keepdims=True)
    return (yz * lax.rsqrt(ms + EPS) * ng).astype(BF16)


def _head_rows(vals):
    return jnp.concatenate([jnp.broadcast_to(v, (P_B, N_B)) for v in vals], axis=0)


def _ssd_prompt_kernel(h_ref, w_ref, cw_ref, cb_ref, dtb_ref, alog_ref, dsk_ref, ng_ref,
                       ob_ref, cout_ref, hout_ref,
                       proj_ref, cs_ref, hs_ref, xact_ref, acum_ref, acumT_ref, dtpT_ref):
    c = pl.program_id(1)
    lc = CHUNK

    @pl.when(c == 0)
    def _():
        cs_ref[:, 0:HIST, :] = jnp.zeros((1, HIST, CONV_DIM), F32)
        hs_ref[...] = jnp.zeros(hs_ref.shape, F32)

    proj_ref[...] = _dot(h_ref[...], w_ref[...])

    tail = _ssd_stage_a(proj_ref, cw_ref, cb_ref, dtb_ref, alog_ref, cs_ref, xact_ref, acum_ref,
                        acumT_ref, dtpT_ref, nb=1, lc=lc)
    cs_ref[:, HIST - 3:HIST, :] = tail

    @pl.when(c == pl.num_programs(1) - 1)
    def _():
        cout_ref[...] = tail[0]

    li = lax.broadcasted_iota(jnp.int32, (lc, lc), 0)
    si = lax.broadcasted_iota(jnp.int32, (lc, lc), 1)
    mask = li >= si
    acum_t = acumT_ref[...]
    wt = dtpT_ref[...] * jnp.exp2(acum_t[:, lc - 1:lc] - acum_t)
    for g in range(G_B):
        bg = xact_ref[:, D_INNER + g * N_B:D_INNER + (g + 1) * N_B].astype(BF16)
        cg = xact_ref[:, D_INNER + GN + g * N_B:D_INNER + GN + (g + 1) * N_B].astype(BF16)
        cb = _dot_nt(cg, bg)
        hg = hs_ref[g]
        yoff = _dot_nt(cg, hg.astype(BF16))
        y = _ssd_heads(g, mask, cb, yoff, xact_ref, acum_ref, acumT_ref, dtpT_ref, dsk_ref)
        cols = slice(g * GP, (g + 1) * GP)
        ob_ref[:, cols] = _ssd_gate_norm(y, proj_ref[:, cols], ng_ref[:, cols])
        xs_t = xact_ref[:, cols].T
        xw = jnp.concatenate(
            [xs_t[rr * P_B:(rr + 1) * P_B, :] * wt[g * R_B + rr:g * R_B + rr + 1, :] for rr in range(R_B)],
            axis=0)
        inc = _dot(xw.astype(BF16), bg)
        dec = _head_rows([jnp.exp2(acum_t[g * R_B + rr:g * R_B + rr + 1, lc - 1:lc]) for rr in range(R_B)])
        hs_ref[g] = dec * hg + inc

    @pl.when(c == pl.num_programs(1) - 1)
    def _():
        hout_ref[...] = hs_ref[...]


def _ssd_prompt(h, w, cw, cb, dtb, alog, dsk, ng, batch, seq):
    t, d = h.shape
    n = w.shape[1]
    nc = seq // CHUNK
    vec = lambda w: pl.BlockSpec((1, w), lambda b, c: (0, 0))
    return pl.pallas_call(
        _ssd_prompt_kernel,
        grid=(batch, nc),
        in_specs=[pl.BlockSpec((CHUNK, d), lambda b, c: (b * nc + c, 0)),
                  pl.BlockSpec((d, n), lambda b, c: (0, 0)),
                  pl.BlockSpec((CONV_W, CONV_DIM), lambda b, c: (0, 0)),
                  vec(CONV_DIM), vec(LANES), vec(LANES), vec(D_INNER), vec(D_INNER)],
        out_specs=[pl.BlockSpec((CHUNK, D_INNER), lambda b, c: (b * nc + c, 0)),
                   pl.BlockSpec((None, CONV_W - 1, CONV_DIM), lambda b, c: (b, 0, 0)),
                   pl.BlockSpec((None, G_B, GP, N_B), lambda b, c: (b, 0, 0, 0))],
        out_shape=[jax.ShapeDtypeStruct((t, D_INNER), BF16),
                   jax.ShapeDtypeStruct((batch, CONV_W - 1, CONV_DIM), F32),
                   jax.ShapeDtypeStruct((batch, G_B, GP, N_B), F32)],
        scratch_shapes=[pltpu.VMEM((CHUNK, n), F32),
                        pltpu.VMEM((1, HIST + CHUNK, CONV_DIM), F32),
                        pltpu.VMEM((G_B, GP, N_B), F32),
                        pltpu.VMEM((CHUNK, CONV_DIM), F32),
                        pltpu.VMEM((CHUNK, LANES), F32),
                        pltpu.VMEM((LANES, CHUNK), F32),
                        pltpu.VMEM((LANES, CHUNK), F32)],
        compiler_params=_params(("arbitrary", "arbitrary"), 56),
        name="ssd_prompt",
    )(h, w, cw, cb, dtb, alog, dsk, ng)


def _ssd_sample_kernel(*refs, nb, lc, chained):
    (proj_ref, cw_ref, cb_ref, dtb_ref, alog_ref, dsk_ref, ng_ref, cin_ref, hin_ref) = refs[:9]
    n_alias = 2 if chained else 1
    (ob_ref, cout_ref, hout_ref,
     cs_ref, xact_ref, acum_ref, acumT_ref, dtpT_ref, yoff_ref) = refs[9 + n_alias:]
    g = pl.program_id(1)
    r = nb * lc

    @pl.when(g == 0)
    def _():
        cs_ref[:, HIST - 3:HIST, :] = cin_ref[...]
        tail = _ssd_stage_a(proj_ref, cw_ref, cb_ref, dtb_ref, alog_ref, cs_ref, xact_ref, acum_ref,
                            acumT_ref, dtpT_ref, nb=nb, lc=lc)
        cout_ref[...] = tail

    li = lax.broadcasted_iota(jnp.int32, (r, r), 0)
    si = lax.broadcasted_iota(jnp.int32, (r, r), 1)
    mask = (li >= si) & ((li // lc) == (si // lc))
    lane_seq = lax.broadcasted_iota(jnp.int32, (GP, r), 1) // lc
    acum_t = acumT_ref[...]
    dtp_t = dtpT_ref[...]

    def branch(gg):
        bg = xact_ref[:, D_INNER + gg * N_B:D_INNER + (gg + 1) * N_B].astype(BF16)
        cg = xact_ref[:, D_INNER + GN + gg * N_B:D_INNER + GN + (gg + 1) * N_B].astype(BF16)
        cb = _dot_nt(cg, bg)
        cols = slice(gg * GP, (gg + 1) * GP)
        xs_t = xact_ref[:, cols].T
        win = 2 * SUBLANES
        for b in range(nb):
            h0 = hin_ref[b, 0]
            w0 = (b * lc // win) * win
            yo = _dot_nt(cg[w0:w0 + win], h0.astype(BF16))
            yoff_ref[b * lc:(b + 1) * lc, :] = yo[b * lc - w0:b * lc - w0 + lc]
            last = b * lc + lc - 1
            xw = jnp.concatenate(
                [xs_t[rr * P_B:(rr + 1) * P_B, :]
                 * (dtp_t[gg * R_B + rr:gg * R_B + rr + 1, :]
                    * jnp.exp2(acum_t[gg * R_B + rr:gg * R_B + rr + 1, last:last + 1]
                              - acum_t[gg * R_B + rr:gg * R_B + rr + 1, :]))
                 for rr in range(R_B)], axis=0)
            xw = jnp.where(lane_seq == b, xw, 0.0)
            inc = _dot(xw.astype(BF16), bg)
            dec = _head_rows([jnp.exp2(acum_t[gg * R_B + rr:gg * R_B + rr + 1, last:last + 1])
                              for rr in range(R_B)])
            hout_ref[b, 0] = dec * h0 + inc
        y = _ssd_heads(gg, mask, cb, yoff_ref[...], xact_ref, acum_ref, acumT_ref, dtpT_ref, dsk_ref)
        ob_ref[...] = _ssd_gate_norm(y, proj_ref[:, cols], ng_ref[:, cols])

    for gg in range(G_B):
        pl.when(g == gg)(functools.partial(branch, gg))


def _ssd_sample(proj, row0, lc, cw, cb, dtb, alog, dsk, ng, conv_state, ssm_state, layer, ob, ssm_out,
                *, nb=16):
    n = proj.shape[1]
    batch = proj.shape[0] // lc
    r = nb * lc
    blk0 = row0 // r
    sb0 = layer * (batch // nb)
    chained = ssm_out is not None
    vec = lambda w: pl.BlockSpec((1, w), lambda i, g: (0, 0))
    state_spec = pl.BlockSpec((nb, 1, GP, N_B), lambda i, g: (sb0 + i, g, 0, 0))
    hbm = pl.BlockSpec(memory_space=pl.ANY)
    args = [proj, cw, cb, dtb, alog, dsk, ng, conv_state, ssm_state, ob] + ([ssm_out] if chained else [])
    aliases = {9: 0, 10: 2} if chained else {9: 0}
    return pl.pallas_call(
        functools.partial(_ssd_sample_kernel, nb=nb, lc=lc, chained=chained),
        grid=(batch // nb, G_B),
        in_specs=[pl.BlockSpec((r, n), lambda i, g: (i, 0)),
                  pl.BlockSpec((CONV_W, CONV_DIM), lambda i, g: (0, 0)),
                  vec(CONV_DIM), vec(LANES), vec(LANES), vec(D_INNER), vec(D_INNER),
                  pl.BlockSpec((nb, CONV_W - 1, CONV_DIM), lambda i, g: (sb0 + i, 0, 0)),
                  state_spec, hbm] + ([hbm] if chained else []),
        out_specs=[pl.BlockSpec((r, GP), lambda i, g: (blk0 + i, g)),
                   pl.BlockSpec((nb, CONV_W - 1, CONV_DIM), lambda i, g: (i, 0, 0)),
                   state_spec],
        out_shape=[jax.ShapeDtypeStruct(ob.shape, BF16),
                   jax.ShapeDtypeStruct((batch, CONV_W - 1, CONV_DIM), F32),
                   jax.ShapeDtypeStruct(ssm_state.shape, F32)],
        input_output_aliases=aliases,
        scratch_shapes=[pltpu.VMEM((nb, HIST + lc, CONV_DIM), F32),
                        pltpu.VMEM((r, CONV_DIM), F32),
                        pltpu.VMEM((r, LANES), F32),
                        pltpu.VMEM((LANES, r), F32),
                        pltpu.VMEM((LANES, r), F32),
                        pltpu.VMEM((r, GP), F32)],
        compiler_params=_params(("arbitrary", "arbitrary"), 48),
        name="ssd_sample",
    )(*args)


def _mid_kernel(h_ref, oa_ref, ob_ref, x_ref, gtp_ref, gts_ref, scp_ref, scs_ref, shp_ref, shs_ref, n2_ref,
                wg_ref, wpa_ref, wpb_ref, wo_ref, rw_ref, rb_ref,
                x1_ref, h2_ref, gates_ref, idx_ref, rank_ref, cnt_ref, carry_ref, *, tok):
    d = h2_ref.shape[1]

    @pl.when(pl.program_id(0) == 0)
    def _():
        carry_ref[...] = jnp.zeros(carry_ref.shape, F32)

    gt = tok.mod_value((gtp_ref, gts_ref))
    sc = tok.mod_value((scp_ref, scs_ref))
    sh = tok.mod_value((shp_ref, shs_ref))
    for r0 in range(0, tok.tm, MID_ROWS):
        rows = slice(r0, r0 + MID_ROWS)
        _mid_rows(rows, gt[rows], sc[rows], sh[rows], h_ref, oa_ref, ob_ref, x_ref, n2_ref,
                  wg_ref, wpa_ref, wpb_ref, wo_ref, rw_ref, rb_ref,
                  x1_ref, h2_ref, gates_ref, idx_ref, rank_ref, carry_ref, d)
    cnt_ref[...] = carry_ref[...].astype(jnp.int32)


def _mid_rows(rows, gt, sc, sh, h_ref, oa_ref, ob_ref, x_ref, n2_ref, wg_ref, wpa_ref, wpb_ref, wo_ref,
              rw_ref, rb_ref, x1_ref, h2_ref, gates_ref, idx_ref, rank_ref, carry_ref, d):
    tm = rows.stop - rows.start
    gab = _dot(h_ref[rows, :], wg_ref[...])
    m = (_sigmoid(gab[:, :d]) * _dot(oa_ref[rows, :], wpa_ref[...])
         + _sigmoid(gab[:, d:]) * _dot(ob_ref[rows, :], wpb_ref[...]))
    x1 = x_ref[rows, :] + gt * _dot(m.astype(BF16), wo_ref[...])
    x1_ref[rows, :] = x1
    ms = jnp.mean(x1 * x1, axis=-1, keepdims=True)
    h2 = x1 * lax.rsqrt(ms + EPS) * n2_ref[...] * (1.0 + sc) + sh
    h2_ref[rows, :] = h2

    logits = _dot(h2.astype(BF16), rw_ref[...]) + rb_ref[...]
    lane = lax.broadcasted_iota(jnp.int32, logits.shape, 1).astype(F32)
    vals, idxs = [], []
    for _ in range(TOP_K):
        mx = jnp.max(logits, axis=-1, keepdims=True)
        ix = jnp.min(jnp.where(logits == mx, lane, float(LANES)), axis=-1, keepdims=True)
        vals.append(mx)
        idxs.append(ix)
        logits = jnp.where(lane == ix, -jnp.inf, logits)
    es = [jnp.exp(v - vals[0]) for v in vals]
    tot = es[0] + es[1] + es[2] + es[3]
    gates = jnp.zeros(logits.shape, F32)
    idx = jnp.zeros(logits.shape, F32)
    for k in range(TOP_K):
        gates = jnp.where(lane == float(k), es[k] / tot, gates)
        idx = jnp.where(lane == float(k), idxs[k], idx)
    gates_ref[rows, :] = gates
    idx_ref[rows, :] = idx.astype(jnp.int32)

    earlier = (lax.broadcasted_iota(jnp.int32, (tm, tm), 1)
               < lax.broadcasted_iota(jnp.int32, (tm, tm), 0)).astype(BF16)
    onehots = [(lane == idxs[k]).astype(F32) for k in range(TOP_K)]
    prefix = _dot(earlier, jnp.concatenate(onehots, axis=1).astype(BF16))
    base = carry_ref[...]
    rank = jnp.zeros(logits.shape, F32)
    for k in range(TOP_K):
        before = prefix[:, k * LANES:(k + 1) * LANES] + base
        rank = jnp.where(lane == float(k), jnp.sum(onehots[k] * before, axis=-1, keepdims=True), rank)
        base = base + jnp.sum(onehots[k], axis=0, keepdims=True)
    carry_ref[...] = base
    rank_ref[rows, :] = rank.astype(jnp.int32)


def _mid(tok, layer, h, oa, ob, x, mod, n2, wg, wpa, wpb, wo, rw, rb):
    t, d = h.shape
    tm = tok.tm
    row = lambda w: pl.BlockSpec((tm, w), lambda i: (i, 0))
    full = lambda a: pl.BlockSpec(a.shape, lambda i: (0,) * a.ndim, pipeline_mode=pl.Buffered(1))
    n2 = n2.reshape(1, d)
    return pl.pallas_call(
        functools.partial(_mid_kernel, tok=tok),
        grid=(tok.n_tiles,),
        in_specs=[row(d), row(d), row(2 * d), row(d),
                  *tok.mod_specs(layer, K_GT1, d), *tok.mod_specs(layer, K_SC2, d),
                  *tok.mod_specs(layer, K_SH2, d),
                  full(n2), full(wg), full(wpa), full(wpb), full(wo), full(rw), full(rb)],
        out_specs=[row(d), row(d), row(LANES), row(LANES), row(LANES),
                   pl.BlockSpec((1, LANES), lambda i: (0, 0))],
        out_shape=[jax.ShapeDtypeStruct((t, d), F32),
                   jax.ShapeDtypeStruct((t, d), F32),
                   jax.ShapeDtypeStruct((t, LANES), F32),
                   jax.ShapeDtypeStruct((t, LANES), jnp.int32),
                   jax.ShapeDtypeStruct((t, LANES), jnp.int32),
                   jax.ShapeDtypeStruct((1, LANES), jnp.int32)],
        scratch_shapes=[pltpu.VMEM((1, LANES), F32)],
        compiler_params=_params(("arbitrary",), 56),
        name="mid",
    )(h, oa, ob, x, *mod, *mod, *mod, n2, wg, wpa, wpb, wo, rw, rb)


def _dispatch_kernel(fill0_ref, filln_ref, n_used_ref, pos_ref, h_ref, xb_ref, zrow, sem, zsem):
    i = pl.program_id(0)
    tm = h_ref.shape[0]
    n_e = filln_ref.shape[0]
    blk = zrow.shape[0]
    n_blocks = xb_ref.shape[0] // blk

    for t in range(tm):
        for k in range(TOP_K):
            _row_copy(h_ref, t, xb_ref, pos_ref[t // POS_TOKENS, (t % POS_TOKENS) * TOP_K + k], sem).start(
                priority=k % 2)

    @pl.when(i == pl.num_programs(0) - 1)
    def _():
        zrow[...] = jnp.zeros(zrow.shape, F32)

        def fill_expert(e, c):
            def fill(r, c2):
                _row_copy(zrow, 0, xb_ref, fill0_ref[e] + r, zsem).start()
                return c2
            return lax.fori_loop(0, filln_ref[e], fill, c)
        lax.fori_loop(0, n_e, fill_expert, 0)

        def drain_expert(e, c):
            def drain(r, c2):
                _row_copy(zrow, 0, xb_ref, 0, zsem).wait()
                return c2
            return lax.fori_loop(0, filln_ref[e], drain, c)
        lax.fori_loop(0, n_e, drain_expert, 0)

        def block_copy(b):
            return pltpu.make_async_copy(zrow, xb_ref.at[pl.ds(pl.multiple_of(b * blk, blk), blk)], zsem)

        def fill_block(b, c):
            block_copy(b).start()
            return c
        lax.fori_loop(n_used_ref[0], n_blocks, fill_block, 0)

        def drain_block(b, c):
            block_copy(b).wait()
            return c
        lax.fori_loop(n_used_ref[0], n_blocks, drain_block, 0)

    for k in range(TOP_K):
        pltpu.make_async_copy(h_ref, xb_ref.at[pl.ds(0, tm)], sem).wait()


def _dispatch(h2, pos, fill0, filln, n_used, rows, *, tm, blk):
    t, d = h2.shape
    grid_spec = pltpu.PrefetchScalarGridSpec(
        num_scalar_prefetch=3,
        grid=(t // tm,),
        in_specs=[pl.BlockSpec((None, tm // POS_TOKENS, LANES), lambda i, f0, fn, nu: (i, 0, 0),
                               memory_space=pltpu.SMEM),
                  pl.BlockSpec((tm, d), lambda i, f0, fn, nu: (i, 0))],
        out_specs=pl.BlockSpec(memory_space=pl.ANY),
        scratch_shapes=[pltpu.VMEM((blk, d), F32), pltpu.SemaphoreType.DMA(()),
                        pltpu.SemaphoreType.DMA(())])
    return pl.pallas_call(
        _dispatch_kernel,
        grid_spec=grid_spec,
        out_shape=jax.ShapeDtypeStruct((rows, d), F32),
        compiler_params=_params(("arbitrary",), 32),
        name="dispatch",
    )(fill0, filln, n_used, pos, h2)


def _moe_kernel(blk_e_ref, blk_first_ref, n_used_ref, x_ref, wgu_ref, bgu_ref, wdn_ref, bdn_ref, o_ref,
                wgu_bf, wdn_bf):
    i = pl.program_id(0)
    d_ff = wdn_bf.shape[0]

    @pl.when(blk_first_ref[i] == 1)
    def _():
        wgu_bf[...] = wgu_ref[...].astype(BF16)
        wdn_bf[...] = wdn_ref[...].astype(BF16)

    @pl.when(i < n_used_ref[0])
    def _():
        gu = _dot(x_ref[...].astype(BF16), wgu_bf[...]) + bgu_ref[...]
        g = jnp.minimum(gu[:, :d_ff], SWIGLU_LIMIT)
        u = jnp.clip(gu[:, d_ff:], -SWIGLU_LIMIT, SWIGLU_LIMIT)
        act = g * _sigmoid(SWIGLU_ALPHA * g) * (u + 1.0)
        o_ref[...] = _dot(act.astype(BF16), wdn_bf[...]) + bdn_ref[...]

    @pl.when(i >= n_used_ref[0])
    def _():
        o_ref[...] = jnp.zeros(o_ref.shape, F32)


def _moe(xb, blk_e, blk_first, n_used, w_gu, b_gu, w_dn, b_dn, e0, *, tm):
    rows, d = xb.shape
    _, _, d_gu = w_gu.shape
    d_ff = w_dn.shape[1]
    expert = lambda i, be, bf, nu: (e0 + be[i], 0, 0)
    grid_spec = pltpu.PrefetchScalarGridSpec(
        num_scalar_prefetch=3,
        grid=(rows // tm,),
        in_specs=[pl.BlockSpec((tm, d), lambda i, be, bf, nu: (jnp.maximum(jnp.minimum(i, nu[0] - 1), 0), 0)),
                  pl.BlockSpec((None, d, d_gu), expert),
                  pl.BlockSpec((None, 1, d_gu), expert),
                  pl.BlockSpec((None, d_ff, d), expert),
                  pl.BlockSpec((None, 1, d), expert)],
        out_specs=pl.BlockSpec((tm, d), lambda i, be, bf, nu: (i, 0)),
        scratch_shapes=[pltpu.VMEM((d, d_gu), BF16), pltpu.VMEM((d_ff, d), BF16)])
    return pl.pallas_call(
        _moe_kernel,
        grid_spec=grid_spec,
        out_shape=jax.ShapeDtypeStruct((rows, d), F32),
        compiler_params=_params(("arbitrary",), 56),
        name="moe",
    )(blk_e, blk_first, n_used, xb, w_gu, b_gu, w_dn, b_dn)


def _route(idx, rank, counts, n_e, tm, tm_tok):
    t = idx.shape[0]
    padded = (counts + tm - 1) // tm * tm
    pad_end = jnp.cumsum(padded)
    pad_start = pad_end - padded
    experts = jnp.arange(n_e, dtype=jnp.int32)
    pos = jnp.sum(jnp.where(idx[..., None] == experts, pad_start, 0), axis=-1) + rank
    pos = pos.reshape(t // tm_tok, tm_tok // POS_TOKENS, LANES).astype(jnp.int32)
    n_blocks = -(-t * TOP_K // tm) + n_e
    blk_start = jnp.arange(n_blocks, dtype=jnp.int32) * tm
    blk_e = jnp.minimum(jnp.sum((blk_start[:, None] >= pad_end[None, :]).astype(jnp.int32), axis=1),
                        n_e - 1).astype(jnp.int32)
    blk_first = jnp.concatenate([jnp.ones((1,), jnp.int32),
                                 (blk_e[1:] != blk_e[:-1]).astype(jnp.int32)])
    n_used = (pad_end[-1:] // tm).astype(jnp.int32)
    return (pos, blk_e, blk_first, n_used, (pad_start + counts).astype(jnp.int32),
            (padded - counts).astype(jnp.int32), n_blocks * tm)


def kernel(x_prompt, x_sample, c_prompt, c_sample, state_ssm, state_conv, norm1_g, norm2_g, final_g,
           w_ada, b_ada, w_in, a_ln_g, a_ln_b, a_ws, a_bs, b_conv_w, b_conv_b, b_dt_bias, b_a_log, b_d,
           b_norm_g, w_proj_a, w_proj_b, w_out, router_w, router_b, w_gu, b_gu, w_dn, b_dn):
    bp, seq, d = x_prompt.shape
    bs, lc_s, _ = x_sample.shape
    depth = w_ada.shape[0]
    n_e = router_w.shape[-1]
    heads = b_d.shape[-1]
    moe_tm = 512
    tok = _Tokens(bp, seq, bs, lc_s, tm=256)
    tok_mid = _Tokens(bp, seq, bs, lc_s, tm=2 * MID_ROWS)
    t_p, t_s, t = tok.t_p, tok.t_s, tok.t

    d_a = d
    c0 = 2 * d_a
    c1 = c0 + D_INNER + CONV_DIM + heads
    w_in_bf = w_in.astype(BF16)
    w_uv = w_in_bf[:, :, :c0]
    w_ssd = jnp.pad(w_in_bf[:, :, c0:c1], ((0, 0), (0, 0), (0, LANES - heads)))
    w_g = w_in_bf[:, :, c1:]
    w_pa, w_pb, w_o = w_proj_a.astype(BF16), w_proj_b.astype(BF16), w_out.astype(BF16)
    rw = jnp.pad(router_w, ((0, 0), (0, 0), (0, LANES - n_e))).astype(BF16)
    rb = jnp.pad(router_b, ((0, 0), (0, LANES - n_e)), constant_values=-jnp.inf).reshape(depth, 1, LANES)
    causal = jnp.tril(jnp.ones((CHUNK, CHUNK), bool))
    ws_p = jnp.where(causal, a_ws, 0.0)
    blk = jnp.where(jnp.tril(jnp.ones((lc_s, lc_s), bool)), a_ws[:, :, :lc_s, :lc_s], 0.0)
    eye = jnp.eye(CHUNK // lc_s, dtype=F32)
    ws_s = jnp.einsum("ab,lgts->lgatbs", eye, blk).reshape(depth, G_A, CHUNK, CHUNK)
    wmix = jnp.stack([ws_p, ws_s], axis=1).astype(BF16)
    dg = d_a // G_A
    bias_p = jnp.repeat(jnp.swapaxes(a_bs, 1, 2), dg, axis=2)
    bias_s = jnp.tile(bias_p[:, :lc_s], (1, CHUNK // lc_s, 1))
    bias = jnp.stack([bias_p, bias_s], axis=1)
    dtb = jnp.pad(b_dt_bias, ((0, 0), (0, LANES - heads))).reshape(depth, 1, LANES)
    alog = jnp.pad(b_a_log, ((0, 0), (0, LANES - heads))).reshape(depth, 1, LANES)
    dsk = jnp.repeat(b_d, P_B, axis=1).reshape(depth, 1, D_INNER)
    cb = b_conv_b.reshape(depth, 1, CONV_DIM)
    ng = b_norm_g.reshape(depth, 1, D_INNER)
    w_gu_all = w_gu.reshape(depth * n_e, d, w_gu.shape[-1])
    b_gu_all = b_gu.reshape(depth * n_e, 1, b_gu.shape[-1])
    w_dn_all = w_dn.reshape(depth * n_e, w_dn.shape[-2], d)
    b_dn_all = b_dn.reshape(depth * n_e, 1, d)
    ssm_in = state_ssm.reshape(depth * bs, G_B, GP, N_B)
    conv_in = state_conv.reshape(depth * bs, CONV_W - 1, state_conv.shape[-1])

    mod_all = _ada(jnp.concatenate([c_prompt, c_sample], axis=0), w_ada, b_ada)
    mod_all = mod_all.reshape(depth, bp + bs, N_MOD, d).transpose(0, 2, 1, 3)
    mod = (mod_all[:, :, :bp], jnp.repeat(mod_all[:, :, bp:], lc_s, axis=2))

    x = jnp.concatenate([x_prompt.reshape(t_p, d), x_sample.reshape(t_s, d)], axis=0)

    h = _comb(tok, x, norm1_g[0], mod, layer_n=0)[0]
    ssm_p, conv_p, conv_s, v_s = [], [], [], []
    ssm_s = None
    y_final = None
    for l in range(depth):
        oa, v_rows = _gmlp(h, w_uv[l], a_ln_g[l], a_ln_b[l], wmix[l], bias[l], t_p)
        ob, cp, sp = _ssd_prompt(h, w_ssd[l], b_conv_w[l], cb[l], dtb[l], alog[l], dsk[l], ng[l], bp, seq)
        proj_s = _proj(h[t_p:], w_ssd[l])
        ob, cs, ssm_s = _ssd_sample(proj_s, t_p, lc_s, b_conv_w[l], cb[l], dtb[l], alog[l], dsk[l], ng[l],
                                    conv_in, ssm_in, l, ob, ssm_s)
        x1, h2, gates, idx, rank, cnt = _mid(tok_mid, l, h, oa, ob, x, mod, norm2_g[l], w_g[l], w_pa[l],
                                             w_pb[l], w_o[l], rw[l], rb[l])
        pos, blk_e, blk_first, n_used, fill0, filln, rows = _route(
            idx[:, :TOP_K], rank[:, :TOP_K], cnt[0, :n_e], n_e, moe_tm, tok.tm)
        xb = _dispatch(h2, pos, fill0, filln, n_used, rows, tm=tok.tm, blk=moe_tm)
        yb = _moe(xb, blk_e, blk_first, n_used, w_gu_all, b_gu_all, w_dn_all, b_dn_all, l * n_e, tm=moe_tm)
        if l + 1 < depth:
            x, h = _comb(tok, x1, norm1_g[l + 1], mod, layer_n=l + 1, yb=yb, pos=pos, gates=gates, layer_gt=l)
        else:
            y_final = _comb(tok, x1, final_g, mod, yb=yb, pos=pos, gates=gates, layer_gt=l)[0]
        ssm_p.append(sp.reshape(bp, heads, P_B, N_B))
        conv_p.append(cp)
        conv_s.append(cs)
        v_s.append(v_rows.reshape(bs, lc_s, d))
    return (y_final[:t_p].reshape(bp, seq, d), y_final[t_p:].reshape(bs, lc_s, d),
            jnp.stack(ssm_p), jnp.stack(conv_p), ssm_s.reshape(depth, bs, heads, P_B, N_B),
            jnp.stack(conv_s), jnp.stack(v_s))
```

```python
import functools

import jax
import jax.numpy as jnp
from jax import lax
from jax.experimental import pallas as pl
from jax.experimental.pallas import tpu as pltpu

F32 = jnp.float32
BF16 = jnp.bfloat16

EPS = 1e-6
SUBLANES = 8
LANES = 128
CHUNK = 128
G_A = 8
G_B = 8
R_B = 4
P_B = 64
N_B = 128
CONV_W = 4
TOP_K = 4
SWIGLU_LIMIT = 7.0
SWIGLU_ALPHA = 1.702
MIB = 2 ** 20
N_MOD = 6
K_SH1, K_SC1, K_GT1, K_SH2, K_SC2, K_GT2 = range(N_MOD)
POS_TOKENS = LANES // TOP_K
MID_ROWS = 256
LOG2E = 1.4426950408889634


def _params(sem, vmem_mib):
    return pltpu.CompilerParams(dimension_semantics=sem, vmem_limit_bytes=vmem_mib * MIB)


def _sigmoid(x):
    return 1.0 / (1.0 + jnp.exp(-x))


def _silu(x):
    return x * _sigmoid(x)


def _gelu(x):
    return 0.5 * x * (1.0 + lax.erf(x * (0.5 ** 0.5)))


def _softplus(x):
    return jnp.maximum(x, 0.0) + jnp.log(1.0 + jnp.exp(-jnp.abs(x)))


def _dot(a, b):
    return jnp.dot(a, b, preferred_element_type=F32)


def _dot_nt(a, b):
    return lax.dot_general(a, b, (((1,), (1,)), ((), ())), preferred_element_type=F32)


class _Tokens:
    def __init__(self, bp, seq, bs, lc_s, tm):
        self.tm = tm
        self.t_p, self.t_s = bp * seq, bs * lc_s
        self.t = self.t_p + self.t_s
        self.n_tiles = self.t // tm
        self.n_p = self.t_p // tm
        self.per_seq = seq // tm
        self.bp = bp

    def mod_specs(self, layer, k, d):
        return [pl.BlockSpec((None, None, self.bp, d), lambda i: (layer, k, 0, 0)),
                pl.BlockSpec((None, None, self.tm, d),
                             lambda i: (layer, k, jnp.maximum(i - self.n_p, 0), 0))]

    def mod_value(self, refs):
        mp_ref, ms_ref = refs
        i = pl.program_id(0)
        seq_row = mp_ref[pl.ds(jnp.minimum(i // self.per_seq, self.bp - 1), 1), :]
        return jnp.where(i < self.n_p, seq_row, ms_ref[...])


def _ada_kernel(c_ref, w_ref, b_ref, o_ref):
    c = c_ref[...]
    o_ref[...] = _dot(_silu(c).astype(BF16), w_ref[...].astype(BF16)) + b_ref[...]


def _ada(c_all, w_ada, b_ada):
    depth, d, n = w_ada.shape
    nc = c_all.shape[0]
    tn = 1536
    return pl.pallas_call(
        _ada_kernel,
        grid=(depth, n // tn),
        in_specs=[pl.BlockSpec((nc, d), lambda l, j: (0, 0)),
                  pl.BlockSpec((None, d, tn), lambda l, j: (l, 0, j)),
                  pl.BlockSpec((None, 1, tn), lambda l, j: (l, 0, j))],
        out_specs=pl.BlockSpec((None, nc, tn), lambda l, j: (l, 0, j)),
        out_shape=jax.ShapeDtypeStruct((depth, nc, n), F32),
        compiler_params=_params(("arbitrary", "arbitrary"), 40),
        name="ada",
    )(c_all, w_ada, b_ada.reshape(depth, 1, n))


def _row_copy(src_ref, src_row, dst_ref, dst_row, sem):
    return pltpu.make_async_copy(src_ref.at[pl.ds(src_row, 1)], dst_ref.at[pl.ds(dst_row, 1)], sem)


def _comb_kernel(*refs, tok, combine, final):
    it = iter(refs)
    if combine:
        pos_ref, posn_ref = next(it), next(it)
    x_ref = next(it)
    if combine:
        yb_ref, g_ref, gt_refs = next(it), next(it), (next(it), next(it))
    n_ref = next(it)
    if not final:
        sc_refs, sh_refs = (next(it), next(it)), (next(it), next(it))
    if combine and not final:
        xo_ref = next(it)
    o_ref = next(it)
    if combine:
        ybuf, sem = next(it), next(it)

    i = pl.program_id(0)
    tm = x_ref.shape[0]

    def gather_row(p_ref, s, j):
        for u in range(POS_TOKENS):
            for k in range(TOP_K):
                _row_copy(yb_ref, p_ref[j, u * TOP_K + k], ybuf.at[s, k], j * POS_TOKENS + u, sem.at[s]).start(
                    priority=k % 2)

    def drain(s):
        for k in range(TOP_K):
            pltpu.make_async_copy(yb_ref.at[pl.ds(0, tm)], ybuf.at[s, k], sem.at[s]).wait()

    def step(slot):
        if combine:
            drain(slot)
            gt = tok.mod_value(gt_refs)
            g = g_ref[...]
        if not final:
            sc, sh = tok.mod_value(sc_refs), tok.mod_value(sh_refs)
        for j in range(tm // POS_TOKENS):
            rows = slice(j * POS_TOKENS, (j + 1) * POS_TOKENS)
            xc = x_ref[rows, :]
            if combine:
                gather_row(posn_ref, 1 - slot, j)
                acc = g[rows, 0:1] * ybuf[slot, 0, rows, :]
                for k in range(1, TOP_K):
                    acc = acc + g[rows, k:k + 1] * ybuf[slot, k, rows, :]
                xc = xc + gt[rows] * acc
                if not final:
                    xo_ref[rows, :] = xc
            ms = jnp.mean(xc * xc, axis=-1, keepdims=True)
            xn = xc * lax.rsqrt(ms + EPS)
            if final:
                o_ref[rows, :] = xn * n_ref[...]
            else:
                o_ref[rows, :] = (xn * n_ref[...] * (1.0 + sc[rows]) + sh[rows]).astype(BF16)
        if combine:
            @pl.when(i == pl.num_programs(0) - 1)
            def _():
                drain(1 - slot)

    if not combine:
        step(0)
        return

    @pl.when(i == 0)
    def _():
        for j in range(tm // POS_TOKENS):
            gather_row(pos_ref, 0, j)

    for slot in range(2):
        pl.when(i % 2 == slot)(functools.partial(step, slot))


def _comb(tok, x, norm_g, mod, layer_n=None, yb=None, pos=None, gates=None, layer_gt=None):
    t, d = x.shape
    tm = tok.tm
    combine = yb is not None
    final = layer_n is None
    row = lambda w: pl.BlockSpec((tm, w), lambda i: (i, 0))

    args, specs, scratch = [], [], []
    if combine:
        args += [pos, pos]
        pos_rows = tm // POS_TOKENS
        specs += [pl.BlockSpec((None, pos_rows, LANES), lambda i: (i, 0, 0), memory_space=pltpu.SMEM),
                  pl.BlockSpec((None, pos_rows, LANES), lambda i: (jnp.minimum(i + 1, tok.n_tiles - 1), 0, 0),
                               memory_space=pltpu.SMEM)]
        scratch = [pltpu.VMEM((2, TOP_K, tm, d), F32), pltpu.SemaphoreType.DMA((2,))]
    args.append(x)
    specs.append(row(d))
    if combine:
        args += [yb, gates, *mod]
        specs += [pl.BlockSpec(memory_space=pl.ANY), row(LANES), *tok.mod_specs(layer_gt, K_GT2, d)]
    args.append(norm_g.reshape(1, d))
    specs.append(pl.BlockSpec((1, d), lambda i: (0, 0)))
    if not final:
        args += [*mod, *mod]
        specs += [*tok.mod_specs(layer_n, K_SC1, d), *tok.mod_specs(layer_n, K_SH1, d)]
    out_shape, out_specs = [], []
    if combine and not final:
        out_shape.append(jax.ShapeDtypeStruct((t, d), F32))
        out_specs.append(row(d))
    out_shape.append(jax.ShapeDtypeStruct((t, d), F32 if final else BF16))
    out_specs.append(row(d))
    return pl.pallas_call(
        functools.partial(_comb_kernel, tok=tok, combine=combine, final=final),
        grid=(tok.n_tiles,),
        in_specs=specs, out_specs=out_specs, out_shape=out_shape,
        scratch_shapes=scratch,
        compiler_params=_params(("arbitrary",), 48),
        name="comb",
    )(*args)


def _gmlp_kernel(h_ref, w_ref, lng_ref, lnb_ref, wmix_ref, bias_ref, o_ref, v_ref, *, n_prompt_tiles):
    i = pl.program_id(0)
    tm, d = o_ref.shape
    uv = _dot(h_ref[...], w_ref[...])
    u = _gelu(uv[:, :d])
    v = _gelu(uv[:, d:])
    mu = jnp.mean(v, axis=-1, keepdims=True)
    vc = v - mu
    var = jnp.mean(vc * vc, axis=-1, keepdims=True)
    vn = vc * lax.rsqrt(var + EPS) * lng_ref[...] + lnb_ref[...]

    @pl.when(i >= n_prompt_tiles)
    def _():
        v_ref[...] = vn

    vb = vn.astype(BF16)
    dg = d // G_A
    for c in range(tm // CHUNK):
        rows = slice(c * CHUNK, (c + 1) * CHUNK)
        for g in range(G_A):
            cols = slice(g * dg, (g + 1) * dg)
            s = _dot(wmix_ref[g], vb[rows, cols]) + bias_ref[:, cols]
            o_ref[rows, cols] = (u[rows, cols] * s).astype(BF16)


def _gmlp(h, w_uv, ln_g, ln_b, wmix, bias, t_prompt, *, tm=512):
    t, d = h.shape
    n_p = t_prompt // tm
    t_s = t - t_prompt
    sel = lambda i: jnp.where(i >= n_p, 1, 0)
    return pl.pallas_call(
        functools.partial(_gmlp_kernel, n_prompt_tiles=n_p),
        grid=(t // tm,),
        in_specs=[pl.BlockSpec((tm, d), lambda i: (i, 0)),
                  pl.BlockSpec((d, 2 * d), lambda i: (0, 0)),
                  pl.BlockSpec((1, d), lambda i: (0, 0)),
                  pl.BlockSpec((1, d), lambda i: (0, 0)),
                  pl.BlockSpec((None, G_A, CHUNK, CHUNK), lambda i: (sel(i), 0, 0, 0)),
                  pl.BlockSpec((None, CHUNK, d), lambda i: (sel(i), 0, 0))],
        out_specs=[pl.BlockSpec((tm, d), lambda i: (i, 0)),
                   pl.BlockSpec((tm, d), lambda i: (jnp.maximum(i - n_p, 0), 0))],
        out_shape=[jax.ShapeDtypeStruct((t, d), BF16),
                   jax.ShapeDtypeStruct((t_s, d), F32)],
        compiler_params=_params(("arbitrary",), 48),
        name="gmlp",
    )(h, w_uv, ln_g.reshape(1, d), ln_b.reshape(1, d), wmix, bias)


def _proj_kernel(x_ref, w_ref, o_ref):
    o_ref[...] = _dot(x_ref[...], w_ref[...])


def _proj(x, w, *, tm=256):
    t, d = x.shape
    n = w.shape[1]
    return pl.pallas_call(
        _proj_kernel,
        grid=(t // tm,),
        in_specs=[pl.BlockSpec((tm, d), lambda i: (i, 0)),
                  pl.BlockSpec((d, n), lambda i: (0, 0))],
        out_specs=pl.BlockSpec((tm, n), lambda i: (i, 0)),
        out_shape=jax.ShapeDtypeStruct((t, n), F32),
        compiler_params=_params(("arbitrary",), 56),
        name="inproj",
    )(x, w)


D_INNER = G_B * R_B * P_B
GN = G_B * N_B
CONV_DIM = D_INNER + 2 * GN
GP = R_B * P_B
HIST = SUBLANES


def _ssd_stage_a(proj_ref, cw_ref, cb_ref, dtb_ref, alog_ref, cs_ref, xact_ref, acum_ref, acumT_ref,
                 dtpT_ref, *, nb, lc):
    r = nb * lc
    cs_ref[:, HIST:HIST + lc, :] = proj_ref[:, D_INNER:D_INNER + CONV_DIM].reshape(nb, lc, CONV_DIM)
    for j in range(CONV_DIM // LANES):
        cols = slice(j * LANES, (j + 1) * LANES)
        full = cs_ref[:, :, cols]
        conv = cb_ref[:, cols] + full[:, HIST:, :] * cw_ref[CONV_W - 1:CONV_W, cols]
        for s in range(1, CONV_W):
            back = pltpu.roll(full, s, axis=1)[:, HIST:, :]
            conv = conv + back * cw_ref[CONV_W - 1 - s:CONV_W - s, cols]
        xact_ref[:, cols] = _silu(conv).reshape(r, LANES)
    tail = cs_ref[:, HIST + lc - 3:HIST + lc, :]

    dt = proj_ref[:, D_INNER + CONV_DIM:D_INNER + CONV_DIM + LANES]
    dtp = _softplus(dt + dtb_ref[...])
    acum = dtp * (-jnp.exp(alog_ref[...]) * LOG2E)
    local = lax.broadcasted_iota(jnp.int32, (r, LANES), 0) % lc
    sh = 1
    while sh < lc:
        acum = acum + jnp.where(local >= sh, pltpu.roll(acum, sh, axis=0), 0.0)
        sh *= 2
    acum_ref[...] = acum
    acumT_ref[...] = acum.T
    dtpT_ref[...] = dtp.T
    return tail


def _ssd_heads(g, mask, cb, yoff, xact_ref, acum_ref, acumT_ref, dtpT_ref, dsk_ref):
    r = xact_ref.shape[0]
    first = lax.broadcasted_iota(jnp.int32, (r, LANES), 1) < P_B
    ys = []
    for pair in range(R_B // 2):
        c0 = g * GP + pair * LANES
        xs = xact_ref[:, c0:c0 + LANES]
        xs_bf = xs.astype(BF16)
        colbs, ds = [], []
        for rr in range(2):
            hh = g * R_B + pair * 2 + rr
            colb = jnp.broadcast_to(acum_ref[:, hh:hh + 1], (r, r))
            seg = colb - acumT_ref[hh:hh + 1, :]
            m = cb * jnp.exp2(jnp.where(mask, seg, -jnp.inf)) * dtpT_ref[hh:hh + 1, :]
            colbs.append(colb)
            ds.append(_dot(m.astype(BF16), xs_bf))
        y = jnp.where(first, ds[0], ds[1])
        y = y + jnp.exp2(jnp.where(first, colbs[0], colbs[1])) * yoff[:, pair * LANES:(pair + 1) * LANES]
        y = y + dsk_ref[:, c0:c0 + LANES] * xs
        ys.append(y)
    return jnp.concatenate(ys, axis=1)


def _ssd_gate_norm(y, z, ng):
    yz = y * _silu(z)
    ms = jnp.mean(yz * yz, axis=-1, keepdims=True)
    return (yz * lax.rsqrt(ms + EPS) * ng).astype(BF16)


def _head_rows(vals):
    return jnp.concatenate([jnp.broadcast_to(v, (P_B, N_B)) for v in vals], axis=0)


def _ssd_prompt_kernel(h_ref, w_ref, cw_ref, cb_ref, dtb_ref, alog_ref, dsk_ref, ng_ref,
                       ob_ref, cout_ref, hout_ref,
                       proj_ref, cs_ref, hs_ref, xact_ref, acum_ref, acumT_ref, dtpT_ref):
    c = pl.program_id(1)
    lc = CHUNK

    @pl.when(c == 0)
    def _():
        cs_ref[:, 0:HIST, :] = jnp.zeros((1, HIST, CONV_DIM), F32)
        hs_ref[...] = jnp.zeros(hs_ref.shape, F32)

    proj_ref[...] = _dot(h_ref[...], w_ref[...])

    tail = _ssd_stage_a(proj_ref, cw_ref, cb_ref, dtb_ref, alog_ref, cs_ref, xact_ref, acum_ref,
                        acumT_ref, dtpT_ref, nb=1, lc=lc)
    cs_ref[:, HIST - 3:HIST, :] = tail

    @pl.when(c == pl.num_programs(1) - 1)
    def _():
        cout_ref[...] = tail[0]

    li = lax.broadcasted_iota(jnp.int32, (lc, lc), 0)
    si = lax.broadcasted_iota(jnp.int32, (lc, lc), 1)
    mask = li >= si
    acum_t = acumT_ref[...]
    wt = dtpT_ref[...] * jnp.exp2(acum_t[:, lc - 1:lc] - acum_t)
    for g in range(G_B):
        bg = xact_ref[:, D_INNER + g * N_B:D_INNER + (g + 1) * N_B].astype(BF16)
        cg = xact_ref[:, D_INNER + GN + g * N_B:D_INNER + GN + (g + 1) * N_B].astype(BF16)
        cb = _dot_nt(cg, bg)
        hg = hs_ref[g]
        yoff = _dot_nt(cg, hg.astype(BF16))
        y = _ssd_heads(g, mask, cb, yoff, xact_ref, acum_ref, acumT_ref, dtpT_ref, dsk_ref)
        cols = slice(g * GP, (g + 1) * GP)
        ob_ref[:, cols] = _ssd_gate_norm(y, proj_ref[:, cols], ng_ref[:, cols])
        xs_t = xact_ref[:, cols].T
        xw = jnp.concatenate(
            [xs_t[rr * P_B:(rr + 1) * P_B, :] * wt[g * R_B + rr:g * R_B + rr + 1, :] for rr in range(R_B)],
            axis=0)
        inc = _dot(xw.astype(BF16), bg)
        dec = _head_rows([jnp.exp2(acum_t[g * R_B + rr:g * R_B + rr + 1, lc - 1:lc]) for rr in range(R_B)])
        hs_ref[g] = dec * hg + inc

    @pl.when(c == pl.num_programs(1) - 1)
    def _():
        hout_ref[...] = hs_ref[...]


def _ssd_prompt(h, w, cw, cb, dtb, alog, dsk, ng, batch, seq):
    t, d = h.shape
    n = w.shape[1]
    nc = seq // CHUNK
    vec = lambda w: pl.BlockSpec((1, w), lambda b, c: (0, 0))
    return pl.pallas_call(
        _ssd_prompt_kernel,
        grid=(batch, nc),
        in_specs=[pl.BlockSpec((CHUNK, d), lambda b, c: (b * nc + c, 0)),
                  pl.BlockSpec((d, n), lambda b, c: (0, 0)),
                  pl.BlockSpec((CONV_W, CONV_DIM), lambda b, c: (0, 0)),
                  vec(CONV_DIM), vec(LANES), vec(LANES), vec(D_INNER), vec(D_INNER)],
        out_specs=[pl.BlockSpec((CHUNK, D_INNER), lambda b, c: (b * nc + c, 0)),
                   pl.BlockSpec((None, CONV_W - 1, CONV_DIM), lambda b, c: (b, 0, 0)),
                   pl.BlockSpec((None, G_B, GP, N_B), lambda b, c: (b, 0, 0, 0))],
        out_shape=[jax.ShapeDtypeStruct((t, D_INNER), BF16),
                   jax.ShapeDtypeStruct((batch, CONV_W - 1, CONV_DIM), F32),
                   jax.ShapeDtypeStruct((batch, G_B, GP, N_B), F32)],
        scratch_shapes=[pltpu.VMEM((CHUNK, n), F32),
                        pltpu.VMEM((1, HIST + CHUNK, CONV_DIM), F32),
                        pltpu.VMEM((G_B, GP, N_B), F32),
                        pltpu.VMEM((CHUNK, CONV_DIM), F32),
                        pltpu.VMEM((CHUNK, LANES), F32),
                        pltpu.VMEM((LANES, CHUNK), F32),
                        pltpu.VMEM((LANES, CHUNK), F32)],
        compiler_params=_params(("arbitrary", "arbitrary"), 56),
        name="ssd_prompt",
    )(h, w, cw, cb, dtb, alog, dsk, ng)


def _ssd_sample_kernel(*refs, nb, lc, chained):
    (proj_ref, cw_ref, cb_ref, dtb_ref, alog_ref, dsk_ref, ng_ref, cin_ref, hin_ref) = refs[:9]
    n_alias = 2 if chained else 1
    (ob_ref, cout_ref, hout_ref,
     cs_ref, xact_ref, acum_ref, acumT_ref, dtpT_ref, yoff_ref) = refs[9 + n_alias:]
    g = pl.program_id(1)
    r = nb * lc

    @pl.when(g == 0)
    def _():
        cs_ref[:, HIST - 3:HIST, :] = cin_ref[...]
        tail = _ssd_stage_a(proj_ref, cw_ref, cb_ref, dtb_ref, alog_ref, cs_ref, xact_ref, acum_ref,
                            acumT_ref, dtpT_ref, nb=nb, lc=lc)
        cout_ref[...] = tail

    li = lax.broadcasted_iota(jnp.int32, (r, r), 0)
    si = lax.broadcasted_iota(jnp.int32, (r, r), 1)
    mask = (li >= si) & ((li // lc) == (si // lc))
    lane_seq = lax.broadcasted_iota(jnp.int32, (GP, r), 1) // lc
    acum_t = acumT_ref[...]
    dtp_t = dtpT_ref[...]

    def branch(gg):
        bg = xact_ref[:, D_INNER + gg * N_B:D_INNER + (gg + 1) * N_B].astype(BF16)
        cg = xact_ref[:, D_INNER + GN + gg * N_B:D_INNER + GN + (gg + 1) * N_B].astype(BF16)
        cb = _dot_nt(cg, bg)
        cols = slice(gg * GP, (gg + 1) * GP)
        xs_t = xact_ref[:, cols].T
        win = 2 * SUBLANES
        for b in range(nb):
            h0 = hin_ref[b, 0]
            w0 = (b * lc // win) * win
            yo = _dot_nt(cg[w0:w0 + win], h0.astype(BF16))
            yoff_ref[b * lc:(b + 1) * lc, :] = yo[b * lc - w0:b * lc - w0 + lc]
            last = b * lc + lc - 1
            xw = jnp.concatenate(
                [xs_t[rr * P_B:(rr + 1) * P_B, :]
                 * (dtp_t[gg * R_B + rr:gg * R_B + rr + 1, :]
                    * jnp.exp2(acum_t[gg * R_B + rr:gg * R_B + rr + 1, last:last + 1]
                              - acum_t[gg * R_B + rr:gg * R_B + rr + 1, :]))
                 for rr in range(R_B)], axis=0)
            xw = jnp.where(lane_seq == b, xw, 0.0)
            inc = _dot(xw.astype(BF16), bg)
            dec = _head_rows([jnp.exp2(acum_t[gg * R_B + rr:gg * R_B + rr + 1, last:last + 1])
                              for rr in range(R_B)])
            hout_ref[b, 0] = dec * h0 + inc
        y = _ssd_heads(gg, mask, cb, yoff_ref[...], xact_ref, acum_ref, acumT_ref, dtpT_ref, dsk_ref)
        ob_ref[...] = _ssd_gate_norm(y, proj_ref[:, cols], ng_ref[:, cols])

    for gg in range(G_B):
        pl.when(g == gg)(functools.partial(branch, gg))


def _ssd_sample(proj, row0, lc, cw, cb, dtb, alog, dsk, ng, conv_state, ssm_state, layer, ob, ssm_out,
                *, nb=16):
    n = proj.shape[1]
    batch = proj.shape[0] // lc
    r = nb * lc
    blk0 = row0 // r
    sb0 = layer * (batch // nb)
    chained = ssm_out is not None
    vec = lambda w: pl.BlockSpec((1, w), lambda i, g: (0, 0))
    state_spec = pl.BlockSpec((nb, 1, GP, N_B), lambda i, g: (sb0 + i, g, 0, 0))
    hbm = pl.BlockSpec(memory_space=pl.ANY)
    args = [proj, cw, cb, dtb, alog, dsk, ng, conv_state, ssm_state, ob] + ([ssm_out] if chained else [])
    aliases = {9: 0, 10: 2} if chained else {9: 0}
    return pl.pallas_call(
        functools.partial(_ssd_sample_kernel, nb=nb, lc=lc, chained=chained),
        grid=(batch // nb, G_B),
        in_specs=[pl.BlockSpec((r, n), lambda i, g: (i, 0)),
                  pl.BlockSpec((CONV_W, CONV_DIM), lambda i, g: (0, 0)),
                  vec(CONV_DIM), vec(LANES), vec(LANES), vec(D_INNER), vec(D_INNER),
                  pl.BlockSpec((nb, CONV_W - 1, CONV_DIM), lambda i, g: (sb0 + i, 0, 0)),
                  state_spec, hbm] + ([hbm] if chained else []),
        out_specs=[pl.BlockSpec((r, GP), lambda i, g: (blk0 + i, g)),
                   pl.BlockSpec((nb, CONV_W - 1, CONV_DIM), lambda i, g: (i, 0, 0)),
                   state_spec],
        out_shape=[jax.ShapeDtypeStruct(ob.shape, BF16),
                   jax.ShapeDtypeStruct((batch, CONV_W - 1, CONV_DIM), F32),
                   jax.ShapeDtypeStruct(ssm_state.shape, F32)],
        input_output_aliases=aliases,
        scratch_shapes=[pltpu.VMEM((nb, HIST + lc, CONV_DIM), F32),
                        pltpu.VMEM((r, CONV_DIM), F32),
                        pltpu.VMEM((r, LANES), F32),
                        pltpu.VMEM((LANES, r), F32),
                        pltpu.VMEM((LANES, r), F32),
                        pltpu.VMEM((r, GP), F32)],
        compiler_params=_params(("arbitrary", "arbitrary"), 48),
        name="ssd_sample",
    )(*args)


def _mid_kernel(h_ref, oa_ref, ob_ref, x_ref, gtp_ref, gts_ref, scp_ref, scs_ref, shp_ref, shs_ref, n2_ref,
                wg_ref, wpa_ref, wpb_ref, wo_ref, rw_ref, rb_ref,
                x1_ref, h2_ref, gates_ref, idx_ref, rank_ref, cnt_ref, carry_ref, *, tok):
    d = h2_ref.shape[1]

    @pl.when(pl.program_id(0) == 0)
    def _():
        carry_ref[...] = jnp.zeros(carry_ref.shape, F32)

    gt = tok.mod_value((gtp_ref, gts_ref))
    sc = tok.mod_value((scp_ref, scs_ref))
    sh = tok.mod_value((shp_ref, shs_ref))
    for r0 in range(0, tok.tm, MID_ROWS):
        rows = slice(r0, r0 + MID_ROWS)
        _mid_rows(rows, gt[rows], sc[rows], sh[rows], h_ref, oa_ref, ob_ref, x_ref, n2_ref,
                  wg_ref, wpa_ref, wpb_ref, wo_ref, rw_ref, rb_ref,
                  x1_ref, h2_ref, gates_ref, idx_ref, rank_ref, carry_ref, d)
    cnt_ref[...] = carry_ref[...].astype(jnp.int32)


def _mid_rows(rows, gt, sc, sh, h_ref, oa_ref, ob_ref, x_ref, n2_ref, wg_ref, wpa_ref, wpb_ref, wo_ref,
              rw_ref, rb_ref, x1_ref, h2_ref, gates_ref, idx_ref, rank_ref, carry_ref, d):
    tm = rows.stop - rows.start
    gab = _dot(h_ref[rows, :], wg_ref[...])
    m = (_sigmoid(gab[:, :d]) * _dot(oa_ref[rows, :], wpa_ref[...])
         + _sigmoid(gab[:, d:]) * _dot(ob_ref[rows, :], wpb_ref[...]))
    x1 = x_ref[rows, :] + gt * _dot(m.astype(BF16), wo_ref[...])
    x1_ref[rows, :] = x1
    ms = jnp.mean(x1 * x1, axis=-1, keepdims=True)
    h2 = x1 * lax.rsqrt(ms + EPS) * n2_ref[...] * (1.0 + sc) + sh
    h2_ref[rows, :] = h2

    logits = _dot(h2.astype(BF16), rw_ref[...]) + rb_ref[...]
    lane = lax.broadcasted_iota(jnp.int32, logits.shape, 1).astype(F32)
    vals, idxs = [], []
    for _ in range(TOP_K):
        mx = jnp.max(logits, axis=-1, keepdims=True)
        ix = jnp.min(jnp.where(logits == mx, lane, float(LANES)), axis=-1, keepdims=True)
        vals.append(mx)
        idxs.append(ix)
        logits = jnp.where(lane == ix, -jnp.inf, logits)
    es = [jnp.exp(v - vals[0]) for v in vals]
    tot = es[0] + es[1] + es[2] + es[3]
    gates = jnp.zeros(logits.shape, F32)
    idx = jnp.zeros(logits.shape, F32)
    for k in range(TOP_K):
        gates = jnp.where(lane == float(k), es[k] / tot, gates)
        idx = jnp.where(lane == float(k), idxs[k], idx)
    gates_ref[rows, :] = gates
    idx_ref[rows, :] = idx.astype(jnp.int32)

    earlier = (lax.broadcasted_iota(jnp.int32, (tm, tm), 1)
               < lax.broadcasted_iota(jnp.int32, (tm, tm), 0)).astype(BF16)
    onehots = [(lane == idxs[k]).astype(F32) for k in range(TOP_K)]
    prefix = _dot(earlier, jnp.concatenate(onehots, axis=1).astype(BF16))
    base = carry_ref[...]
    rank = jnp.zeros(logits.shape, F32)
    for k in range(TOP_K):
        before = prefix[:, k * LANES:(k + 1) * LANES] + base
        rank = jnp.where(lane == float(k), jnp.sum(onehots[k] * before, axis=-1, keepdims=True), rank)
        base = base + jnp.sum(onehots[k], axis=0, keepdims=True)
    carry_ref[...] = base
    rank_ref[rows, :] = rank.astype(jnp.int32)


def _mid(tok, layer, h, oa, ob, x, mod, n2, wg, wpa, wpb, wo, rw, rb):
    t, d = h.shape
    tm = tok.tm
    row = lambda w: pl.BlockSpec((tm, w), lambda i: (i, 0))
    full = lambda a: pl.BlockSpec(a.shape, lambda i: (0,) * a.ndim, pipeline_mode=pl.Buffered(1))
    n2 = n2.reshape(1, d)
    return pl.pallas_call(
        functools.partial(_mid_kernel, tok=tok),
        grid=(tok.n_tiles,),
        in_specs=[row(d), row(d), row(2 * d), row(d),
                  *tok.mod_specs(layer, K_GT1, d), *tok.mod_specs(layer, K_SC2, d),
                  *tok.mod_specs(layer, K_SH2, d),
                  full(n2), full(wg), full(wpa), full(wpb), full(wo), full(rw), full(rb)],
        out_specs=[row(d), row(d), row(LANES), row(LANES), row(LANES),
                   pl.BlockSpec((1, LANES), lambda i: (0, 0))],
        out_shape=[jax.ShapeDtypeStruct((t, d), F32),
                   jax.ShapeDtypeStruct((t, d), F32),
                   jax.ShapeDtypeStruct((t, LANES), F32),
                   jax.ShapeDtypeStruct((t, LANES), jnp.int32),
                   jax.ShapeDtypeStruct((t, LANES), jnp.int32),
                   jax.ShapeDtypeStruct((1, LANES), jnp.int32)],
        scratch_shapes=[pltpu.VMEM((1, LANES), F32)],
        compiler_params=_params(("arbitrary",), 56),
        name="mid",
    )(h, oa, ob, x, *mod, *mod, *mod, n2, wg, wpa, wpb, wo, rw, rb)


def _dispatch_kernel(fill0_ref, filln_ref, n_used_ref, pos_ref, h_ref, xb_ref, zrow, sem, zsem):
    i = pl.program_id(0)
    tm = h_ref.shape[0]
    n_e = filln_ref.shape[0]
    blk = zrow.shape[0]
    n_blocks = xb_ref.shape[0] // blk

    for t in range(tm):
        for k in range(TOP_K):
            _row_copy(h_ref, t, xb_ref, pos_ref[t // POS_TOKENS, (t % POS_TOKENS) * TOP_K + k], sem).start(
                priority=k % 2)

    @pl.when(i == pl.num_programs(0) - 1)
    def _():
        zrow[...] = jnp.zeros(zrow.shape, F32)

        def head_rows(e):
            return jnp.minimum(filln_ref[e], (SUBLANES - fill0_ref[e] % SUBLANES) % SUBLANES)

        def piece_copy(row):
            return pltpu.make_async_copy(zrow.at[pl.ds(0, SUBLANES)],
                                         xb_ref.at[pl.ds(pl.multiple_of(row, SUBLANES), SUBLANES)], zsem)

        def fill_expert(e, c):
            head = head_rows(e)

            def fill_row(r, c2):
                _row_copy(zrow, 0, xb_ref, fill0_ref[e] + r, zsem).start()
                return c2
            lax.fori_loop(0, head, fill_row, 0)

            def fill_piece(q, c2):
                piece_copy(fill0_ref[e] + head + q * SUBLANES).start()
                return c2
            return lax.fori_loop(0, (filln_ref[e] - head) // SUBLANES, fill_piece, c)
        lax.fori_loop(0, n_e, fill_expert, 0)

        def drain_expert(e, c):
            head = head_rows(e)

            def drain_row(r, c2):
                _row_copy(zrow, 0, xb_ref, 0, zsem).wait()
                return c2
            lax.fori_loop(0, head, drain_row, 0)

            def drain_piece(q, c2):
                piece_copy(0).wait()
                return c2
            return lax.fori_loop(0, (filln_ref[e] - head) // SUBLANES, drain_piece, c)
        lax.fori_loop(0, n_e, drain_expert, 0)

        def block_copy(b):
            return pltpu.make_async_copy(zrow, xb_ref.at[pl.ds(pl.multiple_of(b * blk, blk), blk)], zsem)

        def fill_block(b, c):
            block_copy(b).start()
            return c
        lax.fori_loop(n_used_ref[0], n_blocks, fill_block, 0)

        def drain_block(b, c):
            block_copy(b).wait()
            return c
        lax.fori_loop(n_used_ref[0], n_blocks, drain_block, 0)

    for k in range(TOP_K):
        pltpu.make_async_copy(h_ref, xb_ref.at[pl.ds(0, tm)], sem).wait()


def _dispatch(h2, pos, fill0, filln, n_used, rows, *, tm, blk):
    t, d = h2.shape
    grid_spec = pltpu.PrefetchScalarGridSpec(
        num_scalar_prefetch=3,
        grid=(t // tm,),
        in_specs=[pl.BlockSpec((None, tm // POS_TOKENS, LANES), lambda i, f0, fn, nu: (i, 0, 0),
                               memory_space=pltpu.SMEM),
                  pl.BlockSpec((tm, d), lambda i, f0, fn, nu: (i, 0))],
        out_specs=pl.BlockSpec(memory_space=pl.ANY),
        scratch_shapes=[pltpu.VMEM((blk, d), F32), pltpu.SemaphoreType.DMA(()),
                        pltpu.SemaphoreType.DMA(())])
    return pl.pallas_call(
        _dispatch_kernel,
        grid_spec=grid_spec,
        out_shape=jax.ShapeDtypeStruct((rows, d), F32),
        compiler_params=_params(("arbitrary",), 32),
        name="dispatch",
    )(fill0, filln, n_used, pos, h2)


def _moe_kernel(blk_e_ref, blk_first_ref, n_used_ref, x_ref, wgu_ref, bgu_ref, wdn_ref, bdn_ref, o_ref,
                wgu_bf, wdn_bf):
    i = pl.program_id(0)
    d_ff = wdn_bf.shape[0]

    @pl.when(blk_first_ref[i] == 1)
    def _():
        wgu_bf[...] = wgu_ref[...].astype(BF16)
        wdn_bf[...] = wdn_ref[...].astype(BF16)

    @pl.when(i < n_used_ref[0])
    def _():
        gu = _dot(x_ref[...].astype(BF16), wgu_bf[...]) + bgu_ref[...]
        g = jnp.minimum(gu[:, :d_ff], SWIGLU_LIMIT)
        u = jnp.clip(gu[:, d_ff:], -SWIGLU_LIMIT, SWIGLU_LIMIT)
        act = g * _sigmoid(SWIGLU_ALPHA * g) * (u + 1.0)
        o_ref[...] = _dot(act.astype(BF16), wdn_bf[...]) + bdn_ref[...]

    @pl.when(i >= n_used_ref[0])
    def _():
        o_ref[...] = jnp.zeros(o_ref.shape, F32)


def _moe(xb, blk_e, blk_first, n_used, w_gu, b_gu, w_dn, b_dn, e0, *, tm):
    rows, d = xb.shape
    _, _, d_gu = w_gu.shape
    d_ff = w_dn.shape[1]
    expert = lambda i, be, bf, nu: (e0 + be[i], 0, 0)
    grid_spec = pltpu.PrefetchScalarGridSpec(
        num_scalar_prefetch=3,
        grid=(rows // tm,),
        in_specs=[pl.BlockSpec((tm, d), lambda i, be, bf, nu: (jnp.maximum(jnp.minimum(i, nu[0] - 1), 0), 0)),
                  pl.BlockSpec((None, d, d_gu), expert),
                  pl.BlockSpec((None, 1, d_gu), expert),
                  pl.BlockSpec((None, d_ff, d), expert),
                  pl.BlockSpec((None, 1, d), expert)],
        out_specs=pl.BlockSpec((tm, d), lambda i, be, bf, nu: (i, 0)),
        scratch_shapes=[pltpu.VMEM((d, d_gu), BF16), pltpu.VMEM((d_ff, d), BF16)])
    return pl.pallas_call(
        _moe_kernel,
        grid_spec=grid_spec,
        out_shape=jax.ShapeDtypeStruct((rows, d), F32),
        compiler_params=_params(("arbitrary",), 56),
        name="moe",
    )(blk_e, blk_first, n_used, xb, w_gu, b_gu, w_dn, b_dn)


def _route(idx, rank, counts, n_e, tm, tm_tok):
    t = idx.shape[0]
    padded = (counts + tm - 1) // tm * tm
    pad_end = jnp.cumsum(padded)
    pad_start = pad_end - padded
    experts = jnp.arange(n_e, dtype=jnp.int32)
    pos = jnp.sum(jnp.where(idx[..., None] == experts, pad_start, 0), axis=-1) + rank
    pos = pos.reshape(t // tm_tok, tm_tok // POS_TOKENS, LANES).astype(jnp.int32)
    n_blocks = -(-t * TOP_K // tm) + n_e
    blk_start = jnp.arange(n_blocks, dtype=jnp.int32) * tm
    blk_e = jnp.minimum(jnp.sum((blk_start[:, None] >= pad_end[None, :]).astype(jnp.int32), axis=1),
                        n_e - 1).astype(jnp.int32)
    blk_first = jnp.concatenate([jnp.ones((1,), jnp.int32),
                                 (blk_e[1:] != blk_e[:-1]).astype(jnp.int32)])
    n_used = (pad_end[-1:] // tm).astype(jnp.int32)
    return (pos, blk_e, blk_first, n_used, (pad_start + counts).astype(jnp.int32),
            (padded - counts).astype(jnp.int32), n_blocks * tm)


def kernel(x_prompt, x_sample, c_prompt, c_sample, state_ssm, state_conv, norm1_g, norm2_g, final_g,
           w_ada, b_ada, w_in, a_ln_g, a_ln_b, a_ws, a_bs, b_conv_w, b_conv_b, b_dt_bias, b_a_log, b_d,
           b_norm_g, w_proj_a, w_proj_b, w_out, router_w, router_b, w_gu, b_gu, w_dn, b_dn):
    bp, seq, d = x_prompt.shape
    bs, lc_s, _ = x_sample.shape
    depth = w_ada.shape[0]
    n_e = router_w.shape[-1]
    heads = b_d.shape[-1]
    moe_tm = 512
    tok = _Tokens(bp, seq, bs, lc_s, tm=256)
    tok_mid = _Tokens(bp, seq, bs, lc_s, tm=2 * MID_ROWS)
    t_p, t_s, t = tok.t_p, tok.t_s, tok.t

    d_a = d
    c0 = 2 * d_a
    c1 = c0 + D_INNER + CONV_DIM + heads
    w_in_bf = w_in.astype(BF16)
    w_uv = w_in_bf[:, :, :c0]
    w_ssd = jnp.pad(w_in_bf[:, :, c0:c1], ((0, 0), (0, 0), (0, LANES - heads)))
    w_g = w_in_bf[:, :, c1:]
    w_pa, w_pb, w_o = w_proj_a.astype(BF16), w_proj_b.astype(BF16), w_out.astype(BF16)
    rw = jnp.pad(router_w, ((0, 0), (0, 0), (0, LANES - n_e))).astype(BF16)
    rb = jnp.pad(router_b, ((0, 0), (0, LANES - n_e)), constant_values=-jnp.inf).reshape(depth, 1, LANES)
    causal = jnp.tril(jnp.ones((CHUNK, CHUNK), bool))
    ws_p = jnp.where(causal, a_ws, 0.0)
    blk = jnp.where(jnp.tril(jnp.ones((lc_s, lc_s), bool)), a_ws[:, :, :lc_s, :lc_s], 0.0)
    eye = jnp.eye(CHUNK // lc_s, dtype=F32)
    ws_s = jnp.einsum("ab,lgts->lgatbs", eye, blk).reshape(depth, G_A, CHUNK, CHUNK)
    wmix = jnp.stack([ws_p, ws_s], axis=1).astype(BF16)
    dg = d_a // G_A
    bias_p = jnp.repeat(jnp.swapaxes(a_bs, 1, 2), dg, axis=2)
    bias_s = jnp.tile(bias_p[:, :lc_s], (1, CHUNK // lc_s, 1))
    bias = jnp.stack([bias_p, bias_s], axis=1)
    dtb = jnp.pad(b_dt_bias, ((0, 0), (0, LANES - heads))).reshape(depth, 1, LANES)
    alog = jnp.pad(b_a_log, ((0, 0), (0, LANES - heads))).reshape(depth, 1, LANES)
    dsk = jnp.repeat(b_d, P_B, axis=1).reshape(depth, 1, D_INNER)
    cb = b_conv_b.reshape(depth, 1, CONV_DIM)
    ng = b_norm_g.reshape(depth, 1, D_INNER)
    w_gu_all = w_gu.reshape(depth * n_e, d, w_gu.shape[-1])
    b_gu_all = b_gu.reshape(depth * n_e, 1, b_gu.shape[-1])
    w_dn_all = w_dn.reshape(depth * n_e, w_dn.shape[-2], d)
    b_dn_all = b_dn.reshape(depth * n_e, 1, d)
    ssm_in = state_ssm.reshape(depth * bs, G_B, GP, N_B)
    conv_in = state_conv.reshape(depth * bs, CONV_W - 1, state_conv.shape[-1])

    mod_all = _ada(jnp.concatenate([c_prompt, c_sample], axis=0), w_ada, b_ada)
    mod_all = mod_all.reshape(depth, bp + bs, N_MOD, d).transpose(0, 2, 1, 3)
    mod = (mod_all[:, :, :bp], jnp.repeat(mod_all[:, :, bp:], lc_s, axis=2))

    x = jnp.concatenate([x_prompt.reshape(t_p, d), x_sample.reshape(t_s, d)], axis=0)

    h = _comb(tok, x, norm1_g[0], mod, layer_n=0)[0]
    ssm_p, conv_p, conv_s, v_s = [], [], [], []
    ssm_s = None
    y_final = None
    for l in range(depth):
        oa, v_rows = _gmlp(h, w_uv[l], a_ln_g[l], a_ln_b[l], wmix[l], bias[l], t_p)
        ob, cp, sp = _ssd_prompt(h, w_ssd[l], b_conv_w[l], cb[l], dtb[l], alog[l], dsk[l], ng[l], bp, seq)
        proj_s = _proj(h[t_p:], w_ssd[l])
        ob, cs, ssm_s = _ssd_sample(proj_s, t_p, lc_s, b_conv_w[l], cb[l], dtb[l], alog[l], dsk[l], ng[l],
                                    conv_in, ssm_in, l, ob, ssm_s)
        x1, h2, gates, idx, rank, cnt = _mid(tok_mid, l, h, oa, ob, x, mod, norm2_g[l], w_g[l], w_pa[l],
                                             w_pb[l], w_o[l], rw[l], rb[l])
        pos, blk_e, blk_first, n_used, fill0, filln, rows = _route(
            idx[:, :TOP_K], rank[:, :TOP_K], cnt[0, :n_e], n_e, moe_tm, tok.tm)
        xb = _dispatch(h2, pos, fill0, filln, n_used, rows, tm=tok.tm, blk=moe_tm)
        yb = _moe(xb, blk_e, blk_first, n_used, w_gu_all, b_gu_all, w_dn_all, b_dn_all, l * n_e, tm=moe_tm)
        if l + 1 < depth:
            x, h = _comb(tok, x1, norm1_g[l + 1], mod, layer_n=l + 1, yb=yb, pos=pos, gates=gates, layer_gt=l)
        else:
            y_final = _comb(tok, x1, final_g, mod, yb=yb, pos=pos, gates=gates, layer_gt=l)[0]
        ssm_p.append(sp.reshape(bp, heads, P_B, N_B))
        conv_p.append(cp)
        conv_s.append(cs)
        v_s.append(v_rows.reshape(bs, lc_s, d))
    return (y_final[:t_p].reshape(bp, seq, d), y_final[t_p:].reshape(bs, lc_s, d),
            jnp.stack(ssm_p), jnp.stack(conv_p), ssm_s.reshape(depth, bs, heads, P_B, N_B),
            jnp.stack(conv_s), jnp.stack(v_s))
```

```python
import functools

import jax
import jax.numpy as jnp
from jax import lax
from jax.experimental import pallas as pl
from jax.experimental.pallas import tpu as pltpu

F32 = jnp.float32
BF16 = jnp.bfloat16

EPS = 1e-6
SUBLANES = 8
LANES = 128
CHUNK = 128
G_A = 8
G_B = 8
R_B = 4
P_B = 64
N_B = 128
CONV_W = 4
TOP_K = 4
SWIGLU_LIMIT = 7.0
SWIGLU_ALPHA = 1.702
MIB = 2 ** 20
N_MOD = 6
K_SH1, K_SC1, K_GT1, K_SH2, K_SC2, K_GT2 = range(N_MOD)
POS_TOKENS = LANES // TOP_K
MID_ROWS = 256
LOG2E = 1.4426950408889634


def _params(sem, vmem_mib):
    return pltpu.CompilerParams(dimension_semantics=sem, vmem_limit_bytes=vmem_mib * MIB)


def _sigmoid(x):
    return 1.0 / (1.0 + jnp.exp(-x))


def _silu(x):
    return x * _sigmoid(x)


def _gelu(x):
    return 0.5 * x * (1.0 + lax.erf(x * (0.5 ** 0.5)))


def _softplus(x):
    return jnp.maximum(x, 0.0) + jnp.log(1.0 + jnp.exp(-jnp.abs(x)))


def _dot(a, b):
    return jnp.dot(a, b, preferred_element_type=F32)


def _dot_nt(a, b):
    return lax.dot_general(a, b, (((1,), (1,)), ((), ())), preferred_element_type=F32)


class _Tokens:
    def __init__(self, bp, seq, bs, lc_s, tm):
        self.tm = tm
        self.t_p, self.t_s = bp * seq, bs * lc_s
        self.t = self.t_p + self.t_s
        self.n_tiles = self.t // tm
        self.n_p = self.t_p // tm
        self.per_seq = seq // tm
        self.bp = bp

    def mod_specs(self, layer, k, d):
        return [pl.BlockSpec((None, None, self.bp, d), lambda i: (layer, k, 0, 0)),
                pl.BlockSpec((None, None, self.tm, d),
                             lambda i: (layer, k, jnp.maximum(i - self.n_p, 0), 0))]

    def mod_value(self, refs):
        mp_ref, ms_ref = refs
        i = pl.program_id(0)
        seq_row = mp_ref[pl.ds(jnp.minimum(i // self.per_seq, self.bp - 1), 1), :]
        return jnp.where(i < self.n_p, seq_row, ms_ref[...])


def _ada_kernel(c_ref, w_ref, b_ref, o_ref):
    c = c_ref[...]
    o_ref[...] = _dot(_silu(c).astype(BF16), w_ref[...].astype(BF16)) + b_ref[...]


def _ada(c_all, w_ada, b_ada):
    depth, d, n = w_ada.shape
    nc = c_all.shape[0]
    tn = 1536
    return pl.pallas_call(
        _ada_kernel,
        grid=(depth, n // tn),
        in_specs=[pl.BlockSpec((nc, d), lambda l, j: (0, 0)),
                  pl.BlockSpec((None, d, tn), lambda l, j: (l, 0, j)),
                  pl.BlockSpec((None, 1, tn), lambda l, j: (l, 0, j))],
        out_specs=pl.BlockSpec((None, nc, tn), lambda l, j: (l, 0, j)),
        out_shape=jax.ShapeDtypeStruct((depth, nc, n), F32),
        compiler_params=_params(("arbitrary", "arbitrary"), 40),
        name="ada",
    )(c_all, w_ada, b_ada.reshape(depth, 1, n))


def _row_copy(src_ref, src_row, dst_ref, dst_row, sem):
    return pltpu.make_async_copy(src_ref.at[pl.ds(src_row, 1)], dst_ref.at[pl.ds(dst_row, 1)], sem)


def _comb_kernel(*refs, tok, combine, final):
    it = iter(refs)
    if combine:
        pos_ref, posn_ref = next(it), next(it)
    x_ref = next(it)
    if combine:
        yb_ref, g_ref, gt_refs = next(it), next(it), (next(it), next(it))
    n_ref = next(it)
    if not final:
        sc_refs, sh_refs = (next(it), next(it)), (next(it), next(it))
    if combine and not final:
        xo_ref = next(it)
    o_ref = next(it)
    if combine:
        ybuf, sem = next(it), next(it)

    i = pl.program_id(0)
    tm = x_ref.shape[0]

    def gather_row(p_ref, s, j):
        for u in range(POS_TOKENS):
            for k in range(TOP_K):
                _row_copy(yb_ref, p_ref[j, u * TOP_K + k], ybuf.at[s, k], j * POS_TOKENS + u, sem.at[s]).start(
                    priority=k % 2)

    def drain(s):
        for k in range(TOP_K):
            pltpu.make_async_copy(yb_ref.at[pl.ds(0, tm)], ybuf.at[s, k], sem.at[s]).wait()

    def step(slot):
        if combine:
            drain(slot)
            gt = tok.mod_value(gt_refs)
            g = g_ref[...]
        if not final:
            sc, sh = tok.mod_value(sc_refs), tok.mod_value(sh_refs)
        for j in range(tm // POS_TOKENS):
            rows = slice(j * POS_TOKENS, (j + 1) * POS_TOKENS)
            xc = x_ref[rows, :]
            if combine:
                gather_row(posn_ref, 1 - slot, j)
                acc = g[rows, 0:1] * ybuf[slot, 0, rows, :]
                for k in range(1, TOP_K):
                    acc = acc + g[rows, k:k + 1] * ybuf[slot, k, rows, :]
                xc = xc + gt[rows] * acc
                if not final:
                    xo_ref[rows, :] = xc
            ms = jnp.mean(xc * xc, axis=-1, keepdims=True)
            xn = xc * lax.rsqrt(ms + EPS)
            if final:
                o_ref[rows, :] = xn * n_ref[...]
            else:
                o_ref[rows, :] = (xn * n_ref[...] * (1.0 + sc[rows]) + sh[rows]).astype(BF16)
        if combine:
            @pl.when(i == pl.num_programs(0) - 1)
            def _():
                drain(1 - slot)

    if not combine:
        step(0)
        return

    @pl.when(i == 0)
    def _():
        for j in range(tm // POS_TOKENS):
            gather_row(pos_ref, 0, j)

    for slot in range(2):
        pl.when(i % 2 == slot)(functools.partial(step, slot))


def _comb(tok, x, norm_g, mod, layer_n=None, yb=None, pos=None, gates=None, layer_gt=None):
    t, d = x.shape
    tm = tok.tm
    combine = yb is not None
    final = layer_n is None
    row = lambda w: pl.BlockSpec((tm, w), lambda i: (i, 0))

    args, specs, scratch = [], [], []
    if combine:
        args += [pos, pos]
        pos_rows = tm // POS_TOKENS
        specs += [pl.BlockSpec((None, pos_rows, LANES), lambda i: (i, 0, 0), memory_space=pltpu.SMEM),
                  pl.BlockSpec((None, pos_rows, LANES), lambda i: (jnp.minimum(i + 1, tok.n_tiles - 1), 0, 0),
                               memory_space=pltpu.SMEM)]
        scratch = [pltpu.VMEM((2, TOP_K, tm, d), F32), pltpu.SemaphoreType.DMA((2,))]
    args.append(x)
    specs.append(row(d))
    if combine:
        args += [yb, gates, *mod]
        specs += [pl.BlockSpec(memory_space=pl.ANY), row(LANES), *tok.mod_specs(layer_gt, K_GT2, d)]
    args.append(norm_g.reshape(1, d))
    specs.append(pl.BlockSpec((1, d), lambda i: (0, 0)))
    if not final:
        args += [*mod, *mod]
        specs += [*tok.mod_specs(layer_n, K_SC1, d), *tok.mod_specs(layer_n, K_SH1, d)]
    out_shape, out_specs = [], []
    if combine and not final:
        out_shape.append(jax.ShapeDtypeStruct((t, d), F32))
        out_specs.append(row(d))
    out_shape.append(jax.ShapeDtypeStruct((t, d), F32 if final else BF16))
    out_specs.append(row(d))
    return pl.pallas_call(
        functools.partial(_comb_kernel, tok=tok, combine=combine, final=final),
        grid=(tok.n_tiles,),
        in_specs=specs, out_specs=out_specs, out_shape=out_shape,
        scratch_shapes=scratch,
        compiler_params=_params(("arbitrary",), 48),
        name="comb",
    )(*args)


def _gmlp_kernel(h_ref, w_ref, lng_ref, lnb_ref, wmix_ref, bias_ref, o_ref, v_ref, *, n_prompt_tiles):
    i = pl.program_id(0)
    tm, d = o_ref.shape
    uv = _dot(h_ref[...], w_ref[...])
    u = _gelu(uv[:, :d])
    v = _gelu(uv[:, d:])
    mu = jnp.mean(v, axis=-1, keepdims=True)
    vc = v - mu
    var = jnp.mean(vc * vc, axis=-1, keepdims=True)
    vn = vc * lax.rsqrt(var + EPS) * lng_ref[...] + lnb_ref[...]

    @pl.when(i >= n_prompt_tiles)
    def _():
        v_ref[...] = vn

    vb = vn.astype(BF16)
    dg = d // G_A
    for c in range(tm // CHUNK):
        rows = slice(c * CHUNK, (c + 1) * CHUNK)
        for g in range(G_A):
            cols = slice(g * dg, (g + 1) * dg)
            s = _dot(wmix_ref[g], vb[rows, cols]) + bias_ref[:, cols]
            o_ref[rows, cols] = (u[rows, cols] * s).astype(BF16)


def _gmlp(h, w_uv, ln_g, ln_b, wmix, bias, t_prompt, *, tm=512):
    t, d = h.shape
    n_p = t_prompt // tm
    t_s = t - t_prompt
    sel = lambda i: jnp.where(i >= n_p, 1, 0)
    return pl.pallas_call(
        functools.partial(_gmlp_kernel, n_prompt_tiles=n_p),
        grid=(t // tm,),
        in_specs=[pl.BlockSpec((tm, d), lambda i: (i, 0)),
                  pl.BlockSpec((d, 2 * d), lambda i: (0, 0)),
                  pl.BlockSpec((1, d), lambda i: (0, 0)),
                  pl.BlockSpec((1, d), lambda i: (0, 0)),
                  pl.BlockSpec((None, G_A, CHUNK, CHUNK), lambda i: (sel(i), 0, 0, 0)),
                  pl.BlockSpec((None, CHUNK, d), lambda i: (sel(i), 0, 0))],
        out_specs=[pl.BlockSpec((tm, d), lambda i: (i, 0)),
                   pl.BlockSpec((tm, d), lambda i: (jnp.maximum(i - n_p, 0), 0))],
        out_shape=[jax.ShapeDtypeStruct((t, d), BF16),
                   jax.ShapeDtypeStruct((t_s, d), F32)],
        compiler_params=_params(("arbitrary",), 48),
        name="gmlp",
    )(h, w_uv, ln_g.reshape(1, d), ln_b.reshape(1, d), wmix, bias)


def _proj_kernel(x_ref, w_ref, o_ref):
    o_ref[...] = _dot(x_ref[...], w_ref[...])


def _proj(x, w, *, tm=256):
    t, d = x.shape
    n = w.shape[1]
    return pl.pallas_call(
        _proj_kernel,
        grid=(t // tm,),
        in_specs=[pl.BlockSpec((tm, d), lambda i: (i, 0)),
                  pl.BlockSpec((d, n), lambda i: (0, 0))],
        out_specs=pl.BlockSpec((tm, n), lambda i: (i, 0)),
        out_shape=jax.ShapeDtypeStruct((t, n), F32),
        compiler_params=_params(("arbitrary",), 56),
        name="inproj",
    )(x, w)


D_INNER = G_B * R_B * P_B
GN = G_B * N_B
CONV_DIM = D_INNER + 2 * GN
GP = R_B * P_B
HIST = SUBLANES


def _ssd_stage_a(proj_ref, cw_ref, cb_ref, dtb_ref, alog_ref, cs_ref, xact_ref, acum_ref, acumT_ref,
                 dtpT_ref, *, nb, lc):
    r = nb * lc
    cs_ref[:, HIST:HIST + lc, :] = proj_ref[:, D_INNER:D_INNER + CONV_DIM].reshape(nb, lc, CONV_DIM)
    for j in range(CONV_DIM // LANES):
        cols = slice(j * LANES, (j + 1) * LANES)
        full = cs_ref[:, :, cols]
        conv = cb_ref[:, cols] + full[:, HIST:, :] * cw_ref[CONV_W - 1:CONV_W, cols]
        for s in range(1, CONV_W):
            back = pltpu.roll(full, s, axis=1)[:, HIST:, :]
            conv = conv + back * cw_ref[CONV_W - 1 - s:CONV_W - s, cols]
        xact_ref[:, cols] = _silu(conv).reshape(r, LANES)
    tail = cs_ref[:, HIST + lc - 3:HIST + lc, :]

    dt = proj_ref[:, D_INNER + CONV_DIM:D_INNER + CONV_DIM + LANES]
    dtp = _softplus(dt + dtb_ref[...])
    acum = dtp * (-jnp.exp(alog_ref[...]) * LOG2E)
    local = lax.broadcasted_iota(jnp.int32, (r, LANES), 0) % lc
    sh = 1
    while sh < lc:
        acum = acum + jnp.where(local >= sh, pltpu.roll(acum, sh, axis=0), 0.0)
        sh *= 2
    acum_ref[...] = acum
    acumT_ref[...] = acum.T
    dtpT_ref[...] = dtp.T
    return tail


def _ssd_heads(g, mask, cb, yoff, xact_ref, acum_ref, acumT_ref, dtpT_ref, dsk_ref):
    r = xact_ref.shape[0]
    first = lax.broadcasted_iota(jnp.int32, (r, LANES), 1) < P_B
    ys = []
    for pair in range(R_B // 2):
        c0 = g * GP + pair * LANES
        xs = xact_ref[:, c0:c0 + LANES]
        xs_bf = xs.astype(BF16)
        colbs, ds = [], []
        for rr in range(2):
            hh = g * R_B + pair * 2 + rr
            colb = jnp.broadcast_to(acum_ref[:, hh:hh + 1], (r, r))
            seg = colb - acumT_ref[hh:hh + 1, :]
            m = cb * jnp.exp2(jnp.where(mask, seg, -jnp.inf)) * dtpT_ref[hh:hh + 1, :]
            colbs.append(colb)
            ds.append(_dot(m.astype(BF16), xs_bf))
        y = jnp.where(first, ds[0], ds[1])
        y = y + jnp.exp2(jnp.where(first, colbs[0], colbs[1])) * yoff[:, pair * LANES:(pair + 1) * LANES]
        y = y + dsk_ref[:, c0:c0 + LANES] * xs
        ys.append(y)
    return jnp.concatenate(ys, axis=1)


def _ssd_gate_norm(y, z, ng):
    yz = y * _silu(z)
    ms = jnp.mean(yz * yz, axis=-1, keepdims=True)
    return (yz * lax.rsqrt(ms + EPS) * ng).astype(BF16)


def _head_rows(vals):
    return jnp.concatenate([jnp.broadcast_to(v, (P_B, N_B)) for v in vals], axis=0)


def _ssd_prompt_kernel(h_ref, w_ref, cw_ref, cb_ref, dtb_ref, alog_ref, dsk_ref, ng_ref,
                       ob_ref, cout_ref, hout_ref,
                       proj_ref, cs_ref, hs_ref, xact_ref, acum_ref, acumT_ref, dtpT_ref):
    c = pl.program_id(1)
    lc = CHUNK

    @pl.when(c == 0)
    def _():
        cs_ref[:, 0:HIST, :] = jnp.zeros((1, HIST, CONV_DIM), F32)
        hs_ref[...] = jnp.zeros(hs_ref.shape, F32)

    proj_ref[...] = _dot(h_ref[...], w_ref[...])

    tail = _ssd_stage_a(proj_ref, cw_ref, cb_ref, dtb_ref, alog_ref, cs_ref, xact_ref, acum_ref,
                        acumT_ref, dtpT_ref, nb=1, lc=lc)
    cs_ref[:, HIST - 3:HIST, :] = tail

    @pl.when(c == pl.num_programs(1) - 1)
    def _():
        cout_ref[...] = tail[0]

    li = lax.broadcasted_iota(jnp.int32, (lc, lc), 0)
    si = lax.broadcasted_iota(jnp.int32, (lc, lc), 1)
    mask = li >= si
    acum_t = acumT_ref[...]
    wt = dtpT_ref[...] * jnp.exp2(acum_t[:, lc - 1:lc] - acum_t)
    for g in range(G_B):
        bg = xact_ref[:, D_INNER + g * N_B:D_INNER + (g + 1) * N_B].astype(BF16)
        cg = xact_ref[:, D_INNER + GN + g * N_B:D_INNER + GN + (g + 1) * N_B].astype(BF16)
        cb = _dot_nt(cg, bg)
        hg = hs_ref[g]
        yoff = _dot_nt(cg, hg.astype(BF16))
        y = _ssd_heads(g, mask, cb, yoff, xact_ref, acum_ref, acumT_ref, dtpT_ref, dsk_ref)
        cols = slice(g * GP, (g + 1) * GP)
        ob_ref[:, cols] = _ssd_gate_norm(y, proj_ref[:, cols], ng_ref[:, cols])
        xs_t = xact_ref[:, cols].T
        xw = jnp.concatenate(
            [xs_t[rr * P_B:(rr + 1) * P_B, :] * wt[g * R_B + rr:g * R_B + rr + 1, :] for rr in range(R_B)],
            axis=0)
        inc = _dot(xw.astype(BF16), bg)
        dec = _head_rows([jnp.exp2(acum_t[g * R_B + rr:g * R_B + rr + 1, lc - 1:lc]) for rr in range(R_B)])
        hs_ref[g] = dec * hg + inc

    @pl.when(c == pl.num_programs(1) - 1)
    def _():
        hout_ref[...] = hs_ref[...]


def _ssd_prompt(h, w, cw, cb, dtb, alog, dsk, ng, batch, seq):
    t, d = h.shape
    n = w.shape[1]
    nc = seq // CHUNK
    vec = lambda w: pl.BlockSpec((1, w), lambda b, c: (0, 0))
    return pl.pallas_call(
        _ssd_prompt_kernel,
        grid=(batch, nc),
        in_specs=[pl.BlockSpec((CHUNK, d), lambda b, c: (b * nc + c, 0)),
                  pl.BlockSpec((d, n), lambda b, c: (0, 0)),
                  pl.BlockSpec((CONV_W, CONV_DIM), lambda b, c: (0, 0)),
                  vec(CONV_DIM), vec(LANES), vec(LANES), vec(D_INNER), vec(D_INNER)],
        out_specs=[pl.BlockSpec((CHUNK, D_INNER), lambda b, c: (b * nc + c, 0)),
                   pl.BlockSpec((None, CONV_W - 1, CONV_DIM), lambda b, c: (b, 0, 0)),
                   pl.BlockSpec((None, G_B, GP, N_B), lambda b, c: (b, 0, 0, 0))],
        out_shape=[jax.ShapeDtypeStruct((t, D_INNER), BF16),
                   jax.ShapeDtypeStruct((batch, CONV_W - 1, CONV_DIM), F32),
                   jax.ShapeDtypeStruct((batch, G_B, GP, N_B), F32)],
        scratch_shapes=[pltpu.VMEM((CHUNK, n), F32),
                        pltpu.VMEM((1, HIST + CHUNK, CONV_DIM), F32),
                        pltpu.VMEM((G_B, GP, N_B), F32),
                        pltpu.VMEM((CHUNK, CONV_DIM), F32),
                        pltpu.VMEM((CHUNK, LANES), F32),
                        pltpu.VMEM((LANES, CHUNK), F32),
                        pltpu.VMEM((LANES, CHUNK), F32)],
        compiler_params=_params(("arbitrary", "arbitrary"), 56),
        name="ssd_prompt",
    )(h, w, cw, cb, dtb, alog, dsk, ng)


def _ssd_sample_kernel(*refs, nb, lc, chained):
    (proj_ref, cw_ref, cb_ref, dtb_ref, alog_ref, dsk_ref, ng_ref, cin_ref, hin_ref) = refs[:9]
    n_alias = 2 if chained else 1
    (ob_ref, cout_ref, hout_ref,
     cs_ref, xact_ref, acum_ref, acumT_ref, dtpT_ref, yoff_ref) = refs[9 + n_alias:]
    g = pl.program_id(1)
    r = nb * lc

    @pl.when(g == 0)
    def _():
        cs_ref[:, HIST - 3:HIST, :] = cin_ref[...]
        tail = _ssd_stage_a(proj_ref, cw_ref, cb_ref, dtb_ref, alog_ref, cs_ref, xact_ref, acum_ref,
                            acumT_ref, dtpT_ref, nb=nb, lc=lc)
        cout_ref[...] = tail

    li = lax.broadcasted_iota(jnp.int32, (r, r), 0)
    si = lax.broadcasted_iota(jnp.int32, (r, r), 1)
    mask = (li >= si) & ((li // lc) == (si // lc))
    lane_seq = lax.broadcasted_iota(jnp.int32, (GP, r), 1) // lc
    acum_t = acumT_ref[...]
    dtp_t = dtpT_ref[...]

    def branch(gg):
        bg = xact_ref[:, D_INNER + gg * N_B:D_INNER + (gg + 1) * N_B].astype(BF16)
        cg = xact_ref[:, D_INNER + GN + gg * N_B:D_INNER + GN + (gg + 1) * N_B].astype(BF16)
        cb = _dot_nt(cg, bg)
        cols = slice(gg * GP, (gg + 1) * GP)
        xs_t = xact_ref[:, cols].T
        win = 2 * SUBLANES
        for b in range(nb):
            h0 = hin_ref[b, 0]
            w0 = (b * lc // win) * win
            yo = _dot_nt(cg[w0:w0 + win], h0.astype(BF16))
            yoff_ref[b * lc:(b + 1) * lc, :] = yo[b * lc - w0:b * lc - w0 + lc]
            last = b * lc + lc - 1
            xw = jnp.concatenate(
                [xs_t[rr * P_B:(rr + 1) * P_B, :]
                 * (dtp_t[gg * R_B + rr:gg * R_B + rr + 1, :]
                    * jnp.exp2(acum_t[gg * R_B + rr:gg * R_B + rr + 1, last:last + 1]
                              - acum_t[gg * R_B + rr:gg * R_B + rr + 1, :]))
                 for rr in range(R_B)], axis=0)
            xw = jnp.where(lane_seq == b, xw, 0.0)
            inc = _dot(xw.astype(BF16), bg)
            dec = _head_rows([jnp.exp2(acum_t[gg * R_B + rr:gg * R_B + rr + 1, last:last + 1])
                              for rr in range(R_B)])
            hout_ref[b, 0] = dec * h0 + inc
        y = _ssd_heads(gg, mask, cb, yoff_ref[...], xact_ref, acum_ref, acumT_ref, dtpT_ref, dsk_ref)
        ob_ref[...] = _ssd_gate_norm(y, proj_ref[:, cols], ng_ref[:, cols])

    for gg in range(G_B):
        pl.when(g == gg)(functools.partial(branch, gg))


def _ssd_sample(proj, row0, lc, cw, cb, dtb, alog, dsk, ng, conv_state, ssm_state, layer, ob, ssm_out,
                *, nb=16):
    n = proj.shape[1]
    batch = proj.shape[0] // lc
    r = nb * lc
    blk0 = row0 // r
    sb0 = layer * (batch // nb)
    chained = ssm_out is not None
    vec = lambda w: pl.BlockSpec((1, w), lambda i, g: (0, 0))
    state_spec = pl.BlockSpec((nb, 1, GP, N_B), lambda i, g: (sb0 + i, g, 0, 0))
    hbm = pl.BlockSpec(memory_space=pl.ANY)
    args = [proj, cw, cb, dtb, alog, dsk, ng, conv_state, ssm_state, ob] + ([ssm_out] if chained else [])
    aliases = {9: 0, 10: 2} if chained else {9: 0}
    return pl.pallas_call(
        functools.partial(_ssd_sample_kernel, nb=nb, lc=lc, chained=chained),
        grid=(batch // nb, G_B),
        in_specs=[pl.BlockSpec((r, n), lambda i, g: (i, 0)),
                  pl.BlockSpec((CONV_W, CONV_DIM), lambda i, g: (0, 0)),
                  vec(CONV_DIM), vec(LANES), vec(LANES), vec(D_INNER), vec(D_INNER),
                  pl.BlockSpec((nb, CONV_W - 1, CONV_DIM), lambda i, g: (sb0 + i, 0, 0)),
                  state_spec, hbm] + ([hbm] if chained else []),
        out_specs=[pl.BlockSpec((r, GP), lambda i, g: (blk0 + i, g)),
                   pl.BlockSpec((nb, CONV_W - 1, CONV_DIM), lambda i, g: (i, 0, 0)),
                   state_spec],
        out_shape=[jax.ShapeDtypeStruct(ob.shape, BF16),
                   jax.ShapeDtypeStruct((batch, CONV_W - 1, CONV_DIM), F32),
                   jax.ShapeDtypeStruct(ssm_state.shape, F32)],
        input_output_aliases=aliases,
        scratch_shapes=[pltpu.VMEM((nb, HIST + lc, CONV_DIM), F32),
                        pltpu.VMEM((r, CONV_DIM), F32),
                        pltpu.VMEM((r, LANES), F32),
                        pltpu.VMEM((LANES, r), F32),
                        pltpu.VMEM((LANES, r), F32),
                        pltpu.VMEM((r, GP), F32)],
        compiler_params=_params(("arbitrary", "arbitrary"), 48),
        name="ssd_sample",
    )(*args)


def _mid_kernel(h_ref, oa_ref, ob_ref, x_ref, gtp_ref, gts_ref, scp_ref, scs_ref, shp_ref, shs_ref, n2_ref,
                wg_ref, wpa_ref, wpb_ref, wo_ref, rw_ref, rb_ref,
                x1_ref, h2_ref, gates_ref, idx_ref, rank_ref, cnt_ref, carry_ref, *, tok):
    d = h2_ref.shape[1]

    @pl.when(pl.program_id(0) == 0)
    def _():
        carry_ref[...] = jnp.zeros(carry_ref.shape, F32)

    gt = tok.mod_value((gtp_ref, gts_ref))
    sc = tok.mod_value((scp_ref, scs_ref))
    sh = tok.mod_value((shp_ref, shs_ref))
    for r0 in range(0, tok.tm, MID_ROWS):
        rows = slice(r0, r0 + MID_ROWS)
        _mid_rows(rows, gt[rows], sc[rows], sh[rows], h_ref, oa_ref, ob_ref, x_ref, n2_ref,
                  wg_ref, wpa_ref, wpb_ref, wo_ref, rw_ref, rb_ref,
                  x1_ref, h2_ref, gates_ref, idx_ref, rank_ref, carry_ref, d)
    cnt_ref[...] = carry_ref[...].astype(jnp.int32)


def _mid_rows(rows, gt, sc, sh, h_ref, oa_ref, ob_ref, x_ref, n2_ref, wg_ref, wpa_ref, wpb_ref, wo_ref,
              rw_ref, rb_ref, x1_ref, h2_ref, gates_ref, idx_ref, rank_ref, carry_ref, d):
    tm = rows.stop - rows.start
    gab = _dot(h_ref[rows, :], wg_ref[...])
    m = (_sigmoid(gab[:, :d]) * _dot(oa_ref[rows, :], wpa_ref[...])
         + _sigmoid(gab[:, d:]) * _dot(ob_ref[rows, :], wpb_ref[...]))
    x1 = x_ref[rows, :] + gt * _dot(m.astype(BF16), wo_ref[...])
    x1_ref[rows, :] = x1
    ms = jnp.mean(x1 * x1, axis=-1, keepdims=True)
    h2 = x1 * lax.rsqrt(ms + EPS) * n2_ref[...] * (1.0 + sc) + sh
    h2_ref[rows, :] = h2

    logits = _dot(h2.astype(BF16), rw_ref[...]) + rb_ref[...]
    lane = lax.broadcasted_iota(jnp.int32, logits.shape, 1).astype(F32)
    vals, idxs = [], []
    for _ in range(TOP_K):
        mx = jnp.max(logits, axis=-1, keepdims=True)
        ix = jnp.min(jnp.where(logits == mx, lane, float(LANES)), axis=-1, keepdims=True)
        vals.append(mx)
        idxs.append(ix)
        logits = jnp.where(lane == ix, -jnp.inf, logits)
    es = [jnp.exp(v - vals[0]) for v in vals]
    tot = es[0] + es[1] + es[2] + es[3]
    gates = jnp.zeros(logits.shape, F32)
    idx = jnp.zeros(logits.shape, F32)
    for k in range(TOP_K):
        gates = jnp.where(lane == float(k), es[k] / tot, gates)
        idx = jnp.where(lane == float(k), idxs[k], idx)
    gates_ref[rows, :] = gates
    idx_ref[rows, :] = idx.astype(jnp.int32)

    earlier = (lax.broadcasted_iota(jnp.int32, (tm, tm), 1)
               < lax.broadcasted_iota(jnp.int32, (tm, tm), 0)).astype(BF16)
    onehots = [(lane == idxs[k]).astype(F32) for k in range(TOP_K)]
    prefix = _dot(earlier, jnp.concatenate(onehots, axis=1).astype(BF16))
    base = carry_ref[...]
    rank = jnp.zeros(logits.shape, F32)
    for k in range(TOP_K):
        before = prefix[:, k * LANES:(k + 1) * LANES] + base
        rank = jnp.where(lane == float(k), jnp.sum(onehots[k] * before, axis=-1, keepdims=True), rank)
        base = base + jnp.sum(onehots[k], axis=0, keepdims=True)
    carry_ref[...] = base
    rank_ref[rows, :] = rank.astype(jnp.int32)


def _mid(tok, layer, h, oa, ob, x, mod, n2, wg, wpa, wpb, wo, rw, rb):
    t, d = h.shape
    tm = tok.tm
    row = lambda w: pl.BlockSpec((tm, w), lambda i: (i, 0))
    full = lambda a: pl.BlockSpec(a.shape, lambda i: (0,) * a.ndim, pipeline_mode=pl.Buffered(1))
    n2 = n2.reshape(1, d)
    return pl.pallas_call(
        functools.partial(_mid_kernel, tok=tok),
        grid=(tok.n_tiles,),
        in_specs=[row(d), row(d), row(2 * d), row(d),
                  *tok.mod_specs(layer, K_GT1, d), *tok.mod_specs(layer, K_SC2, d),
                  *tok.mod_specs(layer, K_SH2, d),
                  full(n2), full(wg), full(wpa), full(wpb), full(wo), full(rw), full(rb)],
        out_specs=[row(d), row(d), row(LANES), row(LANES), row(LANES),
                   pl.BlockSpec((1, LANES), lambda i: (0, 0))],
        out_shape=[jax.ShapeDtypeStruct((t, d), F32),
                   jax.ShapeDtypeStruct((t, d), F32),
                   jax.ShapeDtypeStruct((t, LANES), F32),
                   jax.ShapeDtypeStruct((t, LANES), jnp.int32),
                   jax.ShapeDtypeStruct((t, LANES), jnp.int32),
                   jax.ShapeDtypeStruct((1, LANES), jnp.int32)],
        scratch_shapes=[pltpu.VMEM((1, LANES), F32)],
        compiler_params=_params(("arbitrary",), 56),
        name="mid",
    )(h, oa, ob, x, *mod, *mod, *mod, n2, wg, wpa, wpb, wo, rw, rb)


def _dispatch_kernel(fill0_ref, filln_ref, n_used_ref, pos_ref, h_ref, xb_ref, zrow, sem, zsem):
    i = pl.program_id(0)
    tm = h_ref.shape[0]
    n_e = filln_ref.shape[0]
    blk = zrow.shape[0]
    n_blocks = xb_ref.shape[0] // blk

    for t in range(tm):
        for k in range(TOP_K):
            _row_copy(h_ref, t, xb_ref, pos_ref[t // POS_TOKENS, (t % POS_TOKENS) * TOP_K + k], sem).start(
                priority=k % 2)

    @pl.when(i == pl.num_programs(0) - 1)
    def _():
        zrow[...] = jnp.zeros(zrow.shape, F32)

        def head_rows(e):
            return jnp.minimum(filln_ref[e], (SUBLANES - fill0_ref[e] % SUBLANES) % SUBLANES)

        def piece_copy(row):
            return pltpu.make_async_copy(zrow.at[pl.ds(0, SUBLANES)],
                                         xb_ref.at[pl.ds(pl.multiple_of(row, SUBLANES), SUBLANES)], zsem)

        def fill_expert(e, c):
            head = head_rows(e)

            def fill_row(r, c2):
                _row_copy(zrow, 0, xb_ref, fill0_ref[e] + r, zsem).start()
                return c2
            lax.fori_loop(0, head, fill_row, 0)

            def fill_piece(q, c2):
                piece_copy(fill0_ref[e] + head + q * SUBLANES).start()
                return c2
            return lax.fori_loop(0, (filln_ref[e] - head) // SUBLANES, fill_piece, c)
        lax.fori_loop(0, n_e, fill_expert, 0)

        def drain_expert(e, c):
            head = head_rows(e)

            def drain_row(r, c2):
                _row_copy(zrow, 0, xb_ref, 0, zsem).wait()
                return c2
            lax.fori_loop(0, head, drain_row, 0)

            def drain_piece(q, c2):
                piece_copy(0).wait()
                return c2
            return lax.fori_loop(0, (filln_ref[e] - head) // SUBLANES, drain_piece, c)
        lax.fori_loop(0, n_e, drain_expert, 0)

        def block_copy(b):
            return pltpu.make_async_copy(zrow, xb_ref.at[pl.ds(pl.multiple_of(b * blk, blk), blk)], zsem)

        def fill_block(b, c):
            block_copy(b).start()
            return c
        lax.fori_loop(n_used_ref[0], n_blocks, fill_block, 0)

        def drain_block(b, c):
            block_copy(b).wait()
            return c
        lax.fori_loop(n_used_ref[0], n_blocks, drain_block, 0)

    for k in range(TOP_K):
        pltpu.make_async_copy(h_ref, xb_ref.at[pl.ds(0, tm)], sem).wait()


def _dispatch(h2, pos, fill0, filln, n_used, rows, *, tm, blk):
    t, d = h2.shape
    grid_spec = pltpu.PrefetchScalarGridSpec(
        num_scalar_prefetch=3,
        grid=(t // tm,),
        in_specs=[pl.BlockSpec((None, tm // POS_TOKENS, LANES), lambda i, f0, fn, nu: (i, 0, 0),
                               memory_space=pltpu.SMEM),
                  pl.BlockSpec((tm, d), lambda i, f0, fn, nu: (i, 0))],
        out_specs=pl.BlockSpec(memory_space=pl.ANY),
        scratch_shapes=[pltpu.VMEM((blk, d), F32), pltpu.SemaphoreType.DMA(()),
                        pltpu.SemaphoreType.DMA(())])
    return pl.pallas_call(
        _dispatch_kernel,
        grid_spec=grid_spec,
        out_shape=jax.ShapeDtypeStruct((rows, d), F32),
        compiler_params=_params(("arbitrary",), 32),
        name="dispatch",
    )(fill0, filln, n_used, pos, h2)


def _moe_kernel(blk_e_ref, blk_first_ref, n_used_ref, blk_nxt_ref, blk_slot_ref,
                x_ref, wgu_hbm, bgu_ref, wdn_hbm, bdn_ref, o_ref,
                wgu_buf, wdn_buf, wsem, wgu_bf, wdn_bf, *, e0):
    i = pl.program_id(0)
    d_ff = wdn_bf.shape[0]

    def weight_copies(e, s):
        return (pltpu.make_async_copy(wgu_hbm.at[e0 + e], wgu_buf.at[s], wsem.at[0, s]),
                pltpu.make_async_copy(wdn_hbm.at[e0 + e], wdn_buf.at[s], wsem.at[1, s]))

    @pl.when(i == 0)
    def _():
        for c in weight_copies(blk_e_ref[0], blk_slot_ref[0]):
            c.start()

    @pl.when(blk_first_ref[i] == 1)
    def _():
        s = blk_slot_ref[i]
        for c in weight_copies(blk_e_ref[i], s):
            c.wait()

        @pl.when(blk_nxt_ref[i] >= 0)
        def _():
            for c in weight_copies(blk_nxt_ref[i], 1 - s):
                c.start()

        wgu_bf[...] = wgu_buf[s].astype(BF16)
        wdn_bf[...] = wdn_buf[s].astype(BF16)

    @pl.when(i < n_used_ref[0])
    def _():
        gu = _dot(x_ref[...].astype(BF16), wgu_bf[...]) + bgu_ref[...]
        g = jnp.minimum(gu[:, :d_ff], SWIGLU_LIMIT)
        u = jnp.clip(gu[:, d_ff:], -SWIGLU_LIMIT, SWIGLU_LIMIT)
        act = g * _sigmoid(SWIGLU_ALPHA * g) * (u + 1.0)
        o_ref[...] = _dot(act.astype(BF16), wdn_bf[...]) + bdn_ref[...]

    @pl.when(i >= n_used_ref[0])
    def _():
        o_ref[...] = jnp.zeros(o_ref.shape, F32)


def _moe(xb, blk_e, blk_first, n_used, blk_nxt, blk_slot, w_gu, b_gu, w_dn, b_dn, e0, *, tm):
    rows, d = xb.shape
    _, _, d_gu = w_gu.shape
    d_ff = w_dn.shape[1]
    expert = lambda i, be, bf, nu, bn, bs: (e0 + be[i], 0, 0)
    hbm = pl.BlockSpec(memory_space=pl.ANY)
    grid_spec = pltpu.PrefetchScalarGridSpec(
        num_scalar_prefetch=5,
        grid=(rows // tm,),
        in_specs=[pl.BlockSpec((tm, d),
                               lambda i, be, bf, nu, bn, bs: (jnp.maximum(jnp.minimum(i, nu[0] - 1), 0), 0)),
                  hbm, pl.BlockSpec((None, 1, d_gu), expert),
                  hbm, pl.BlockSpec((None, 1, d), expert)],
        out_specs=pl.BlockSpec((tm, d), lambda i, be, bf, nu, bn, bs: (i, 0)),
        scratch_shapes=[pltpu.VMEM((2, d, d_gu), F32), pltpu.VMEM((2, d_ff, d), F32),
                        pltpu.SemaphoreType.DMA((2, 2)),
                        pltpu.VMEM((d, d_gu), BF16), pltpu.VMEM((d_ff, d), BF16)])
    return pl.pallas_call(
        functools.partial(_moe_kernel, e0=e0),
        grid_spec=grid_spec,
        out_shape=jax.ShapeDtypeStruct((rows, d), F32),
        compiler_params=_params(("arbitrary",), 56),
        name="moe",
    )(blk_e, blk_first, n_used, blk_nxt, blk_slot, xb, w_gu, b_gu, w_dn, b_dn)


def _route(idx, rank, counts, n_e, tm, tm_tok):
    t = idx.shape[0]
    padded = (counts + tm - 1) // tm * tm
    pad_end = jnp.cumsum(padded)
    pad_start = pad_end - padded
    experts = jnp.arange(n_e, dtype=jnp.int32)
    pos = jnp.sum(jnp.where(idx[..., None] == experts, pad_start, 0), axis=-1) + rank
    pos = pos.reshape(t // tm_tok, tm_tok // POS_TOKENS, LANES).astype(jnp.int32)
    n_blocks = -(-t * TOP_K // tm) + n_e
    blk_start = jnp.arange(n_blocks, dtype=jnp.int32) * tm
    blk_e = jnp.minimum(jnp.sum((blk_start[:, None] >= pad_end[None, :]).astype(jnp.int32), axis=1),
                        n_e - 1).astype(jnp.int32)
    blk_first = jnp.concatenate([jnp.ones((1,), jnp.int32),
                                 (blk_e[1:] != blk_e[:-1]).astype(jnp.int32)])
    n_used = (pad_end[-1:] // tm).astype(jnp.int32)
    blk_first = blk_first * (jnp.arange(n_blocks, dtype=jnp.int32) < n_used[0]).astype(jnp.int32)
    live = counts > 0
    later = live[None, :] & (experts[None, :] > experts[:, None])
    next_e = jnp.min(jnp.where(later, experts[None, :], n_e), axis=1)
    next_e = jnp.where(next_e == n_e, -1, next_e)
    slot_e = (jnp.cumsum(live.astype(jnp.int32)) - 1) % 2
    onehot = blk_e[:, None] == experts[None, :]
    blk_nxt = jnp.sum(jnp.where(onehot, next_e[None, :], 0), axis=1).astype(jnp.int32)
    blk_slot = jnp.sum(jnp.where(onehot, slot_e[None, :], 0), axis=1).astype(jnp.int32)
    return (pos, blk_e, blk_first, n_used, blk_nxt, blk_slot, (pad_start + counts).astype(jnp.int32),
            (padded - counts).astype(jnp.int32), n_blocks * tm)


def kernel(x_prompt, x_sample, c_prompt, c_sample, state_ssm, state_conv, norm1_g, norm2_g, final_g,
           w_ada, b_ada, w_in, a_ln_g, a_ln_b, a_ws, a_bs, b_conv_w, b_conv_b, b_dt_bias, b_a_log, b_d,
           b_norm_g, w_proj_a, w_proj_b, w_out, router_w, router_b, w_gu, b_gu, w_dn, b_dn):
    bp, seq, d = x_prompt.shape
    bs, lc_s, _ = x_sample.shape
    depth = w_ada.shape[0]
    n_e = router_w.shape[-1]
    heads = b_d.shape[-1]
    moe_tm = 512
    tok = _Tokens(bp, seq, bs, lc_s, tm=256)
    tok_mid = _Tokens(bp, seq, bs, lc_s, tm=2 * MID_ROWS)
    t_p, t_s, t = tok.t_p, tok.t_s, tok.t

    d_a = d
    c0 = 2 * d_a
    c1 = c0 + D_INNER + CONV_DIM + heads
    w_in_bf = w_in.astype(BF16)
    w_uv = w_in_bf[:, :, :c0]
    w_ssd = jnp.pad(w_in_bf[:, :, c0:c1], ((0, 0), (0, 0), (0, LANES - heads)))
    w_g = w_in_bf[:, :, c1:]
    w_pa, w_pb, w_o = w_proj_a.astype(BF16), w_proj_b.astype(BF16), w_out.astype(BF16)
    rw = jnp.pad(router_w, ((0, 0), (0, 0), (0, LANES - n_e))).astype(BF16)
    rb = jnp.pad(router_b, ((0, 0), (0, LANES - n_e)), constant_values=-jnp.inf).reshape(depth, 1, LANES)
    causal = jnp.tril(jnp.ones((CHUNK, CHUNK), bool))
    ws_p = jnp.where(causal, a_ws, 0.0)
    blk = jnp.where(jnp.tril(jnp.ones((lc_s, lc_s), bool)), a_ws[:, :, :lc_s, :lc_s], 0.0)
    eye = jnp.eye(CHUNK // lc_s, dtype=F32)
    ws_s = jnp.einsum("ab,lgts->lgatbs", eye, blk).reshape(depth, G_A, CHUNK, CHUNK)
    wmix = jnp.stack([ws_p, ws_s], axis=1).astype(BF16)
    dg = d_a // G_A
    bias_p = jnp.repeat(jnp.swapaxes(a_bs, 1, 2), dg, axis=2)
    bias_s = jnp.tile(bias_p[:, :lc_s], (1, CHUNK // lc_s, 1))
    bias = jnp.stack([bias_p, bias_s], axis=1)
    dtb = jnp.pad(b_dt_bias, ((0, 0), (0, LANES - heads))).reshape(depth, 1, LANES)
    alog = jnp.pad(b_a_log, ((0, 0), (0, LANES - heads))).reshape(depth, 1, LANES)
    dsk = jnp.repeat(b_d, P_B, axis=1).reshape(depth, 1, D_INNER)
    cb = b_conv_b.reshape(depth, 1, CONV_DIM)
    ng = b_norm_g.reshape(depth, 1, D_INNER)
    w_gu_all = w_gu.reshape(depth * n_e, d, w_gu.shape[-1])
    b_gu_all = b_gu.reshape(depth * n_e, 1, b_gu.shape[-1])
    w_dn_all = w_dn.reshape(depth * n_e, w_dn.shape[-2], d)
    b_dn_all = b_dn.reshape(depth * n_e, 1, d)
    ssm_in = state_ssm.reshape(depth * bs, G_B, GP, N_B)
    conv_in = state_conv.reshape(depth * bs, CONV_W - 1, state_conv.shape[-1])

    mod_all = _ada(jnp.concatenate([c_prompt, c_sample], axis=0), w_ada, b_ada)
    mod_all = mod_all.reshape(depth, bp + bs, N_MOD, d).transpose(0, 2, 1, 3)
    mod = (mod_all[:, :, :bp], jnp.repeat(mod_all[:, :, bp:], lc_s, axis=2))

    x = jnp.concatenate([x_prompt.reshape(t_p, d), x_sample.reshape(t_s, d)], axis=0)

    h = _comb(tok, x, norm1_g[0], mod, layer_n=0)[0]
    ssm_p, conv_p, conv_s, v_s = [], [], [], []
    ssm_s = None
    y_final = None
    for l in range(depth):
        oa, v_rows = _gmlp(h, w_uv[l], a_ln_g[l], a_ln_b[l], wmix[l], bias[l], t_p)
        ob, cp, sp = _ssd_prompt(h, w_ssd[l], b_conv_w[l], cb[l], dtb[l], alog[l], dsk[l], ng[l], bp, seq)
        proj_s = _proj(h[t_p:], w_ssd[l])
        ob, cs, ssm_s = _ssd_sample(proj_s, t_p, lc_s, b_conv_w[l], cb[l], dtb[l], alog[l], dsk[l], ng[l],
                                    conv_in, ssm_in, l, ob, ssm_s)
        x1, h2, gates, idx, rank, cnt = _mid(tok_mid, l, h, oa, ob, x, mod, norm2_g[l], w_g[l], w_pa[l],
                                             w_pb[l], w_o[l], rw[l], rb[l])
        pos, blk_e, blk_first, n_used, blk_nxt, blk_slot, fill0, filln, rows = _route(
            idx[:, :TOP_K], rank[:, :TOP_K], cnt[0, :n_e], n_e, moe_tm, tok.tm)
        xb = _dispatch(h2, pos, fill0, filln, n_used, rows, tm=tok.tm, blk=moe_tm)
        yb = _moe(xb, blk_e, blk_first, n_used, blk_nxt, blk_slot, w_gu_all, b_gu_all, w_dn_all, b_dn_all,
                  l * n_e, tm=moe_tm)
        if l + 1 < depth:
            x, h = _comb(tok, x1, norm1_g[l + 1], mod, layer_n=l + 1, yb=yb, pos=pos, gates=gates, layer_gt=l)
        else:
            y_final = _comb(tok, x1, final_g, mod, yb=yb, pos=pos, gates=gates, layer_gt=l)[0]
        ssm_p.append(sp.reshape(bp, heads, P_B, N_B))
        conv_p.append(cp)
        conv_s.append(cs)
        v_s.append(v_rows.reshape(bs, lc_s, d))
    return (y_final[:t_p].reshape(bp, seq, d), y_final[t_p:].reshape(bs, lc_s, d),
            jnp.stack(ssm_p), jnp.stack(conv_p), ssm_s.reshape(depth, bs, heads, P_B, N_B),
            jnp.stack(conv_s), jnp.stack(v_s))
```
